```python
import jax
import jax.numpy as jnp
from jax import lax
import numpy as np

D_MODEL = 1024
BATCH = 8
SEQ = 4096
DEPTH = 2

GRID_W = 64
CTX_LEN = 256
INNER = 2 * D_MODEL
NA_HEADS = 16
NA_DIM = D_MODEL // NA_HEADS
NA_WIDTH = NA_HEADS * NA_DIM
NA_WIN_H = 8
NA_WIN_W = 16
ML_HEADS = 4
ML_DIM = D_MODEL // ML_HEADS
ML_WIDTH = ML_HEADS * ML_DIM
ML_CHUNK = 64
HG_HEADS = 16
HG_KDIM = 128
HG_VDIM = INNER // HG_HEADS
HG_KW = HG_HEADS * HG_KDIM
HG_VW = HG_HEADS * HG_VDIM
HG_CHUNK = 32
ROPE_BASE = 10000.0
EPS = 1e-6
N_AB = (DEPTH + 1) // 2
N_C = DEPTH // 2
AB_SIZES = (NA_WIDTH,) * 4 + (ML_WIDTH,) * 5 + (4 * ML_HEADS,)
C_SIZES = (HG_KW, HG_KW, HG_KW, HG_VW, HG_VW)
AB_IN = sum(AB_SIZES)
C_IN = sum(C_SIZES)
F32 = jnp.float32

kernel_name = "hybrid_na_mlstm_hgrn2_dit"


def _rms(x, w):
    xf = x.astype(F32)
    y = xf * lax.rsqrt(jnp.mean(xf * xf, axis=-1, keepdims=True) + EPS)
    return (y * w.astype(F32)).astype(x.dtype)


def _heads(t, n):
    return t.reshape(t.shape[:-1] + (n, t.shape[-1] // n))


def _bht(a):
    return jnp.swapaxes(a, 1, 2).astype(F32)


def _split_cols(p, sizes):
    return jnp.split(p, [int(s) for s in np.cumsum(sizes)[:-1]], axis=-1)


def _same(a):
    return a


def _flip_t(a):
    return jnp.flip(a, axis=2)


def _ada(cvec, w, b):
    m = jax.nn.silu(cvec) @ w + b
    return jnp.split(m.reshape((-1, 1, m.shape[-1])), 3, axis=-1)


def _rope_1d(x, pos):
    half = x.shape[-1] // 2
    freqs = ROPE_BASE ** (-jnp.arange(half, dtype=F32) / half)
    ang = pos.astype(F32)[:, None] * freqs[None, :]
    cos, sin = jnp.cos(ang)[:, None, :], jnp.sin(ang)[:, None, :]
    x1, x2 = x[..., :half], x[..., half:]
    return jnp.concatenate([x1 * cos - x2 * sin, x2 * cos + x1 * sin], axis=-1)


def _axial_rope(x):
    t = jnp.arange(x.shape[1])
    half = x.shape[-1] // 2
    xf = x.astype(F32)
    out = jnp.concatenate([_rope_1d(xf[..., :half], t // GRID_W),
                           _rope_1d(xf[..., half:], t % GRID_W)], axis=-1)
    return out.astype(x.dtype)


def _to_chunks(a, L):
    B, H, T = a.shape[:3]
    return jnp.moveaxis(a.reshape((B, H, T // L, L) + a.shape[3:]), 2, 0)


def _from_chunks(a):
    a = jnp.moveaxis(a, 0, 2)
    return a.reshape(a.shape[:2] + (-1,) + a.shape[4:])


def _mlstm_chunkwise(q, k, v, ig, fg, state):
    L = ML_CHUNK
    lf = jax.nn.log_sigmoid(fg)
    tri = jnp.tril(jnp.ones((L, L), bool))

    def step(carry, xs):
        C, n, m = carry
        qc, kc, vc, ic, lfc = xs
        b = jnp.cumsum(lfc, axis=-1)
        dmat = jnp.where(tri, b[..., :, None] - b[..., None, :] + ic[..., None, :], -jnp.inf)
        m_inter = b + m[..., None]
        m_t = jnp.maximum(jnp.max(dmat, axis=-1), m_inter)
        s = jnp.einsum('bhtk,bhsk->bhts', qc, kc) * jnp.exp(dmat - m_t[..., None])
        carry_w = jnp.exp(m_inter - m_t)
        num = jnp.einsum('bhts,bhsv->bhtv', s, vc) + carry_w[..., None] * jnp.einsum('bhtk,bhkv->bhtv', qc, C)
        den = jnp.sum(s, axis=-1) + carry_w * jnp.einsum('bhtk,bhk->bht', qc, n)
        h = num / jnp.maximum(jnp.abs(den), jnp.exp(-m_t))[..., None]
        bL = b[..., -1]
        g = bL[..., None] - b + ic
        m_new = jnp.maximum(bL + m, jnp.max(g, axis=-1))
        kw = kc * jnp.exp(g - m_new[..., None])[..., None]
        dec = jnp.exp(bL + m - m_new)
        C = dec[..., None, None] * C + jnp.einsum('bhsk,bhsv->bhkv', kw, vc)
        n = dec[..., None] * n + jnp.sum(kw, axis=2)
        return (C, n, m_new), h

    xs = tuple(_to_chunks(a, L) for a in (q, k, v, ig, lf))
    state, h = lax.scan(step, state, xs)
    return _from_chunks(h), state


def _mlstm_final_state(k, v, ig, fg):
    b = jnp.cumsum(jax.nn.log_sigmoid(fg), axis=-1)
    g = b[..., -1:] - b + ig
    m = jnp.max(g, axis=-1)
    kw = k * jnp.exp(g - m[..., None])[..., None]
    return (jnp.einsum('bhsk,bhsv->bhkv', kw, v), jnp.sum(kw, axis=2), m)


def _mlstm_bidir(lat, ctx, need_ctx):
    q_l, k_l, v_l, g_l = lat
    q_c, k_c, v_c, g_c = ctx
    B, H, _, dk = q_l.shape
    dv = v_l.shape[-1]
    out_l, out_c = [], []
    for d, tr in enumerate((_same, _flip_t)):
        ic, fc = tr(g_c[..., 2 * d]), tr(g_c[..., 2 * d + 1])
        il, fl = tr(g_l[..., 2 * d]), tr(g_l[..., 2 * d + 1])
        if need_ctx:
            zero = (jnp.zeros((B, H, dk, dv), F32), jnp.zeros((B, H, dk), F32), jnp.zeros((B, H), F32))
            h_c, st = _mlstm_chunkwise(tr(q_c), tr(k_c), tr(v_c), ic, fc, zero)
            out_c.append(tr(h_c))
        else:
            st = _mlstm_final_state(tr(k_c), tr(v_c), ic, fc)
        h_l, _ = _mlstm_chunkwise(tr(q_l), tr(k_l), tr(v_l), il, fl, st)
        out_l.append(tr(h_l))
    return out_l[0] + out_l[1], (out_c[0] + out_c[1] if need_ctx else None)


def _gla_chunkwise(q, k, v, lf, S):
    L = HG_CHUNK
    tri = jnp.tril(jnp.ones((L, L), bool))[..., None]

    def step(S, xs):
        qc, kc, vc, lfc = xs
        b = jnp.cumsum(lfc, axis=2)
        o = jnp.einsum('bhtk,bhkv->bhtv', qc * jnp.exp(b), S)
        pair = jnp.exp(jnp.where(tri, b[:, :, :, None, :] - b[:, :, None, :, :], -jnp.inf))
        a = jnp.einsum('bhtk,bhtsk->bhts', qc, pair * kc[:, :, None, :, :])
        o = o + jnp.einsum('bhts,bhsv->bhtv', a, vc)
        bL = b[:, :, -1:, :]
        S = jnp.exp(bL[:, :, 0])[..., None] * S + jnp.einsum('bhsk,bhsv->bhkv', kc * jnp.exp(bL - b), vc)
        return S, o

    xs = tuple(_to_chunks(a, L) for a in (q, k, v, lf))
    S, o = lax.scan(step, S, xs)
    return _from_chunks(o), S


def _gla_final_state(k, v, lf):
    b = jnp.cumsum(lf, axis=2)
    return jnp.einsum('bhsk,bhsv->bhkv', k * jnp.exp(b[:, :, -1:] - b), v)


def _gla_bidir(lat, ctx, need_ctx):
    q_l, v_l, dirs_l = lat
    q_c, v_c, dirs_c = ctx
    B, H, _, dk = q_l.shape
    dv = v_l.shape[-1]
    out_l, out_c = [], []
    for d, tr in enumerate((_same, _flip_t)):
        (k_l, lf_l), (k_c, lf_c) = dirs_l[d], dirs_c[d]
        if need_ctx:
            o_c, S = _gla_chunkwise(tr(q_c), tr(k_c), tr(v_c), tr(lf_c), jnp.zeros((B, H, dk, dv), F32))
            out_c.append(tr(o_c))
        else:
            S = _gla_final_state(tr(k_c), tr(v_c), tr(lf_c))
        o_l, _ = _gla_chunkwise(tr(q_l), tr(k_l), tr(v_l), tr(lf_l), S)
        out_l.append(tr(o_l))
    return out_l[0] + out_l[1], (out_c[0] + out_c[1] if need_ctx else None)


def _neighbourhood_attention(q, k, v, k_ctx, v_ctx, rpb):
    B, T, H, d = q.shape
    rows = T // GRID_W
    kh, kw = min(NA_WIN_H, rows), NA_WIN_W
    qg = (q * (d ** -0.5)).reshape(B, rows, GRID_W, H, d)
    kg = k.reshape(B, rows, GRID_W, H, d)
    vg = v.reshape(B, rows, GRID_W, H, d)
    col = jnp.arange(GRID_W)
    col_idx = jnp.clip(col - kw // 2, 0, GRID_W - kw)[:, None] + jnp.arange(kw)[None, :]
    col_bias = rpb[:, :, col_idx - col[:, None] + NA_WIN_W - 1]

    def one_row(r):
        r0 = jnp.clip(r - kh // 2, 0, rows - kh)
        q_r = lax.dynamic_index_in_dim(qg, r, axis=1, keepdims=False)
        k_nb = lax.dynamic_slice_in_dim(kg, r0, kh, axis=1)[:, :, col_idx]
        v_nb = lax.dynamic_slice_in_dim(vg, r0, kh, axis=1)[:, :, col_idx]
        row_off = r0 + jnp.arange(kh) - r + NA_WIN_H - 1
        bias = jnp.transpose(jnp.take(col_bias, row_off, axis=1), (0, 2, 1, 3))
        s_nb = jnp.einsum('bqhd,baqwhd->bhqaw', q_r, k_nb).astype(F32) + bias[None].astype(F32)
        s_cx = jnp.einsum('bqhd,bchd->bhqc', q_r, k_ctx).astype(F32)
        p = jax.nn.softmax(jnp.concatenate([s_nb.reshape(B, H, GRID_W, kh * kw), s_cx], axis=-1), axis=-1)
        p = p.astype(v.dtype)
        p_nb = p[..., :kh * kw].reshape(B, H, GRID_W, kh, kw)
        return (jnp.einsum('bhqaw,baqwhd->bqhd', p_nb, v_nb)
                + jnp.einsum('bhqc,bchd->bqhd', p[..., kh * kw:], v_ctx))

    out = lax.map(one_row, jnp.arange(rows))
    return jnp.moveaxis(out, 0, 1).reshape(B, T, H, d)


def _context_attention(q, k, v):
    s = jnp.einsum('bqhd,bkhd->bhqk', q, k).astype(F32) * (q.shape[-1] ** -0.5)
    p = jax.nn.softmax(s, axis=-1).astype(v.dtype)
    return jnp.einsum('bhqk,bkhd->bqhd', p, v)


def _ab_mixer(h_lat, h_ctx, w_in, b_gate, q_norm, k_norm, rpb, h_norm, w_out, need_ctx):
    def project(h, rotary):
        qa, ka, va, za, qb, kb, vb, ob, zb, g = _split_cols(h @ w_in, AB_SIZES)
        qa = _rms(_heads(qa, NA_HEADS), q_norm)
        ka = _rms(_heads(ka, NA_HEADS), k_norm)
        qb, kb = _heads(qb, ML_HEADS), _heads(kb, ML_HEADS) * (ML_DIM ** -0.5)
        if rotary:
            qb, kb = _axial_rope(qb), _axial_rope(kb)
        gates = jnp.transpose(_heads(g + b_gate, 4), (0, 3, 1, 2)).astype(F32)
        mb = (_bht(qb), _bht(kb), _bht(_heads(vb, ML_HEADS)), gates)
        return (qa, ka, _heads(va, NA_HEADS), za), mb, (ob, zb)

    (qa_l, ka_l, va_l, za_l), mb_l, (ob_l, zb_l) = project(h_lat, True)
    (qa_c, ka_c, va_c, za_c), mb_c, (ob_c, zb_c) = project(h_ctx, False)
    oa_l = _neighbourhood_attention(qa_l, ka_l, va_l, ka_c, va_c, rpb)
    hb_l, hb_c = _mlstm_bidir(mb_l, mb_c, need_ctx)

    def merge(oa, hb, za, ob, zb):
        hb = jnp.swapaxes(hb, 1, 2).astype(ob.dtype) * jax.nn.sigmoid(_heads(ob, ML_HEADS))
        hb = _rms(hb, h_norm.reshape(ML_HEADS, ML_DIM))
        ya = oa.reshape(oa.shape[:2] + (NA_WIDTH,)) * jax.nn.silu(za)
        yb = hb.reshape(hb.shape[:2] + (ML_WIDTH,)) * jax.nn.silu(zb)
        return jnp.concatenate([ya, yb], axis=-1) @ w_out

    y_lat = merge(oa_l, hb_l, za_l, ob_l, zb_l)
    y_ctx = merge(_context_attention(qa_c, ka_c, va_c), hb_c, za_c, ob_c, zb_c) if need_ctx else None
    return y_lat, y_ctx


def _c_mixer(h_lat, h_ctx, w_in, lb, h_norm, w_out, need_ctx):
    def project(h):
        q, f_fwd, f_bwd, i, z = _split_cols(h @ w_in, C_SIZES)

        def decay(f_pre):
            f = lb + (1.0 - lb) * jax.nn.sigmoid(f_pre.astype(F32))
            return _bht(_heads(1.0 - f, HG_HEADS)), _bht(_heads(jnp.log(f), HG_HEADS))

        rec = (_bht(_heads(jax.nn.silu(q), HG_HEADS)), _bht(_heads(i, HG_HEADS)), (decay(f_fwd), decay(f_bwd)))
        return rec, z

    rec_l, z_l = project(h_lat)
    rec_c, z_c = project(h_ctx)
    o_l, o_c = _gla_bidir(rec_l, rec_c, need_ctx)

    def merge(o, z):
        o = _rms(jnp.swapaxes(o, 1, 2).astype(z.dtype), h_norm.reshape(HG_HEADS, HG_VDIM))
        return (o.reshape(o.shape[:2] + (HG_VW,)) * jax.nn.silu(z)) @ w_out

    return merge(o_l, z_l), (merge(o_c, z_c) if need_ctx else None)


def _lower_bound(lb_param, layer):
    s = jax.nn.softmax(lb_param.astype(F32), axis=0)
    return (jnp.cumsum(s, axis=0) - s[0])[layer]


def setup_inputs(seed: int = 0) -> dict:
    key = jax.random.key(seed)
    ks = jax.random.split(key, 22)

    def nrm(k, shape, s):
        return jax.random.normal(k, shape, F32) * s

    f_bias = jnp.linspace(3.0, 6.0, ML_HEADS, dtype=F32)
    b_gate_ab = jnp.concatenate([nrm(ks[8], (N_AB, ML_HEADS), 0.1),
                                 f_bias + nrm(ks[9], (N_AB, ML_HEADS), 0.1),
                                 nrm(ks[10], (N_AB, ML_HEADS), 0.1),
                                 f_bias + nrm(ks[11], (N_AB, ML_HEADS), 0.1)], axis=-1)
    return {
        "x": nrm(ks[0], (BATCH, SEQ, D_MODEL), 1.0),
        "c": nrm(ks[1], (BATCH, D_MODEL), 1.0),
        "ctx": nrm(ks[2], (BATCH, CTX_LEN, D_MODEL), 1.0),
        "c_ctx": nrm(ks[3], (D_MODEL,), 1.0),
        "norm_w": 1.0 + nrm(ks[4], (DEPTH, D_MODEL), 0.05),
        "w_ada": nrm(ks[5], (DEPTH, D_MODEL, 3 * D_MODEL), 0.5 * D_MODEL ** -0.5),
        "b_ada": nrm(ks[6], (DEPTH, 3 * D_MODEL), 0.02),
        "w_in_ab": nrm(ks[7], (N_AB, D_MODEL, AB_IN), D_MODEL ** -0.5),
        "b_gate_ab": b_gate_ab,
        "q_norm_a": 1.0 + nrm(ks[12], (N_AB, NA_DIM), 0.05),
        "k_norm_a": 1.0 + nrm(ks[13], (N_AB, NA_DIM), 0.05),
        "rpb_a": nrm(ks[14], (N_AB, NA_HEADS, 2 * NA_WIN_H - 1, 2 * NA_WIN_W - 1), 0.1),
        "h_norm_b": 1.0 + nrm(ks[15], (N_AB, ML_WIDTH), 0.05),
        "w_out_ab": nrm(ks[16], (N_AB, INNER, D_MODEL), INNER ** -0.5),
        "w_in_c": nrm(ks[17], (N_C, D_MODEL, C_IN), D_MODEL ** -0.5),
        "lb_c": nrm(ks[18], (DEPTH, HG_KW), 0.5),
        "h_norm_c": 1.0 + nrm(ks[19], (N_C, HG_VW), 0.05),
        "w_out_c": nrm(ks[20], (N_C, INNER, D_MODEL), INNER ** -0.5),
    }


def reference(x, c, ctx, c_ctx, norm_w, w_ada, b_ada, w_in_ab, b_gate_ab, q_norm_a, k_norm_a, rpb_a,
              h_norm_b, w_out_ab, w_in_c, lb_c, h_norm_c, w_out_c):
    for l in range(DEPTH):
        need_ctx = l < DEPTH - 1
        shift, scale, gate = _ada(c, w_ada[l], b_ada[l])
        shift_c, scale_c, gate_c = _ada(c_ctx, w_ada[l], b_ada[l])
        h_lat = _rms(x, norm_w[l]) * (1.0 + scale) + shift
        h_ctx = _rms(ctx, norm_w[l]) * (1.0 + scale_c) + shift_c
        j = l // 2
        if l % 2 == 0:
            y_lat, y_ctx = _ab_mixer(h_lat, h_ctx, w_in_ab[j], b_gate_ab[j], q_norm_a[j], k_norm_a[j],
                                     rpb_a[j], h_norm_b[j], w_out_ab[j], need_ctx)
        else:
            y_lat, y_ctx = _c_mixer(h_lat, h_ctx, w_in_c[j], _lower_bound(lb_c, l), h_norm_c[j],
                                    w_out_c[j], need_ctx)
        x = x + gate * y_lat
        if need_ctx:
            ctx = ctx + gate_c * y_ctx
    return x
```

```python
import functools

import numpy as np
import jax
import jax.numpy as jnp
from jax import lax
from jax.experimental import pallas as pl
from jax.experimental.pallas import tpu as pltpu

F32 = jnp.float32
BF16 = jnp.bfloat16
HIGHEST = lax.Precision.HIGHEST

GRID_W = 64
NA_HEADS = 16
NA_DIM = 64
NA_WIN_H = 8
NA_WIN_W = 16
ML_HEADS = 4
ML_DIM = 256
HG_HEADS = 16
HG_DIM = 128
ROPE_BASE = 10000.0
EPS = 1e-6
NEG = -1e30

LANES = 128
V7X_VMEM_LIMIT = 56 * 1024 * 1024

NA_QROWS = 4
NA_KROWS = NA_QROWS + NA_WIN_H - 1
ML_CHUNK = 256
HG_CHUNK = 64
HG_GROUP = 2

NT = (((1,), (1,)), ((), ()))
TN = (((0,), (0,)), ((), ()))


def _params(*sem):
    return pltpu.CompilerParams(dimension_semantics=sem, vmem_limit_bytes=V7X_VMEM_LIMIT)


def _silu(z):
    return z * jax.nn.sigmoid(z)


def _log_sigmoid(z):
    return jnp.minimum(z, 0.0) - jnp.log1p(jnp.exp(-jnp.abs(z)))


def _ada_kernel(c_ref, w_ref, b_ref, o_ref):
    s = _silu(c_ref[...])
    o_ref[...] = jnp.dot(s.astype(BF16), w_ref[...].astype(BF16), preferred_element_type=F32) + b_ref[...]


def _ada(cvec, w, b):
    R, D = cvec.shape
    N = w.shape[1]
    tn = D
    return pl.pallas_call(
        _ada_kernel, grid=(N // tn,),
        in_specs=[pl.BlockSpec((R, D), lambda n: (0, 0)),
                  pl.BlockSpec((D, tn), lambda n: (0, n)),
                  pl.BlockSpec((1, tn), lambda n: (0, n))],
        out_specs=pl.BlockSpec((R, tn), lambda n: (0, n)),
        out_shape=jax.ShapeDtypeStruct((R, N), F32),
        compiler_params=_params("arbitrary"), name="ada",
    )(cvec, w, b.reshape(1, N))


def _inproj_kernel(x_ref, nw_ref, mod_ref, w_ref, bias_ref, o_ref, hn_ref, *, n_ctx):
    i = pl.program_id(1)
    tm = x_ref.shape[1]

    @pl.when(pl.program_id(2) == 0)
    def _():
        x = x_ref[0]
        y = x * lax.rsqrt(jnp.mean(x * x, axis=-1, keepdims=True) + EPS) * nw_ref[...]
        row = i * tm + lax.broadcasted_iota(jnp.int32, (tm, 1), 0)
        is_ctx = row < n_ctx
        shift = jnp.where(is_ctx, mod_ref[0, 0:1, :], mod_ref[0, 2:3, :])
        scale = jnp.where(is_ctx, mod_ref[0, 1:2, :], mod_ref[0, 3:4, :])
        hn_ref[...] = (y * (1.0 + scale) + shift).astype(BF16)

    o_ref[0] = jnp.dot(hn_ref[...], w_ref[...], preferred_element_type=F32) + bias_ref[...]


def _inproj(xc, norm_w, mod, w, bias, n_ctx, tm, tn):
    B, S, D = xc.shape
    N = w.shape[1]
    return pl.pallas_call(
        functools.partial(_inproj_kernel, n_ctx=n_ctx),
        grid=(B, S // tm, N // tn),
        in_specs=[pl.BlockSpec((1, tm, D), lambda b, i, n: (b, i, 0)),
                  pl.BlockSpec((1, D), lambda b, i, n: (0, 0)),
                  pl.BlockSpec((1, 4, D), lambda b, i, n: (b, 0, 0)),
                  pl.BlockSpec((D, tn), lambda b, i, n: (0, n)),
                  pl.BlockSpec((1, tn), lambda b, i, n: (0, n))],
        out_specs=pl.BlockSpec((1, tm, tn), lambda b, i, n: (b, i, n)),
        out_shape=jax.ShapeDtypeStruct((B, S, N), F32),
        scratch_shapes=[pltpu.VMEM((tm, D), BF16)],
        compiler_params=_params("parallel", "parallel", "arbitrary"), name="inproj",
    )(xc, norm_w.reshape(1, D), mod, w, bias)


def _na_bias_tables(rpb, rows):
    qr = np.arange(NA_QROWS)[:, None, None, None]
    qc = np.arange(GRID_W)[None, :, None, None]
    ka = np.arange(NA_KROWS)[None, None, :, None]
    kc = np.arange(GRID_W)[None, None, None, :]
    nblk = rows // NA_QROWS
    tables = []
    for blk in (0, 1, nblk - 1):
        kr0 = int(np.clip(NA_QROWS * blk - NA_WIN_H // 2, 0, rows - NA_KROWS))
        r = NA_QROWS * blk + qr
        r0 = np.clip(r - NA_WIN_H // 2, 0, rows - NA_WIN_H)
        c0 = np.clip(qc - NA_WIN_W // 2, 0, GRID_W - NA_WIN_W)
        krow = kr0 + ka
        valid = (krow >= r0) & (krow < r0 + NA_WIN_H) & (kc >= c0) & (kc < c0 + NA_WIN_W)
        ri = np.clip(krow - r + NA_WIN_H - 1, 0, 2 * NA_WIN_H - 2)
        ci = np.clip(kc - qc + NA_WIN_W - 1, 0, 2 * NA_WIN_W - 2)
        shape = (NA_QROWS, GRID_W, NA_KROWS, GRID_W)
        valid = np.broadcast_to(valid, shape).reshape(NA_QROWS * GRID_W, NA_KROWS * GRID_W)
        ri = np.broadcast_to(ri, shape).reshape(valid.shape)
        ci = np.broadcast_to(ci, shape).reshape(valid.shape)
        tables.append(jnp.where(valid[None], rpb[:, ri, ci], NEG))
    tables.append(jnp.full_like(tables[0], NEG))
    return jnp.stack(tables).astype(F32)


def _na_kernel(q_ref, k_ref, v_ref, bias_ref, qn_ref, kn_ref, ones_ref, o_ref, kbf_ref, vbf_ref, *, n_ctx, rows):
    i = pl.program_id(2)
    nq = q_ref.shape[1]
    win = NA_KROWS * GRID_W
    lane = lax.broadcasted_iota(jnp.int32, (1, LANES), 1)

    def rms(x, w):
        ss = jnp.dot((x * x).astype(BF16), ones_ref[...], preferred_element_type=F32)
        return x * lax.rsqrt(ss * (1.0 / NA_DIM) + EPS) * w

    @pl.when(i == 0)
    def _():
        kbf_ref[...] = rms(k_ref[0], kn_ref[...]).astype(BF16)
        vbf_ref[...] = v_ref[0].astype(BF16)

    q = rms(q_ref[0], qn_ref[...]) * (NA_DIM ** -0.5)
    kr0 = jnp.clip(NA_QROWS * (i - 1) - NA_WIN_H // 2, 0, rows - NA_KROWS)
    start = pl.multiple_of(n_ctx + GRID_W * kr0, GRID_W)
    kw = kbf_ref[pl.ds(start, win), :]
    vw = vbf_ref[pl.ds(start, win), :]
    kc = kbf_ref[0:n_ctx, :]
    vc = vbf_ref[0:n_ctx, :]
    outs = []
    for h in range(LANES // NA_DIM):
        in_head = (lane >= NA_DIM * h) & (lane < NA_DIM * (h + 1))
        qh = jnp.where(in_head, q, 0.0).astype(BF16)
        s_nb = lax.dot_general(qh, kw, NT, preferred_element_type=F32) + bias_ref[0, h]
        s_cx = lax.dot_general(qh, kc, NT, preferred_element_type=F32)
        m = jnp.maximum(jnp.max(s_nb, axis=1, keepdims=True), jnp.max(s_cx, axis=1, keepdims=True))
        p_nb = jnp.exp(s_nb - m)
        p_cx = jnp.exp(s_cx - m)
        den = jnp.sum(p_nb, axis=1, keepdims=True) + jnp.sum(p_cx, axis=1, keepdims=True)
        o = (jnp.dot(p_nb.astype(BF16), vw, preferred_element_type=F32)
             + jnp.dot(p_cx.astype(BF16), vc, preferred_element_type=F32))
        outs.append(o / den)
    o_ref[0] = jnp.where(lane < NA_DIM, outs[0], outs[1])


def _na_attention(P, bias_tab, q_norm, k_norm, n_ctx):
    B, S, _ = P.shape
    width = NA_HEADS * NA_DIM
    npair = width // LANES
    nq = NA_QROWS * GRID_W
    assert n_ctx == nq and (S - n_ctx) % nq == 0
    nblk = S // nq
    rows = (S - n_ctx) // GRID_W
    win = NA_KROWS * GRID_W
    hp = LANES // NA_DIM
    ones = jnp.asarray(np.kron(np.eye(hp), np.ones((NA_DIM, NA_DIM))), BF16)
    qn = jnp.tile(q_norm, hp).reshape(1, LANES)
    kn = jnp.tile(k_norm, hp).reshape(1, LANES)

    def case(i):
        return jnp.where(i == 0, 3, jnp.where(i == 1, 0, jnp.where(i == nblk - 1, 2, 1)))

    return pl.pallas_call(
        functools.partial(_na_kernel, n_ctx=n_ctx, rows=rows),
        grid=(B, npair, nblk),
        in_specs=[pl.BlockSpec((1, nq, LANES), lambda b, p, i: (b, i, p)),
                  pl.BlockSpec((1, S, LANES), lambda b, p, i: (b, 0, npair + p)),
                  pl.BlockSpec((1, S, LANES), lambda b, p, i: (b, 0, 2 * npair + p)),
                  pl.BlockSpec((1, hp, nq, win), lambda b, p, i: (case(i), p, 0, 0)),
                  pl.BlockSpec((1, LANES), lambda b, p, i: (0, 0)),
                  pl.BlockSpec((1, LANES), lambda b, p, i: (0, 0)),
                  pl.BlockSpec((LANES, LANES), lambda b, p, i: (0, 0))],
        out_specs=pl.BlockSpec((1, nq, LANES), lambda b, p, i: (b, i, p)),
        out_shape=jax.ShapeDtypeStruct((B, S, width), F32),
        scratch_shapes=[pltpu.VMEM((S, LANES), BF16), pltpu.VMEM((S, LANES), BF16)],
        compiler_params=_params("parallel", "parallel", "arbitrary"), name="na_attention",
    )(P, P, P, bias_tab, qn, kn, ones)


def _rope_tables(n_ctx, T):
    half = ML_DIM // 4
    t = jnp.arange(T)
    freqs = ROPE_BASE ** (-jnp.arange(half, dtype=F32) / half)
    ang_r = (t // GRID_W).astype(F32)[:, None] * freqs[None, :]
    ang_c = (t % GRID_W).astype(F32)[:, None] * freqs[None, :]
    cos = jnp.concatenate([jnp.cos(ang_r)] * 2 + [jnp.cos(ang_c)] * 2, axis=-1)
    sin = jnp.concatenate([-jnp.sin(ang_r), jnp.sin(ang_r), -jnp.sin(ang_c), jnp.sin(ang_c)], axis=-1)
    cos = jnp.concatenate([jnp.ones((n_ctx, ML_DIM), F32), cos], axis=0)
    sin = jnp.concatenate([jnp.zeros((n_ctx, ML_DIM), F32), sin], axis=0)
    return cos, sin


def _rope(x, cos, sin):
    xr = jnp.concatenate([pltpu.roll(x[:, a * LANES:(a + 1) * LANES], LANES // 2, axis=1)
                          for a in range(x.shape[1] // LANES)], axis=1)
    return x * cos + xr * sin


def _mlstm_kernel(q_ref, k_ref, v_ref, g_ref, cos_ref, sin_ref, o_ref, c_ref, n_ref, m_ref, *, reverse):
    L = q_ref.shape[1]

    @pl.when(pl.program_id(1) == 0)
    def _():
        c_ref[...] = jnp.zeros_like(c_ref)
        n_ref[...] = jnp.zeros_like(n_ref)
        m_ref[...] = jnp.zeros_like(m_ref)

    t_idx = lax.broadcasted_iota(jnp.int32, (L, L), 0)
    s_idx = lax.broadcasted_iota(jnp.int32, (L, L), 1)
    mask = (s_idx >= t_idx) if reverse else (s_idx <= t_idx)
    mf = mask.astype(F32)
    last = 0 if reverse else L - 1

    g = g_ref[0]
    lsg = _log_sigmoid(g)
    g_t = g.T
    b_cols = jnp.dot(mf, lsg, precision=HIGHEST, preferred_element_type=F32)
    b_rows = lax.dot_general(lsg.T, mf, NT, precision=HIGHEST, preferred_element_type=F32)
    cos = cos_ref[...]
    sin = sin_ref[...]

    for h in range(ML_HEADS):
        ii = (2 if reverse else 0) * ML_HEADS + h
        fi = (3 if reverse else 1) * ML_HEADS + h
        ig_col, ig_row = g[:, ii:ii + 1], g_t[ii:ii + 1, :]
        b_col, b_row = b_cols[:, fi:fi + 1], b_rows[fi:fi + 1, :]
        b_last = b_col[last:last + 1, :]
        hs = slice(h * ML_DIM, (h + 1) * ML_DIM)
        q = _rope(q_ref[0, :, hs], cos, sin)
        k = _rope(k_ref[0, :, hs], cos, sin) * (ML_DIM ** -0.5)
        qb, kb, vb = q.astype(BF16), k.astype(BF16), v_ref[0, :, hs].astype(BF16)
        m_prev = m_ref[h, 0:1, 0:1]
        c_prev = c_ref[h]
        n_prev = n_ref[h]

        dmat = jnp.where(mask, b_col - b_row + ig_row, NEG)
        m_inter = b_col + m_prev
        m_t = jnp.maximum(jnp.max(dmat, axis=1, keepdims=True), m_inter)
        s = lax.dot_general(qb, kb, NT, preferred_element_type=F32) * jnp.exp(dmat - m_t)
        carry_w = jnp.exp(m_inter - m_t)
        num = (jnp.dot(s.astype(BF16), vb, preferred_element_type=F32)
               + carry_w * jnp.dot(qb, c_prev.astype(BF16), preferred_element_type=F32))
        den = jnp.sum(s, axis=1, keepdims=True) + carry_w * jnp.sum(q * n_prev, axis=1, keepdims=True)
        o_ref[0, :, hs] = num / jnp.maximum(jnp.abs(den), jnp.exp(-m_t))

        g_col = b_last - b_col + ig_col
        m_new = jnp.maximum(b_last + m_prev, jnp.max(g_col, axis=0, keepdims=True))
        kw = k * jnp.exp(g_col - m_new)
        dec = jnp.exp(b_last + m_prev - m_new)
        c_ref[h] = dec * c_prev + lax.dot_general(kw.astype(BF16), vb, TN, preferred_element_type=F32)
        n_ref[h] = dec * n_prev + jnp.sum(kw, axis=0, keepdims=True)
        m_ref[h] = jnp.broadcast_to(m_new, m_ref.shape[1:])


def _mlstm_scan(P, cos, sin, n_ctx, reverse):
    B, S, _ = P.shape
    L = ML_CHUNK
    width = ML_HEADS * ML_DIM
    assert n_ctx == L and S % L == 0
    nch = S // L
    gate_blk = 9 * width // LANES

    def cidx(j):
        return jnp.where(j == 0, 0, nch - j) if reverse else j

    return pl.pallas_call(
        functools.partial(_mlstm_kernel, reverse=reverse),
        grid=(B, nch),
        in_specs=[pl.BlockSpec((1, L, width), lambda b, j: (b, cidx(j), 4)),
                  pl.BlockSpec((1, L, width), lambda b, j: (b, cidx(j), 5)),
                  pl.BlockSpec((1, L, width), lambda b, j: (b, cidx(j), 6)),
                  pl.BlockSpec((1, L, LANES), lambda b, j: (b, cidx(j), gate_blk)),
                  pl.BlockSpec((L, ML_DIM), lambda b, j: (cidx(j), 0)),
                  pl.BlockSpec((L, ML_DIM), lambda b, j: (cidx(j), 0))],
        out_specs=pl.BlockSpec((1, L, width), lambda b, j: (b, cidx(j), 0)),
        out_shape=jax.ShapeDtypeStruct((B, S, width), F32),
        scratch_shapes=[pltpu.VMEM((ML_HEADS, ML_DIM, ML_DIM), F32),
                        pltpu.VMEM((ML_HEADS, 1, ML_DIM), F32),
                        pltpu.VMEM((ML_HEADS, 8, LANES), F32)],
        compiler_params=_params("parallel", "arbitrary"), name="mlstm_bwd" if reverse else "mlstm_fwd",
    )(P, P, P, P, cos, sin)


def _merge_ab_kernel(oa_ref, za_ref, hf_ref, hb_ref, ob_ref, zb_ref, x_ref, gate_ref, hn_ref, w_ref, o_ref, *, n_ctx):
    tm = x_ref.shape[1]
    width = oa_ref.shape[2]
    ya = oa_ref[0] * _silu(za_ref[0])
    hb = (hf_ref[0] + hb_ref[0]) * jax.nn.sigmoid(ob_ref[0])
    parts = []
    for h in range(ML_HEADS):
        seg = hb[:, h * ML_DIM:(h + 1) * ML_DIM]
        parts.append(seg * lax.rsqrt(jnp.mean(seg * seg, axis=1, keepdims=True) + EPS))
    yb = jnp.concatenate(parts, axis=1) * hn_ref[...] * _silu(zb_ref[0])
    y = (jnp.dot(ya.astype(BF16), w_ref[0:width, :], preferred_element_type=F32)
         + jnp.dot(yb.astype(BF16), w_ref[width:2 * width, :], preferred_element_type=F32))
    row = pl.program_id(1) * tm + lax.broadcasted_iota(jnp.int32, (tm, 1), 0)
    gate = jnp.where(row < n_ctx, gate_ref[0, 0:1, :], gate_ref[0, 1:2, :])
    o_ref[0] = x_ref[0] + gate * y


def _merge_ab(oa, hf, hb, P, xc, gate, h_norm, w_out, n_ctx, tm):
    B, S, D = xc.shape
    width = oa.shape[2]
    tok = lambda c: pl.BlockSpec((1, tm, width), lambda b, i: (b, i, c))
    return pl.pallas_call(
        functools.partial(_merge_ab_kernel, n_ctx=n_ctx),
        grid=(B, S // tm),
        in_specs=[tok(0), tok(3), tok(0), tok(0), tok(7), tok(8),
                  pl.BlockSpec((1, tm, D), lambda b, i: (b, i, 0)),
                  pl.BlockSpec((1, 2, D), lambda b, i: (b, 0, 0)),
                  pl.BlockSpec((1, width), lambda b, i: (0, 0)),
                  pl.BlockSpec((2 * width, D), lambda b, i: (0, 0))],
        out_specs=pl.BlockSpec((1, tm, D), lambda b, i: (b, i, 0)),
        out_shape=jax.ShapeDtypeStruct((B, S, D), F32),
        compiler_params=_params("parallel", "parallel"), name="merge_ab",
    )(oa, P, hf, hb, P, P, xc, gate, h_norm.reshape(1, width), w_out)


def _gla_kernel(q_ref, f_ref, v_ref, lb_ref, o_ref, st_ref, *, reverse):
    C = q_ref.shape[1]

    @pl.when(pl.program_id(2) == 0)
    def _():
        st_ref[...] = jnp.zeros_like(st_ref)

    t_idx = lax.broadcasted_iota(jnp.int32, (C, C), 0)
    s_idx = lax.broadcasted_iota(jnp.int32, (C, C), 1)
    mask = (s_idx >= t_idx) if reverse else (s_idx <= t_idx)
    mf = mask.astype(F32)
    last = 0 if reverse else C - 1

    for g in range(HG_GROUP):
        sl = slice(g * HG_DIM, (g + 1) * HG_DIM)
        lb = lb_ref[:, sl]
        f = lb + (1.0 - lb) * jax.nn.sigmoid(f_ref[0, :, sl])
        kk = 1.0 - f
        b = jnp.dot(mf, jnp.log(f), precision=HIGHEST, preferred_element_type=F32)
        b_last = b[last:last + 1, :]
        qt = (_silu(q_ref[0, :, sl]) * jnp.exp(b)).astype(BF16)
        kt = (kk * jnp.exp(-b)).astype(BF16)
        kh = (kk * jnp.exp(b_last - b)).astype(BF16)
        vb = v_ref[0, :, sl].astype(BF16)
        st = st_ref[g]
        a = jnp.where(mask, lax.dot_general(qt, kt, NT, preferred_element_type=F32), 0.0)
        o_ref[0, :, sl] = (lax.dot_general(qt, st.astype(BF16), NT, preferred_element_type=F32)
                           + jnp.dot(a.astype(BF16), vb, preferred_element_type=F32))
        st_ref[g] = st * jnp.exp(b_last) + lax.dot_general(vb, kh, TN, preferred_element_type=F32)


def _gla_scan(P, lb, n_ctx, reverse):
    B, S, _ = P.shape
    C = HG_CHUNK
    width = HG_HEADS * HG_DIM
    gw = HG_GROUP * HG_DIM
    ngrp = width // gw
    nch, nctx_ch = S // C, n_ctx // C
    fcol = (2 if reverse else 1) * ngrp

    def cidx(j):
        if not reverse:
            return j
        return jnp.where(j < nctx_ch, nctx_ch - 1 - j, nch + nctx_ch - 1 - j)

    def oidx(j):
        return jnp.maximum(cidx(j) - nctx_ch, 0) if not reverse else jnp.where(j < nctx_ch, nch - nctx_ch - 1,
                                                                                cidx(j) - nctx_ch)

    return pl.pallas_call(
        functools.partial(_gla_kernel, reverse=reverse),
        grid=(B, ngrp, nch),
        in_specs=[pl.BlockSpec((1, C, gw), lambda b, g, j: (b, cidx(j), g)),
                  pl.BlockSpec((1, C, gw), lambda b, g, j: (b, cidx(j), fcol + g)),
                  pl.BlockSpec((1, C, gw), lambda b, g, j: (b, cidx(j), 3 * ngrp + g)),
                  pl.BlockSpec((1, gw), lambda b, g, j: (0, g))],
        out_specs=pl.BlockSpec((1, C, gw), lambda b, g, j: (b, oidx(j), g)),
        out_shape=jax.ShapeDtypeStruct((B, S - n_ctx, width), F32),
        scratch_shapes=[pltpu.VMEM((HG_GROUP, HG_DIM, HG_DIM), F32)],
        compiler_params=_params("parallel", "parallel", "arbitrary"), name="gla_bwd" if reverse else "gla_fwd",
    )(P, P, P, lb)


def _merge_c_kernel(of_ref, ob_ref, z_ref, x_ref, gate_ref, hn_ref, w_ref, o_ref):
    o = of_ref[0] + ob_ref[0]
    parts = []
    for h in range(HG_HEADS):
        seg = o[:, h * HG_DIM:(h + 1) * HG_DIM]
        parts.append(seg * lax.rsqrt(jnp.mean(seg * seg, axis=1, keepdims=True) + EPS))
    y = jnp.concatenate(parts, axis=1) * hn_ref[...] * _silu(z_ref[0])
    o_ref[0] = x_ref[0] + gate_ref[0] * jnp.dot(y.astype(BF16), w_ref[...], preferred_element_type=F32)


def _merge_c(of, ob, P, xc, gate, h_norm, w_out, n_ctx, tm):
    B, T, width = of.shape
    D = xc.shape[2]
    off = n_ctx // tm
    return pl.pallas_call(
        _merge_c_kernel,
        grid=(B, T // tm),
        in_specs=[pl.BlockSpec((1, tm, width), lambda b, i: (b, i, 0)),
                  pl.BlockSpec((1, tm, width), lambda b, i: (b, i, 0)),
                  pl.BlockSpec((1, tm, width), lambda b, i: (b, i + off, 4)),
                  pl.BlockSpec((1, tm, D), lambda b, i: (b, i + off, 0)),
                  pl.BlockSpec((1, 1, D), lambda b, i: (b, 0, 0)),
                  pl.BlockSpec((1, width), lambda b, i: (0, 0)),
                  pl.BlockSpec((width, D), lambda b, i: (0, 0))],
        out_specs=pl.BlockSpec((1, tm, D), lambda b, i: (b, i, 0)),
        out_shape=jax.ShapeDtypeStruct((B, T, D), F32),
        compiler_params=_params("parallel", "parallel"), name="merge_c",
    )(of, ob, P, xc, gate, h_norm.reshape(1, width), w_out)


def _pad_cols(w, n):
    return jnp.pad(w, ((0, 0), (0, n - w.shape[1])))


def kernel(x, c, ctx, c_ctx, norm_w, w_ada, b_ada, w_in_ab, b_gate_ab, q_norm_a, k_norm_a, rpb_a, h_norm_b, w_out_ab,
           w_in_c, lb_c, h_norm_c, w_out_c):
    B, T, D = x.shape
    n_ctx = ctx.shape[1]
    S = n_ctx + T
    tm_in = S // 8
    xc = jnp.concatenate([ctx, x], axis=1)
    cvec = jnp.concatenate([c, c_ctx[None], jnp.zeros((16 - B - 1, D), F32)], axis=0)

    def modulation(l):
        m = _ada(cvec, w_ada[l], b_ada[l])
        shift, scale, gate = m[:, :D], m[:, D:2 * D], m[:, 2 * D:]
        ctx_row = lambda a: jnp.broadcast_to(a[B][None], (B, D))
        mod = jnp.stack([ctx_row(shift), ctx_row(scale), shift[:B], scale[:B]], axis=1)
        return mod, jnp.stack([ctx_row(gate), gate[:B]], axis=1)

    mod, gate = modulation(0)
    n_ab = w_in_ab.shape[2]
    n_pad = 76 * LANES
    w_in = _pad_cols(w_in_ab[0], n_pad).astype(BF16)
    bias = jnp.zeros((1, n_pad), F32).at[0, n_ab - 4 * ML_HEADS:n_ab].set(b_gate_ab[0])
    P = _inproj(xc, norm_w[0], mod, w_in, bias, n_ctx, tm_in, n_pad // 4)
    oa = _na_attention(P, _na_bias_tables(rpb_a[0], T // GRID_W), q_norm_a[0], k_norm_a[0], n_ctx)
    cos, sin = _rope_tables(n_ctx, T)
    hf = _mlstm_scan(P, cos, sin, n_ctx, reverse=False)
    hb = _mlstm_scan(P, cos, sin, n_ctx, reverse=True)
    xc = _merge_ab(oa, hf, hb, P, xc, gate, h_norm_b[0], w_out_ab[0].astype(BF16), n_ctx, tm_in // 2)

    mod, gate = modulation(1)
    sm = jax.nn.softmax(lb_c.astype(F32), axis=0)
    lb = (jnp.cumsum(sm, axis=0) - sm[0])[1].reshape(1, -1)
    n_c = w_in_c.shape[2]
    P = _inproj(xc, norm_w[1], mod, w_in_c[0].astype(BF16), jnp.zeros((1, n_c), F32), n_ctx, tm_in, n_c // 4)
    of = _gla_scan(P, lb, n_ctx, reverse=False)
    ob = _gla_scan(P, lb, n_ctx, reverse=True)
    return _merge_c(of, ob, P, xc, gate[:, 1:2], h_norm_c[0], w_out_c[0].astype(BF16), n_ctx, n_ctx)
```

```python
import functools

import numpy as np
import jax
import jax.numpy as jnp
from jax import lax
from jax.experimental import pallas as pl
from jax.experimental.pallas import tpu as pltpu

F32 = jnp.float32
BF16 = jnp.bfloat16
HIGHEST = lax.Precision.HIGHEST

GRID_W = 64
NA_HEADS = 16
NA_DIM = 64
NA_WIN_H = 8
NA_WIN_W = 16
ML_HEADS = 4
ML_DIM = 256
HG_HEADS = 16
HG_DIM = 128
ROPE_BASE = 10000.0
EPS = 1e-6
NEG = -1e30

LANES = 128
V7X_VMEM_LIMIT = 56 * 1024 * 1024

NA_QROWS = 4
NA_KROWS = NA_QROWS + NA_WIN_H - 1
ML_CHUNK = 256
HG_CHUNK = 64
HG_BLOCK = 256
HG_GROUP = 4

NT = (((1,), (1,)), ((), ()))
TN = (((0,), (0,)), ((), ()))


def _params(*sem):
    return pltpu.CompilerParams(dimension_semantics=sem, vmem_limit_bytes=V7X_VMEM_LIMIT)


def _silu(z):
    return z * jax.nn.sigmoid(z)


def _log_sigmoid(z):
    return jnp.minimum(z, 0.0) - jnp.log1p(jnp.exp(-jnp.abs(z)))


def _split3(x):
    hi = x.astype(BF16)
    r1 = x - hi.astype(F32)
    mid = r1.astype(BF16)
    return hi, mid, (r1 - mid.astype(F32)).astype(BF16)


def _cumsum_rows(tri_bf16, x):
    hi, mid, lo = _split3(x)
    return (jnp.dot(tri_bf16, hi, preferred_element_type=F32) + jnp.dot(tri_bf16, mid, preferred_element_type=F32)
            + jnp.dot(tri_bf16, lo, preferred_element_type=F32))


def _cumsum_cols(x, tri_bf16):
    hi, mid, lo = _split3(x)
    return (lax.dot_general(hi, tri_bf16, NT, preferred_element_type=F32)
            + lax.dot_general(mid, tri_bf16, NT, preferred_element_type=F32)
            + lax.dot_general(lo, tri_bf16, NT, preferred_element_type=F32))


def _ada_kernel(c_ref, w_ref, b_ref, o_ref):
    s = _silu(c_ref[...])
    o_ref[...] = jnp.dot(s.astype(BF16), w_ref[...].astype(BF16), preferred_element_type=F32) + b_ref[...]


def _ada(cvec, w, b):
    R, D = cvec.shape
    N = w.shape[1]
    tn = D
    return pl.pallas_call(
        _ada_kernel, grid=(N // tn,),
        in_specs=[pl.BlockSpec((R, D), lambda n: (0, 0)),
                  pl.BlockSpec((D, tn), lambda n: (0, n)),
                  pl.BlockSpec((1, tn), lambda n: (0, n))],
        out_specs=pl.BlockSpec((R, tn), lambda n: (0, n)),
        out_shape=jax.ShapeDtypeStruct((R, N), F32),
        compiler_params=_params("arbitrary"), name="ada",
    )(cvec, w, b.reshape(1, N))


def _inproj_kernel(x_ref, nw_ref, mod_ref, w_ref, bias_ref, o_ref, hn_ref, *, n_ctx):
    i = pl.program_id(1)
    tm = x_ref.shape[1]

    @pl.when(pl.program_id(2) == 0)
    def _():
        x = x_ref[0]
        y = x * lax.rsqrt(jnp.mean(x * x, axis=-1, keepdims=True) + EPS) * nw_ref[...]
        row = i * tm + lax.broadcasted_iota(jnp.int32, (tm, 1), 0)
        is_ctx = row < n_ctx
        shift = jnp.where(is_ctx, mod_ref[0, 0:1, :], mod_ref[0, 2:3, :])
        scale = jnp.where(is_ctx, mod_ref[0, 1:2, :], mod_ref[0, 3:4, :])
        hn_ref[...] = (y * (1.0 + scale) + shift).astype(BF16)

    o_ref[0] = jnp.dot(hn_ref[...], w_ref[...], preferred_element_type=F32) + bias_ref[...]


def _inproj(xc, norm_w, mod, w, bias, n_ctx, tm, tn):
    B, S, D = xc.shape
    N = w.shape[1]
    return pl.pallas_call(
        functools.partial(_inproj_kernel, n_ctx=n_ctx),
        grid=(B, S // tm, N // tn),
        in_specs=[pl.BlockSpec((1, tm, D), lambda b, i, n: (b, i, 0)),
                  pl.BlockSpec((1, D), lambda b, i, n: (0, 0)),
                  pl.BlockSpec((1, 4, D), lambda b, i, n: (b, 0, 0)),
                  pl.BlockSpec((D, tn), lambda b, i, n: (0, n)),
                  pl.BlockSpec((1, tn), lambda b, i, n: (0, n))],
        out_specs=pl.BlockSpec((1, tm, tn), lambda b, i, n: (b, i, n)),
        out_shape=jax.ShapeDtypeStruct((B, S, N), F32),
        scratch_shapes=[pltpu.VMEM((tm, D), BF16)],
        compiler_params=_params("parallel", "parallel", "arbitrary"), name="inproj",
    )(xc, norm_w.reshape(1, D), mod, w, bias)


def _na_bias_tables(rpb, rows):
    H = rpb.shape[0]
    qr, ka = np.arange(NA_QROWS)[:, None], np.arange(NA_KROWS)[None, :]
    qc, kc = np.arange(GRID_W)[:, None], np.arange(GRID_W)[None, :]
    c0 = np.clip(qc - NA_WIN_W // 2, 0, GRID_W - NA_WIN_W)
    col_ok = (kc >= c0) & (kc < c0 + NA_WIN_W)
    col_sel = ((kc - qc + NA_WIN_W - 1)[..., None] == np.arange(2 * NA_WIN_W - 1)) & col_ok[..., None]
    nblk = rows // NA_QROWS
    tables = []
    for blk in (0, 1, nblk - 1):
        kr0 = int(np.clip(NA_QROWS * blk - NA_WIN_H // 2, 0, rows - NA_KROWS))
        r = NA_QROWS * blk + qr
        r0 = np.clip(r - NA_WIN_H // 2, 0, rows - NA_WIN_H)
        krow = kr0 + ka
        row_ok = (krow >= r0) & (krow < r0 + NA_WIN_H)
        row_sel = ((krow - r + NA_WIN_H - 1)[..., None] == np.arange(2 * NA_WIN_H - 1)) & row_ok[..., None]
        tab = jnp.einsum("rai,hij,qkj->hrqak", row_sel.astype(np.float32), rpb.astype(F32),
                         col_sel.astype(np.float32), precision=HIGHEST)
        valid = row_ok[:, None, :, None] & col_ok[None, :, None, :]
        tables.append(jnp.where(valid[None], tab, NEG).reshape(H, NA_QROWS * GRID_W, NA_KROWS * GRID_W))
    tables.append(jnp.full_like(tables[0], NEG))
    return jnp.stack(tables).astype(F32)


def _na_kernel(q_ref, k_ref, v_ref, bias_ref, qn_ref, kn_ref, ones_ref, o_ref, kbf_ref, vbf_ref, *, n_ctx, rows):
    i = pl.program_id(2)
    nq = q_ref.shape[1]
    win = NA_KROWS * GRID_W
    lane = lax.broadcasted_iota(jnp.int32, (1, LANES), 1)

    def rms(x, w):
        ss = jnp.dot((x * x).astype(BF16), ones_ref[...], preferred_element_type=F32)
        return x * lax.rsqrt(ss * (1.0 / NA_DIM) + EPS) * w

    @pl.when(i == 0)
    def _():
        kbf_ref[...] = rms(k_ref[0], kn_ref[...]).astype(BF16)
        vbf_ref[...] = v_ref[0].astype(BF16)

    q = rms(q_ref[0], qn_ref[...]) * (NA_DIM ** -0.5)
    kr0 = jnp.clip(NA_QROWS * (i - 1) - NA_WIN_H // 2, 0, rows - NA_KROWS)
    start = pl.multiple_of(n_ctx + GRID_W * kr0, GRID_W)
    kw = kbf_ref[pl.ds(start, win), :]
    vw = vbf_ref[pl.ds(start, win), :]
    kc = kbf_ref[0:n_ctx, :]
    vc = vbf_ref[0:n_ctx, :]
    heads = range(LANES // NA_DIM)
    qh = [jnp.where((lane >= NA_DIM * h) & (lane < NA_DIM * (h + 1)), q, 0.0).astype(BF16) for h in heads]
    s_nb = [lax.dot_general(qh[h], kw, NT, preferred_element_type=F32) for h in heads]
    s_cx = [lax.dot_general(qh[h], kc, NT, preferred_element_type=F32) for h in heads]
    p_nb, p_cx, den = [], [], []
    for h in heads:
        sb = s_nb[h] + bias_ref[0, h]
        m = jnp.maximum(jnp.max(sb, axis=1, keepdims=True), jnp.max(s_cx[h], axis=1, keepdims=True))
        pn = jnp.exp(sb - m)
        pc = jnp.exp(s_cx[h] - m)
        den.append(jnp.sum(pn, axis=1, keepdims=True) + jnp.sum(pc, axis=1, keepdims=True))
        p_nb.append(pn.astype(BF16))
        p_cx.append(pc.astype(BF16))
    o = [jnp.dot(p_nb[h], vw, preferred_element_type=F32) + jnp.dot(p_cx[h], vc, preferred_element_type=F32)
         for h in heads]
    o_ref[0] = jnp.where(lane < NA_DIM, o[0] / den[0], o[1] / den[1])


def _na_attention(P, bias_tab, q_norm, k_norm, n_ctx):
    B, S, _ = P.shape
    width = NA_HEADS * NA_DIM
    npair = width // LANES
    nq = NA_QROWS * GRID_W
    assert n_ctx == nq and (S - n_ctx) % nq == 0
    nblk = S // nq
    rows = (S - n_ctx) // GRID_W
    win = NA_KROWS * GRID_W
    hp = LANES // NA_DIM
    ones = jnp.asarray(np.kron(np.eye(hp), np.ones((NA_DIM, NA_DIM))), BF16)
    qn = jnp.tile(q_norm, hp).reshape(1, LANES)
    kn = jnp.tile(k_norm, hp).reshape(1, LANES)

    def case(i):
        return jnp.where(i == 0, 3, jnp.where(i == 1, 0, jnp.where(i == nblk - 1, 2, 1)))

    return pl.pallas_call(
        functools.partial(_na_kernel, n_ctx=n_ctx, rows=rows),
        grid=(B, npair, nblk),
        in_specs=[pl.BlockSpec((1, nq, LANES), lambda b, p, i: (b, i, p)),
                  pl.BlockSpec((1, S, LANES), lambda b, p, i: (b, 0, npair + p)),
                  pl.BlockSpec((1, S, LANES), lambda b, p, i: (b, 0, 2 * npair + p)),
                  pl.BlockSpec((1, hp, nq, win), lambda b, p, i: (case(i), p, 0, 0)),
                  pl.BlockSpec((1, LANES), lambda b, p, i: (0, 0)),
                  pl.BlockSpec((1, LANES), lambda b, p, i: (0, 0)),
                  pl.BlockSpec((LANES, LANES), lambda b, p, i: (0, 0))],
        out_specs=pl.BlockSpec((1, nq, LANES), lambda b, p, i: (b, i, p)),
        out_shape=jax.ShapeDtypeStruct((B, S, width), F32),
        scratch_shapes=[pltpu.VMEM((S, LANES), BF16), pltpu.VMEM((S, LANES), BF16)],
        compiler_params=_params("parallel", "parallel", "arbitrary"), name="na_attention",
    )(P, P, P, bias_tab, qn, kn, ones)


def _rope_tables(n_ctx, T):
    half = ML_DIM // 4
    t = jnp.arange(T)
    freqs = ROPE_BASE ** (-jnp.arange(half, dtype=F32) / half)
    ang_r = (t // GRID_W).astype(F32)[:, None] * freqs[None, :]
    ang_c = (t % GRID_W).astype(F32)[:, None] * freqs[None, :]
    cos = jnp.concatenate([jnp.cos(ang_r)] * 2 + [jnp.cos(ang_c)] * 2, axis=-1)
    sin = jnp.concatenate([-jnp.sin(ang_r), jnp.sin(ang_r), -jnp.sin(ang_c), jnp.sin(ang_c)], axis=-1)
    cos = jnp.concatenate([jnp.ones((n_ctx, ML_DIM), F32), cos], axis=0)
    sin = jnp.concatenate([jnp.zeros((n_ctx, ML_DIM), F32), sin], axis=0)
    return cos, sin


def _rope(x, cos, sin):
    xr = jnp.concatenate([pltpu.roll(x[:, a * LANES:(a + 1) * LANES], LANES // 2, axis=1)
                          for a in range(x.shape[1] // LANES)], axis=1)
    return x * cos + xr * sin


def _mlstm_kernel(q_ref, k_ref, v_ref, g_ref, cos_ref, sin_ref, o_ref, c_ref, n_ref, m_ref, *, reverse):
    L = q_ref.shape[1]

    @pl.when(pl.program_id(1) == 0)
    def _():
        c_ref[...] = jnp.zeros_like(c_ref)
        n_ref[...] = jnp.zeros_like(n_ref)
        m_ref[...] = jnp.zeros_like(m_ref)

    t_idx = lax.broadcasted_iota(jnp.int32, (L, L), 0)
    s_idx = lax.broadcasted_iota(jnp.int32, (L, L), 1)
    mask = (s_idx >= t_idx) if reverse else (s_idx <= t_idx)
    tri = mask.astype(BF16)
    last = 0 if reverse else L - 1

    g = g_ref[0]
    lsg = _log_sigmoid(g)
    g_t = g.T
    b_cols = _cumsum_rows(tri, lsg)
    b_rows = _cumsum_cols(lsg.T, tri)
    cos = cos_ref[...]
    sin = sin_ref[...]

    heads = range(ML_HEADS)
    hs = [slice(h * ML_DIM, (h + 1) * ML_DIM) for h in heads]
    q, qb, kb, vb, kwb, decay, m_t, carry_w, m_new, dec, kw_sum = ([None] * ML_HEADS for _ in range(11))
    for h in heads:
        ii = (2 if reverse else 0) * ML_HEADS + h
        fi = (3 if reverse else 1) * ML_HEADS + h
        ig_col, ig_row = g[:, ii:ii + 1], g_t[ii:ii + 1, :]
        b_col, b_row = b_cols[:, fi:fi + 1], b_rows[fi:fi + 1, :]
        b_last = b_col[last:last + 1, :]
        q[h] = _rope(q_ref[0, :, hs[h]], cos, sin)
        k = _rope(k_ref[0, :, hs[h]], cos, sin) * (ML_DIM ** -0.5)
        qb[h], kb[h], vb[h] = q[h].astype(BF16), k.astype(BF16), v_ref[0, :, hs[h]].astype(BF16)
        m_prev = m_ref[h, 0:1, 0:1]
        dmat = jnp.where(mask, b_col - b_row + ig_row, NEG)
        m_inter = b_col + m_prev
        m_t[h] = jnp.maximum(jnp.max(dmat, axis=1, keepdims=True), m_inter)
        decay[h] = jnp.exp(dmat - m_t[h])
        carry_w[h] = jnp.exp(m_inter - m_t[h])
        g_col = b_last - b_col + ig_col
        m_new[h] = jnp.maximum(b_last + m_prev, jnp.max(g_col, axis=0, keepdims=True))
        kw = k * jnp.exp(g_col - m_new[h])
        kwb[h] = kw.astype(BF16)
        kw_sum[h] = jnp.sum(kw, axis=0, keepdims=True)
        dec[h] = jnp.exp(b_last + m_prev - m_new[h])

    qk = [lax.dot_general(qb[h], kb[h], NT, preferred_element_type=F32) for h in heads]
    qc = [jnp.dot(qb[h], c_ref[h].astype(BF16), preferred_element_type=F32) for h in heads]
    upd = [lax.dot_general(kwb[h], vb[h], TN, preferred_element_type=F32) for h in heads]
    s = [qk[h] * decay[h] for h in heads]
    sv = [jnp.dot(s[h].astype(BF16), vb[h], preferred_element_type=F32) for h in heads]
    for h in heads:
        n_prev = n_ref[h]
        num = sv[h] + carry_w[h] * qc[h]
        den = jnp.sum(s[h], axis=1, keepdims=True) + carry_w[h] * jnp.sum(q[h] * n_prev, axis=1, keepdims=True)
        o_ref[0, :, hs[h]] = num / jnp.maximum(jnp.abs(den), jnp.exp(-m_t[h]))
        c_ref[h] = dec[h] * c_ref[h] + upd[h]
        n_ref[h] = dec[h] * n_prev + kw_sum[h]
        m_ref[h] = jnp.broadcast_to(m_new[h], m_ref.shape[1:])


def _mlstm_scan(P, cos, sin, n_ctx, reverse):
    B, S, _ = P.shape
    L = ML_CHUNK
    width = ML_HEADS * ML_DIM
    assert n_ctx == L and S % L == 0
    nch = S // L
    gate_blk = 9 * width // LANES

    def cidx(j):
        return jnp.where(j == 0, 0, nch - j) if reverse else j

    return pl.pallas_call(
        functools.partial(_mlstm_kernel, reverse=reverse),
        grid=(B, nch),
        in_specs=[pl.BlockSpec((1, L, width), lambda b, j: (b, cidx(j), 4)),
                  pl.BlockSpec((1, L, width), lambda b, j: (b, cidx(j), 5)),
                  pl.BlockSpec((1, L, width), lambda b, j: (b, cidx(j), 6)),
                  pl.BlockSpec((1, L, LANES), lambda b, j: (b, cidx(j), gate_blk)),
                  pl.BlockSpec((L, ML_DIM), lambda b, j: (cidx(j), 0)),
                  pl.BlockSpec((L, ML_DIM), lambda b, j: (cidx(j), 0))],
        out_specs=pl.BlockSpec((1, L, width), lambda b, j: (b, cidx(j), 0)),
        out_shape=jax.ShapeDtypeStruct((B, S, width), F32),
        scratch_shapes=[pltpu.VMEM((ML_HEADS, ML_DIM, ML_DIM), F32),
                        pltpu.VMEM((ML_HEADS, 1, ML_DIM), F32),
                        pltpu.VMEM((ML_HEADS, 8, LANES), F32)],
        compiler_params=_params("parallel", "arbitrary"), name="mlstm_bwd" if reverse else "mlstm_fwd",
    )(P, P, P, P, cos, sin)


def _merge_ab_kernel(oa_ref, za_ref, hf_ref, hb_ref, ob_ref, zb_ref, x_ref, gate_ref, hn_ref, w_ref, o_ref, *, n_ctx):
    tm = x_ref.shape[1]
    width = oa_ref.shape[2]
    ya = oa_ref[0] * _silu(za_ref[0])
    hb = (hf_ref[0] + hb_ref[0]) * jax.nn.sigmoid(ob_ref[0])
    parts = []
    for h in range(ML_HEADS):
        seg = hb[:, h * ML_DIM:(h + 1) * ML_DIM]
        parts.append(seg * lax.rsqrt(jnp.mean(seg * seg, axis=1, keepdims=True) + EPS))
    yb = jnp.concatenate(parts, axis=1) * hn_ref[...] * _silu(zb_ref[0])
    y = (jnp.dot(ya.astype(BF16), w_ref[0:width, :], preferred_element_type=F32)
         + jnp.dot(yb.astype(BF16), w_ref[width:2 * width, :], preferred_element_type=F32))
    row = pl.program_id(1) * tm + lax.broadcasted_iota(jnp.int32, (tm, 1), 0)
    gate = jnp.where(row < n_ctx, gate_ref[0, 0:1, :], gate_ref[0, 1:2, :])
    o_ref[0] = x_ref[0] + gate * y


def _merge_ab(oa, hf, hb, P, xc, gate, h_norm, w_out, n_ctx, tm):
    B, S, D = xc.shape
    width = oa.shape[2]
    tok = lambda c: pl.BlockSpec((1, tm, width), lambda b, i: (b, i, c))
    return pl.pallas_call(
        functools.partial(_merge_ab_kernel, n_ctx=n_ctx),
        grid=(B, S // tm),
        in_specs=[tok(0), tok(3), tok(0), tok(0), tok(7), tok(8),
                  pl.BlockSpec((1, tm, D), lambda b, i: (b, i, 0)),
                  pl.BlockSpec((1, 2, D), lambda b, i: (b, 0, 0)),
                  pl.BlockSpec((1, width), lambda b, i: (0, 0)),
                  pl.BlockSpec((2 * width, D), lambda b, i: (0, 0))],
        out_specs=pl.BlockSpec((1, tm, D), lambda b, i: (b, i, 0)),
        out_shape=jax.ShapeDtypeStruct((B, S, D), F32),
        compiler_params=_params("parallel", "parallel"), name="merge_ab",
    )(oa, P, hf, hb, P, P, xc, gate, h_norm.reshape(1, width), w_out)


def _gla_kernel(q_ref, f_ref, v_ref, lb_ref, o_ref, st_ref, *, reverse):
    C = HG_CHUNK
    T = q_ref.shape[1]
    nsub = T // C
    gw = q_ref.shape[2]

    @pl.when(pl.program_id(2) == 0)
    def _():
        st_ref[...] = jnp.zeros_like(st_ref)

    def causal(n, same_chunk):
        t_idx = lax.broadcasted_iota(jnp.int32, (n, n), 0)
        s_idx = lax.broadcasted_iota(jnp.int32, (n, n), 1)
        m = (s_idx >= t_idx) if reverse else (s_idx <= t_idx)
        return m & (t_idx // C == s_idx // C) if same_chunk else m

    mask = causal(C, False)
    tri = causal(T, True).astype(BF16)
    last = 0 if reverse else C - 1
    mid = C // 2 if reverse else C // 2 - 1
    order = list(range(nsub - 1, -1, -1) if reverse else range(nsub))
    heads = [slice(g * HG_DIM, (g + 1) * HG_DIM) for g in range(HG_GROUP)]
    rows = [slice(c * C, (c + 1) * C) for c in range(nsub)]

    lb = lb_ref[...]
    f = lb + (1.0 - lb) * jax.nn.sigmoid(f_ref[0])
    b = _cumsum_rows(tri, jnp.log(f))
    b_mid = [b[c * C + mid:c * C + mid + 1, :] for c in range(nsub)]
    b_last = [b[c * C + last:c * C + last + 1, :] for c in range(nsub)]
    b_mid_rows = jnp.concatenate([jnp.broadcast_to(r, (C, gw)) for r in b_mid], axis=0)
    qm = _silu(q_ref[0]) * jnp.exp(b - b_mid_rows)
    km = (1.0 - f) * jnp.exp(b_mid_rows - b)
    e_mid = [jnp.exp(r) for r in b_mid]
    e_last_mid = [jnp.exp(l - r) for l, r in zip(b_last, b_mid)]
    dec = [jnp.exp(l) for l in b_last]
    qmb, kmb, vb = qm.astype(BF16), km.astype(BF16), v_ref[0].astype(BF16)

    a, upd = {}, {}
    for g, sl in enumerate(heads):
        for c, rs in enumerate(rows):
            s = lax.dot_general(qmb[rs, sl], kmb[rs, sl], NT, preferred_element_type=F32)
            a[g, c] = jnp.where(mask, s, 0.0).astype(BF16)
            kh = (km[rs, sl] * e_last_mid[c][:, sl]).astype(BF16)
            upd[g, c] = lax.dot_general(vb[rs, sl], kh, TN, preferred_element_type=F32)
    intra = {gc: jnp.dot(a[gc], vb[rows[gc[1]], heads[gc[0]]], preferred_element_type=F32) for gc in a}
    for g, sl in enumerate(heads):
        st = st_ref[g]
        for c in order:
            rs = rows[c]
            qi = (qm[rs, sl] * e_mid[c][:, sl]).astype(BF16)
            o_ref[0, rs, sl] = intra[g, c] + lax.dot_general(qi, st.astype(BF16), NT, preferred_element_type=F32)
            st = st * dec[c][:, sl] + upd[g, c]
        st_ref[g] = st


def _gla_scan(P, lb, n_ctx, reverse):
    B, S, _ = P.shape
    C = HG_BLOCK
    width = HG_HEADS * HG_DIM
    gw = HG_GROUP * HG_DIM
    ngrp = width // gw
    assert n_ctx % C == 0 and S % C == 0 and C % HG_CHUNK == 0
    nch, nctx_ch = S // C, n_ctx // C
    fcol = (2 if reverse else 1) * ngrp

    def cidx(j):
        if not reverse:
            return j
        return jnp.where(j < nctx_ch, nctx_ch - 1 - j, nch + nctx_ch - 1 - j)

    def oidx(j):
        return jnp.maximum(cidx(j) - nctx_ch, 0) if not reverse else jnp.where(j < nctx_ch, nch - nctx_ch - 1,
                                                                                cidx(j) - nctx_ch)

    return pl.pallas_call(
        functools.partial(_gla_kernel, reverse=reverse),
        grid=(B, ngrp, nch),
        in_specs=[pl.BlockSpec((1, C, gw), lambda b, g, j: (b, cidx(j), g)),
                  pl.BlockSpec((1, C, gw), lambda b, g, j: (b, cidx(j), fcol + g)),
                  pl.BlockSpec((1, C, gw), lambda b, g, j: (b, cidx(j), 3 * ngrp + g)),
                  pl.BlockSpec((1, gw), lambda b, g, j: (0, g))],
        out_specs=pl.BlockSpec((1, C, gw), lambda b, g, j: (b, oidx(j), g)),
        out_shape=jax.ShapeDtypeStruct((B, S - n_ctx, width), F32),
        scratch_shapes=[pltpu.VMEM((HG_GROUP, HG_DIM, HG_DIM), F32)],
        compiler_params=_params("parallel", "parallel", "arbitrary"), name="gla_bwd" if reverse else "gla_fwd",
    )(P, P, P, lb)


def _merge_c_kernel(of_ref, ob_ref, z_ref, x_ref, gate_ref, hn_ref, w_ref, o_ref):
    o = of_ref[0] + ob_ref[0]
    parts = []
    for h in range(HG_HEADS):
        seg = o[:, h * HG_DIM:(h + 1) * HG_DIM]
        parts.append(seg * lax.rsqrt(jnp.mean(seg * seg, axis=1, keepdims=True) + EPS))
    y = jnp.concatenate(parts, axis=1) * hn_ref[...] * _silu(z_ref[0])
    o_ref[0] = x_ref[0] + gate_ref[0] * jnp.dot(y.astype(BF16), w_ref[...], preferred_element_type=F32)


def _merge_c(of, ob, P, xc, gate, h_norm, w_out, n_ctx, tm):
    B, T, width = of.shape
    D = xc.shape[2]
    off = n_ctx // tm
    return pl.pallas_call(
        _merge_c_kernel,
        grid=(B, T // tm),
        in_specs=[pl.BlockSpec((1, tm, width), lambda b, i: (b, i, 0)),
                  pl.BlockSpec((1, tm, width), lambda b, i: (b, i, 0)),
                  pl.BlockSpec((1, tm, width), lambda b, i: (b, i + off, 4)),
                  pl.BlockSpec((1, tm, D), lambda b, i: (b, i + off, 0)),
                  pl.BlockSpec((1, 1, D), lambda b, i: (b, 0, 0)),
                  pl.BlockSpec((1, width), lambda b, i: (0, 0)),
                  pl.BlockSpec((width, D), lambda b, i: (0, 0))],
        out_specs=pl.BlockSpec((1, tm, D), lambda b, i: (b, i, 0)),
        out_shape=jax.ShapeDtypeStruct((B, T, D), F32),
        compiler_params=_params("parallel", "parallel"), name="merge_c",
    )(of, ob, P, xc, gate, h_norm.reshape(1, width), w_out)


def _pad_cols(w, n):
    return jnp.pad(w, ((0, 0), (0, n - w.shape[1])))


def kernel(x, c, ctx, c_ctx, norm_w, w_ada, b_ada, w_in_ab, b_gate_ab, q_norm_a, k_norm_a, rpb_a, h_norm_b, w_out_ab,
           w_in_c, lb_c, h_norm_c, w_out_c):
    B, T, D = x.shape
    n_ctx = ctx.shape[1]
    S = n_ctx + T
    tm_in = S // 8
    xc = jnp.concatenate([ctx, x], axis=1)
    cvec = jnp.concatenate([c, c_ctx[None], jnp.zeros((16 - B - 1, D), F32)], axis=0)

    def modulation(l):
        m = _ada(cvec, w_ada[l], b_ada[l])
        shift, scale, gate = m[:, :D], m[:, D:2 * D], m[:, 2 * D:]
        ctx_row = lambda a: jnp.broadcast_to(a[B][None], (B, D))
        mod = jnp.stack([ctx_row(shift), ctx_row(scale), shift[:B], scale[:B]], axis=1)
        return mod, jnp.stack([ctx_row(gate), gate[:B]], axis=1)

    mod, gate = modulation(0)
    n_ab = w_in_ab.shape[2]
    n_pad = 76 * LANES
    w_in = _pad_cols(w_in_ab[0], n_pad).astype(BF16)
    bias = jnp.zeros((1, n_pad), F32).at[0, n_ab - 4 * ML_HEADS:n_ab].set(b_gate_ab[0])
    P = _inproj(xc, norm_w[0], mod, w_in, bias, n_ctx, tm_in, n_pad // 4)
    oa = _na_attention(P, _na_bias_tables(rpb_a[0], T // GRID_W), q_norm_a[0], k_norm_a[0], n_ctx)
    cos, sin = _rope_tables(n_ctx, T)
    hf = _mlstm_scan(P, cos, sin, n_ctx, reverse=False)
    hb = _mlstm_scan(P, cos, sin, n_ctx, reverse=True)
    xc = _merge_ab(oa, hf, hb, P, xc, gate, h_norm_b[0], w_out_ab[0].astype(BF16), n_ctx, tm_in // 2)

    mod, gate = modulation(1)
    sm = jax.nn.softmax(lb_c.astype(F32), axis=0)
    lb = (jnp.cumsum(sm, axis=0) - sm[0])[1].reshape(1, -1)
    n_c = w_in_c.shape[2]
    P = _inproj(xc, norm_w[1], mod, w_in_c[0].astype(BF16), jnp.zeros((1, n_c), F32), n_ctx, tm_in, n_c // 4)
    of = _gla_scan(P, lb, n_ctx, reverse=False)
    ob = _gla_scan(P, lb, n_ctx, reverse=True)
    return _merge_c(of, ob, P, xc, gate[:, 1:2], h_norm_c[0], w_out_c[0].astype(BF16), n_ctx, n_ctx)
```

```python
import functools

import numpy as np
import jax
import jax.numpy as jnp
from jax import lax
from jax.experimental import pallas as pl
from jax.experimental.pallas import tpu as pltpu

F32 = jnp.float32
BF16 = jnp.bfloat16
HIGHEST = lax.Precision.HIGHEST

GRID_W = 64
NA_HEADS = 16
NA_DIM = 64
NA_WIN_H = 8
NA_WIN_W = 16
ML_HEADS = 4
ML_DIM = 256
HG_HEADS = 16
HG_DIM = 128
ROPE_BASE = 10000.0
EPS = 1e-6
NEG = -1e30
LOG2E = 1.4426950408889634

LANES = 128
V7X_VMEM_LIMIT = 56 * 1024 * 1024

NA_QROWS = 4
NA_KROWS = NA_QROWS + NA_WIN_H - 1
ML_CHUNK = 256
HG_CHUNK = 64
HG_BLOCK = 256
HG_GROUP = 8

NT = (((1,), (1,)), ((), ()))
TN = (((0,), (0,)), ((), ()))


def _params(*sem):
    return pltpu.CompilerParams(dimension_semantics=sem, vmem_limit_bytes=V7X_VMEM_LIMIT)


def _silu(z):
    return z * jax.nn.sigmoid(z)


def _log_sigmoid(z):
    return jnp.minimum(z, 0.0) - jnp.log1p(jnp.exp(-jnp.abs(z)))


def _split3(x):
    hi = x.astype(BF16)
    r1 = x - hi.astype(F32)
    mid = r1.astype(BF16)
    return hi, mid, (r1 - mid.astype(F32)).astype(BF16)


def _cumsum_rows(tri_bf16, x):
    hi, mid, lo = _split3(x)
    return (jnp.dot(tri_bf16, hi, preferred_element_type=F32) + jnp.dot(tri_bf16, mid, preferred_element_type=F32)
            + jnp.dot(tri_bf16, lo, preferred_element_type=F32))


def _cumsum_cols(x, tri_bf16):
    hi, mid, lo = _split3(x)
    return (lax.dot_general(hi, tri_bf16, NT, preferred_element_type=F32)
            + lax.dot_general(mid, tri_bf16, NT, preferred_element_type=F32)
            + lax.dot_general(lo, tri_bf16, NT, preferred_element_type=F32))


def _ada_kernel(c_ref, w_ref, b_ref, o_ref):
    s = _silu(c_ref[...])
    o_ref[...] = jnp.dot(s.astype(BF16), w_ref[...].astype(BF16), preferred_element_type=F32) + b_ref[...]


def _ada(cvec, w, b):
    R, D = cvec.shape
    N = w.shape[1]
    tn = D
    return pl.pallas_call(
        _ada_kernel, grid=(N // tn,),
        in_specs=[pl.BlockSpec((R, D), lambda n: (0, 0)),
                  pl.BlockSpec((D, tn), lambda n: (0, n)),
                  pl.BlockSpec((1, tn), lambda n: (0, n))],
        out_specs=pl.BlockSpec((R, tn), lambda n: (0, n)),
        out_shape=jax.ShapeDtypeStruct((R, N), F32),
        compiler_params=_params("arbitrary"), name="ada",
    )(cvec, w, b.reshape(1, N))


def _inproj_kernel(x_ref, nw_ref, mod_ref, w_ref, bias_ref, o_ref, hn_ref, *, n_ctx):
    i = pl.program_id(1)
    tm = x_ref.shape[1]

    @pl.when(pl.program_id(2) == 0)
    def _():
        x = x_ref[0]
        y = x * lax.rsqrt(jnp.mean(x * x, axis=-1, keepdims=True) + EPS) * nw_ref[...]
        row = i * tm + lax.broadcasted_iota(jnp.int32, (tm, 1), 0)
        is_ctx = row < n_ctx
        shift = jnp.where(is_ctx, mod_ref[0, 0:1, :], mod_ref[0, 2:3, :])
        scale = jnp.where(is_ctx, mod_ref[0, 1:2, :], mod_ref[0, 3:4, :])
        hn_ref[...] = (y * (1.0 + scale) + shift).astype(BF16)

    o_ref[0] = (jnp.dot(hn_ref[...], w_ref[...], preferred_element_type=F32) + bias_ref[...]).astype(o_ref.dtype)


def _inproj(xc, norm_w, mod, w, bias, n_ctx, tm, tn, out_dtype):
    B, S, D = xc.shape
    N = w.shape[1]
    return pl.pallas_call(
        functools.partial(_inproj_kernel, n_ctx=n_ctx),
        grid=(B, S // tm, N // tn),
        in_specs=[pl.BlockSpec((1, tm, D), lambda b, i, n: (b, i, 0)),
                  pl.BlockSpec((1, D), lambda b, i, n: (0, 0)),
                  pl.BlockSpec((1, 4, D), lambda b, i, n: (b, 0, 0)),
                  pl.BlockSpec((D, tn), lambda b, i, n: (0, n)),
                  pl.BlockSpec((1, tn), lambda b, i, n: (0, n))],
        out_specs=pl.BlockSpec((1, tm, tn), lambda b, i, n: (b, i, n)),
        out_shape=jax.ShapeDtypeStruct((B, S, N), out_dtype),
        scratch_shapes=[pltpu.VMEM((tm, D), BF16)],
        compiler_params=_params("parallel", "parallel", "arbitrary"), name="inproj",
    )(xc, norm_w.reshape(1, D), mod, w, bias)


def _na_bias_tables(rpb, rows):
    H = rpb.shape[0]
    qr, ka = np.arange(NA_QROWS)[:, None], np.arange(NA_KROWS)[None, :]
    qc, kc = np.arange(GRID_W)[:, None], np.arange(GRID_W)[None, :]
    c0 = np.clip(qc - NA_WIN_W // 2, 0, GRID_W - NA_WIN_W)
    col_ok = (kc >= c0) & (kc < c0 + NA_WIN_W)
    col_sel = ((kc - qc + NA_WIN_W - 1)[..., None] == np.arange(2 * NA_WIN_W - 1)) & col_ok[..., None]
    nblk = rows // NA_QROWS
    tables = []
    for blk in (0, 1, nblk - 1):
        kr0 = int(np.clip(NA_QROWS * blk - NA_WIN_H // 2, 0, rows - NA_KROWS))
        r = NA_QROWS * blk + qr
        r0 = np.clip(r - NA_WIN_H // 2, 0, rows - NA_WIN_H)
        krow = kr0 + ka
        row_ok = (krow >= r0) & (krow < r0 + NA_WIN_H)
        row_sel = ((krow - r + NA_WIN_H - 1)[..., None] == np.arange(2 * NA_WIN_H - 1)) & row_ok[..., None]
        tab = jnp.einsum("rai,hij,qkj->hrqak", row_sel.astype(np.float32), rpb.astype(F32),
                         col_sel.astype(np.float32), precision=HIGHEST)
        valid = row_ok[:, None, :, None] & col_ok[None, :, None, :]
        tables.append(jnp.where(valid[None], tab * LOG2E, NEG).reshape(H, NA_QROWS * GRID_W, NA_KROWS * GRID_W))
    return jnp.stack(tables).astype(F32)


def _na_kernel(q_ref, k_ref, v_ref, bias_ref, qn_ref, kn_ref, ones_ref, o_ref, k_s, q0_s, q1_s, v0_s, v1_s,
               *, n_ctx, rows):
    nq = NA_QROWS * GRID_W
    win = NA_KROWS * GRID_W
    nblk = rows // NA_QROWS
    lane = lax.broadcasted_iota(jnp.int32, (1, LANES), 1)
    in_h0 = lane < NA_DIM
    qh_s, vh_s = (q0_s, q1_s), (v0_s, v1_s)
    den_lane = (NA_DIM, 0)

    def rms(x, w):
        ss = jnp.dot((x * x).astype(BF16), ones_ref[...], preferred_element_type=F32)
        return x * lax.rsqrt(ss * (1.0 / NA_DIM) + EPS) * w

    k_s[...] = rms(k_ref[0].astype(F32), kn_ref[...]).astype(BF16)
    q = rms(q_ref[0].astype(F32), qn_ref[...]) * (NA_DIM ** -0.5 * LOG2E)
    q0_s[...] = jnp.where(in_h0, q, 0.0).astype(BF16)
    q1_s[...] = jnp.where(in_h0, 0.0, q).astype(BF16)
    v = v_ref[0].astype(F32)
    v0_s[...] = jnp.where(in_h0, v, jnp.where(lane == den_lane[0], 1.0, 0.0)).astype(BF16)
    v1_s[...] = jnp.where(in_h0, jnp.where(lane == den_lane[1], 1.0, 0.0), v).astype(BF16)

    def softmax_pv(h, s_parts, bias, v_starts):
        if bias is not None:
            s_parts = [s_parts[0] + bias] + s_parts[1:]
        m = functools.reduce(jnp.maximum, [jnp.max(s, axis=1, keepdims=True) for s in s_parts])
        return sum(jnp.dot(jnp.exp2(s - m).astype(BF16), vh_s[h][pl.ds(st, s.shape[1]), :],
                           preferred_element_type=F32) for s, st in zip(s_parts, v_starts))

    def store(q_start, o):
        den = [o[h][:, den_lane[h]:den_lane[h] + 1] for h in range(2)]
        o_ref[0, pl.ds(q_start, nq), :] = jnp.where(in_h0, o[0] / den[0], o[1] / den[1]).astype(o_ref.dtype)

    kc = k_s[0:n_ctx, :]
    s_ctx = [lax.dot_general(qh_s[h][0:n_ctx, :], kc, NT, preferred_element_type=F32) for h in range(2)]
    store(0, [softmax_pv(h, [s_ctx[h]], None, [0]) for h in range(2)])

    def block_pos(blk):
        q_start = pl.multiple_of(n_ctx + nq * blk, nq)
        kr0 = jnp.clip(NA_QROWS * blk - NA_WIN_H // 2, 0, rows - NA_KROWS)
        return q_start, pl.multiple_of(n_ctx + GRID_W * kr0, GRID_W)

    def logits(h, blk):
        q_start, k_start = block_pos(blk)
        qh = qh_s[h][pl.ds(q_start, nq), :]
        return [lax.dot_general(qh, k_s[pl.ds(k_start, win), :], NT, preferred_element_type=F32),
                lax.dot_general(qh, kc, NT, preferred_element_type=F32)]

    def attend(h, blk, s_parts):
        case = jnp.where(blk == 0, 0, jnp.where(blk == nblk - 1, 2, 1))
        return softmax_pv(h, s_parts, bias_ref[case, h], [block_pos(blk)[1], 0])

    def two_blocks(it, carry):
        b0, b1 = 2 * it, 2 * it + 1
        s00, s10, s01 = logits(0, b0), logits(1, b0), logits(0, b1)
        o00 = attend(0, b0, s00)
        s11 = logits(1, b1)
        o10 = attend(1, b0, s10)
        o01 = attend(0, b1, s01)
        o11 = attend(1, b1, s11)
        store(block_pos(b0)[0], [o00, o10])
        store(block_pos(b1)[0], [o01, o11])
        return carry

    lax.fori_loop(0, nblk // 2, two_blocks, 0)


def _na_attention(P, bias_tab, q_norm, k_norm, n_ctx):
    B, S, _ = P.shape
    width = NA_HEADS * NA_DIM
    npair = width // LANES
    nq = NA_QROWS * GRID_W
    rows = (S - n_ctx) // GRID_W
    assert n_ctx == nq and rows % (2 * NA_QROWS) == 0
    win = NA_KROWS * GRID_W
    hp = LANES // NA_DIM
    ones = jnp.asarray(np.kron(np.eye(hp), np.ones((NA_DIM, NA_DIM))), BF16)
    qn = jnp.tile(q_norm, hp).reshape(1, LANES)
    kn = jnp.tile(k_norm, hp).reshape(1, LANES)
    seq = lambda c: pl.BlockSpec((1, S, LANES), lambda b, p: (b, 0, c * npair + p))
    vec = pl.BlockSpec((1, LANES), lambda b, p: (0, 0))
    return pl.pallas_call(
        functools.partial(_na_kernel, n_ctx=n_ctx, rows=rows),
        grid=(B, npair),
        in_specs=[seq(0), seq(1), seq(2),
                  pl.BlockSpec((bias_tab.shape[0], hp, nq, win), lambda b, p: (0, p, 0, 0)),
                  vec, vec, pl.BlockSpec((LANES, LANES), lambda b, p: (0, 0))],
        out_specs=seq(0),
        out_shape=jax.ShapeDtypeStruct((B, S, width), BF16),
        scratch_shapes=[pltpu.VMEM((S, LANES), BF16)] * 5,
        compiler_params=_params("parallel", "parallel"), name="na_attention",
    )(P, P, P, bias_tab, qn, kn, ones)


def _rope_tables(n_ctx, T):
    half = ML_DIM // 4
    t = jnp.arange(T)
    freqs = ROPE_BASE ** (-jnp.arange(half, dtype=F32) / half)
    ang_r = (t // GRID_W).astype(F32)[:, None] * freqs[None, :]
    ang_c = (t % GRID_W).astype(F32)[:, None] * freqs[None, :]
    cos = jnp.concatenate([jnp.cos(ang_r)] * 2 + [jnp.cos(ang_c)] * 2, axis=-1)
    sin = jnp.concatenate([-jnp.sin(ang_r), jnp.sin(ang_r), -jnp.sin(ang_c), jnp.sin(ang_c)], axis=-1)
    cos = jnp.concatenate([jnp.ones((n_ctx, ML_DIM), F32), cos], axis=0)
    sin = jnp.concatenate([jnp.zeros((n_ctx, ML_DIM), F32), sin], axis=0)
    return cos, sin


def _rope(x, cos, sin):
    xr = jnp.concatenate([pltpu.roll(x[:, a * LANES:(a + 1) * LANES], LANES // 2, axis=1)
                          for a in range(x.shape[1] // LANES)], axis=1)
    return x * cos + xr * sin


def _mlstm_kernel(q_ref, k_ref, v_ref, g_ref, cos_ref, sin_ref, o_ref, c_ref, n_ref, m_ref, *, reverse):
    L = q_ref.shape[1]

    @pl.when(pl.program_id(1) == 0)
    def _():
        c_ref[...] = jnp.zeros_like(c_ref)
        n_ref[...] = jnp.zeros_like(n_ref)
        m_ref[...] = jnp.zeros_like(m_ref)

    t_idx = lax.broadcasted_iota(jnp.int32, (L, L), 0)
    s_idx = lax.broadcasted_iota(jnp.int32, (L, L), 1)
    mask = (s_idx >= t_idx) if reverse else (s_idx <= t_idx)
    tri = mask.astype(BF16)
    last = 0 if reverse else L - 1

    g = g_ref[0]
    lsg = _log_sigmoid(g)
    g_t = g.T
    b_cols = _cumsum_rows(tri, lsg)
    b_rows = _cumsum_cols(lsg.T, tri)
    cos = cos_ref[...]
    sin = sin_ref[...]

    heads = range(ML_HEADS)
    hs = [slice(h * ML_DIM, (h + 1) * ML_DIM) for h in heads]
    q, qb, kb, vb, kwb, decay, m_t, carry_w, m_new, dec, kw_sum = ([None] * ML_HEADS for _ in range(11))
    for h in heads:
        ii = (2 if reverse else 0) * ML_HEADS + h
        fi = (3 if reverse else 1) * ML_HEADS + h
        ig_col, ig_row = g[:, ii:ii + 1], g_t[ii:ii + 1, :]
        b_col, b_row = b_cols[:, fi:fi + 1], b_rows[fi:fi + 1, :]
        b_last = b_col[last:last + 1, :]
        q[h] = _rope(q_ref[0, :, hs[h]].astype(F32), cos, sin)
        k = _rope(k_ref[0, :, hs[h]].astype(F32), cos, sin) * (ML_DIM ** -0.5)
        qb[h], kb[h], vb[h] = q[h].astype(BF16), k.astype(BF16), v_ref[0, :, hs[h]]
        m_prev = m_ref[h, 0:1, 0:1]
        dmat = jnp.where(mask, b_col - b_row + ig_row, NEG)
        m_inter = b_col + m_prev
        m_t[h] = jnp.maximum(jnp.max(dmat, axis=1, keepdims=True), m_inter)
        decay[h] = jnp.exp(dmat - m_t[h])
        carry_w[h] = jnp.exp(m_inter - m_t[h])
        g_col = b_last - b_col + ig_col
        m_new[h] = jnp.maximum(b_last + m_prev, jnp.max(g_col, axis=0, keepdims=True))
        kw = k * jnp.exp(g_col - m_new[h])
        kwb[h] = kw.astype(BF16)
        kw_sum[h] = jnp.sum(kw, axis=0, keepdims=True)
        dec[h] = jnp.exp(b_last + m_prev - m_new[h])

    qk = [lax.dot_general(qb[h], kb[h], NT, preferred_element_type=F32) for h in heads]
    qc = [jnp.dot(qb[h], c_ref[h].astype(BF16), preferred_element_type=F32) for h in heads]
    upd = [lax.dot_general(kwb[h], vb[h], TN, preferred_element_type=F32) for h in heads]
    s = [qk[h] * decay[h] for h in heads]
    sv = [jnp.dot(s[h].astype(BF16), vb[h], preferred_element_type=F32) for h in heads]
    for h in heads:
        n_prev = n_ref[h]
        num = sv[h] + carry_w[h] * qc[h]
        den = jnp.sum(s[h], axis=1, keepdims=True) + carry_w[h] * jnp.sum(q[h] * n_prev, axis=1, keepdims=True)
        o_ref[0, :, hs[h]] = num / jnp.maximum(jnp.abs(den), jnp.exp(-m_t[h]))
        c_ref[h] = dec[h] * c_ref[h] + upd[h]
        n_ref[h] = dec[h] * n_prev + kw_sum[h]
        m_ref[h] = jnp.broadcast_to(m_new[h], m_ref.shape[1:])


def _mlstm_scan(P, gates, cos, sin, n_ctx, reverse):
    B, S, _ = P.shape
    L = ML_CHUNK
    width = ML_HEADS * ML_DIM
    assert n_ctx == L and S % L == 0
    nch = S // L

    def cidx(j):
        return jnp.where(j == 0, 0, nch - j) if reverse else j

    return pl.pallas_call(
        functools.partial(_mlstm_kernel, reverse=reverse),
        grid=(B, nch),
        in_specs=[pl.BlockSpec((1, L, width), lambda b, j: (b, cidx(j), 4)),
                  pl.BlockSpec((1, L, width), lambda b, j: (b, cidx(j), 5)),
                  pl.BlockSpec((1, L, width), lambda b, j: (b, cidx(j), 6)),
                  pl.BlockSpec((1, L, LANES), lambda b, j: (b, cidx(j), 0)),
                  pl.BlockSpec((L, ML_DIM), lambda b, j: (cidx(j), 0)),
                  pl.BlockSpec((L, ML_DIM), lambda b, j: (cidx(j), 0))],
        out_specs=pl.BlockSpec((1, L, width), lambda b, j: (b, cidx(j), 0)),
        out_shape=jax.ShapeDtypeStruct((B, S, width), F32),
        scratch_shapes=[pltpu.VMEM((ML_HEADS, ML_DIM, ML_DIM), F32),
                        pltpu.VMEM((ML_HEADS, 1, ML_DIM), F32),
                        pltpu.VMEM((ML_HEADS, 8, LANES), F32)],
        compiler_params=_params("parallel", "arbitrary"), name="mlstm_bwd" if reverse else "mlstm_fwd",
    )(P, P, P, gates, cos, sin)


def _merge_ab_kernel(oa_ref, za_ref, hf_ref, hb_ref, ob_ref, zb_ref, x_ref, gate_ref, hn_ref, w_ref, o_ref, *, n_ctx):
    tm = x_ref.shape[1]
    width = oa_ref.shape[2]
    ya = oa_ref[0].astype(F32) * _silu(za_ref[0].astype(F32))
    hb = (hf_ref[0] + hb_ref[0]) * jax.nn.sigmoid(ob_ref[0].astype(F32))
    parts = []
    for h in range(ML_HEADS):
        seg = hb[:, h * ML_DIM:(h + 1) * ML_DIM]
        parts.append(seg * lax.rsqrt(jnp.mean(seg * seg, axis=1, keepdims=True) + EPS))
    yb = jnp.concatenate(parts, axis=1) * hn_ref[...] * _silu(zb_ref[0].astype(F32))
    y = (jnp.dot(ya.astype(BF16), w_ref[0:width, :], preferred_element_type=F32)
         + jnp.dot(yb.astype(BF16), w_ref[width:2 * width, :], preferred_element_type=F32))
    row = pl.program_id(1) * tm + lax.broadcasted_iota(jnp.int32, (tm, 1), 0)
    gate = jnp.where(row < n_ctx, gate_ref[0, 0:1, :], gate_ref[0, 1:2, :])
    o_ref[0] = x_ref[0] + gate * y


def _merge_ab(oa, hf, hb, P, xc, gate, h_norm, w_out, n_ctx, tm):
    B, S, D = xc.shape
    width = oa.shape[2]
    tok = lambda c: pl.BlockSpec((1, tm, width), lambda b, i: (b, i, c))
    return pl.pallas_call(
        functools.partial(_merge_ab_kernel, n_ctx=n_ctx),
        grid=(B, S // tm),
        in_specs=[tok(0), tok(3), tok(0), tok(0), tok(7), tok(8),
                  pl.BlockSpec((1, tm, D), lambda b, i: (b, i, 0)),
                  pl.BlockSpec((1, 2, D), lambda b, i: (b, 0, 0)),
                  pl.BlockSpec((1, width), lambda b, i: (0, 0)),
                  pl.BlockSpec((2 * width, D), lambda b, i: (0, 0))],
        out_specs=pl.BlockSpec((1, tm, D), lambda b, i: (b, i, 0)),
        out_shape=jax.ShapeDtypeStruct((B, S, D), F32),
        compiler_params=_params("parallel", "parallel"), name="merge_ab",
    )(oa, P, hf, hb, P, P, xc, gate, h_norm.reshape(1, width), w_out)


def _gla_kernel(q_ref, f_ref, v_ref, lb_ref, o_ref, st_ref, *, reverse):
    C = HG_CHUNK
    T = q_ref.shape[1]
    nsub = T // C
    gw = q_ref.shape[2]

    @pl.when(pl.program_id(2) == 0)
    def _():
        st_ref[...] = jnp.zeros_like(st_ref)

    def causal(n, same_chunk):
        t_idx = lax.broadcasted_iota(jnp.int32, (n, n), 0)
        s_idx = lax.broadcasted_iota(jnp.int32, (n, n), 1)
        m = (s_idx >= t_idx) if reverse else (s_idx <= t_idx)
        return m & (t_idx // C == s_idx // C) if same_chunk else m

    mask = causal(C, False)
    tri = causal(T, True).astype(BF16)
    last = 0 if reverse else C - 1
    mid = C // 2 if reverse else C // 2 - 1
    order = list(range(nsub - 1, -1, -1) if reverse else range(nsub))
    heads = [slice(g * HG_DIM, (g + 1) * HG_DIM) for g in range(HG_GROUP)]
    rows = [slice(c * C, (c + 1) * C) for c in range(nsub)]

    lb = lb_ref[...]
    f = lb + (1.0 - lb) * jax.nn.sigmoid(f_ref[0])
    b = _cumsum_rows(tri, jnp.log(f))
    b_mid = [b[c * C + mid:c * C + mid + 1, :] for c in range(nsub)]
    b_last = [b[c * C + last:c * C + last + 1, :] for c in range(nsub)]
    b_mid_rows = jnp.concatenate([jnp.broadcast_to(r, (C, gw)) for r in b_mid], axis=0)
    qm = _silu(q_ref[0].astype(F32)) * jnp.exp(b - b_mid_rows)
    km = (1.0 - f) * jnp.exp(b_mid_rows - b)
    e_mid = [jnp.exp(r) for r in b_mid]
    e_last_mid = [jnp.exp(l - r) for l, r in zip(b_last, b_mid)]
    dec = [jnp.exp(l) for l in b_last]
    qmb, kmb, vb = qm.astype(BF16), km.astype(BF16), v_ref[0]

    a, upd = {}, {}
    for g, sl in enumerate(heads):
        for c, rs in enumerate(rows):
            s = lax.dot_general(qmb[rs, sl], kmb[rs, sl], NT, preferred_element_type=F32)
            a[g, c] = jnp.where(mask, s, 0.0).astype(BF16)
            kh = (km[rs, sl] * e_last_mid[c][:, sl]).astype(BF16)
            upd[g, c] = lax.dot_general(vb[rs, sl], kh, TN, preferred_element_type=F32)
    intra = {gc: jnp.dot(a[gc], vb[rows[gc[1]], heads[gc[0]]], preferred_element_type=F32) for gc in a}
    for g, sl in enumerate(heads):
        st = st_ref[g]
        for c in order:
            rs = rows[c]
            qi = (qm[rs, sl] * e_mid[c][:, sl]).astype(BF16)
            o_ref[0, rs, sl] = intra[g, c] + lax.dot_general(qi, st.astype(BF16), NT, preferred_element_type=F32)
            st = st * dec[c][:, sl] + upd[g, c]
        st_ref[g] = st


def _gla_scan(P, Pf, lb, n_ctx, reverse):
    B, S, _ = P.shape
    C = HG_BLOCK
    width = HG_HEADS * HG_DIM
    gw = HG_GROUP * HG_DIM
    ngrp = width // gw
    assert n_ctx % C == 0 and S % C == 0 and C % HG_CHUNK == 0
    nch, nctx_ch = S // C, n_ctx // C
    fcol = ngrp if reverse else 0

    def cidx(j):
        if not reverse:
            return j
        return jnp.where(j < nctx_ch, nctx_ch - 1 - j, nch + nctx_ch - 1 - j)

    def oidx(j):
        return jnp.maximum(cidx(j) - nctx_ch, 0) if not reverse else jnp.where(j < nctx_ch, nch - nctx_ch - 1,
                                                                                cidx(j) - nctx_ch)

    return pl.pallas_call(
        functools.partial(_gla_kernel, reverse=reverse),
        grid=(B, ngrp, nch),
        in_specs=[pl.BlockSpec((1, C, gw), lambda b, g, j: (b, cidx(j), g)),
                  pl.BlockSpec((1, C, gw), lambda b, g, j: (b, cidx(j), fcol + g)),
                  pl.BlockSpec((1, C, gw), lambda b, g, j: (b, cidx(j), ngrp + g)),
                  pl.BlockSpec((1, gw), lambda b, g, j: (0, g))],
        out_specs=pl.BlockSpec((1, C, gw), lambda b, g, j: (b, oidx(j), g)),
        out_shape=jax.ShapeDtypeStruct((B, S - n_ctx, width), F32),
        scratch_shapes=[pltpu.VMEM((HG_GROUP, HG_DIM, HG_DIM), F32)],
        compiler_params=_params("parallel", "parallel", "arbitrary"), name="gla_bwd" if reverse else "gla_fwd",
    )(P, Pf, P, lb)


def _merge_c_kernel(of_ref, ob_ref, z_ref, x_ref, gate_ref, hn_ref, w_ref, o_ref):
    o = of_ref[0] + ob_ref[0]
    parts = []
    for h in range(HG_HEADS):
        seg = o[:, h * HG_DIM:(h + 1) * HG_DIM]
        parts.append(seg * lax.rsqrt(jnp.mean(seg * seg, axis=1, keepdims=True) + EPS))
    y = jnp.concatenate(parts, axis=1) * hn_ref[...] * _silu(z_ref[0].astype(F32))
    o_ref[0] = x_ref[0] + gate_ref[0] * jnp.dot(y.astype(BF16), w_ref[...], preferred_element_type=F32)


def _merge_c(of, ob, P, xc, gate, h_norm, w_out, n_ctx, tm):
    B, T, width = of.shape
    D = xc.shape[2]
    off = n_ctx // tm
    return pl.pallas_call(
        _merge_c_kernel,
        grid=(B, T // tm),
        in_specs=[pl.BlockSpec((1, tm, width), lambda b, i: (b, i, 0)),
                  pl.BlockSpec((1, tm, width), lambda b, i: (b, i, 0)),
                  pl.BlockSpec((1, tm, width), lambda b, i: (b, i + off, 2)),
                  pl.BlockSpec((1, tm, D), lambda b, i: (b, i + off, 0)),
                  pl.BlockSpec((1, 1, D), lambda b, i: (b, 0, 0)),
                  pl.BlockSpec((1, width), lambda b, i: (0, 0)),
                  pl.BlockSpec((width, D), lambda b, i: (0, 0))],
        out_specs=pl.BlockSpec((1, tm, D), lambda b, i: (b, i, 0)),
        out_shape=jax.ShapeDtypeStruct((B, T, D), F32),
        compiler_params=_params("parallel", "parallel"), name="merge_c",
    )(of, ob, P, xc, gate, h_norm.reshape(1, width), w_out)


def _pad_cols(w, n):
    return jnp.pad(w, ((0, 0), (0, n - w.shape[1])))


def kernel(x, c, ctx, c_ctx, norm_w, w_ada, b_ada, w_in_ab, b_gate_ab, q_norm_a, k_norm_a, rpb_a, h_norm_b, w_out_ab,
           w_in_c, lb_c, h_norm_c, w_out_c):
    B, T, D = x.shape
    n_ctx = ctx.shape[1]
    S = n_ctx + T
    tm_in = S // 4
    tm_out = S // 16
    xc = jnp.concatenate([ctx, x], axis=1)
    cvec = jnp.concatenate([c, c_ctx[None], jnp.zeros((16 - B - 1, D), F32)], axis=0)

    def modulation(l):
        m = _ada(cvec, w_ada[l], b_ada[l])
        shift, scale, gate = m[:, :D], m[:, D:2 * D], m[:, 2 * D:]
        ctx_row = lambda a: jnp.broadcast_to(a[B][None], (B, D))
        mod = jnp.stack([ctx_row(shift), ctx_row(scale), shift[:B], scale[:B]], axis=1)
        return mod, jnp.stack([ctx_row(gate), gate[:B]], axis=1)

    mod, gate = modulation(0)
    n_main = 9 * NA_HEADS * NA_DIM
    w_main = w_in_ab[0][:, :n_main].astype(BF16)
    w_gate = _pad_cols(w_in_ab[0][:, n_main:], LANES).astype(BF16)
    b_gate = _pad_cols(b_gate_ab[0][None], LANES)
    P = _inproj(xc, norm_w[0], mod, w_main, jnp.zeros((1, n_main), F32), n_ctx, tm_in, n_main // 4, BF16)
    gates = _inproj(xc, norm_w[0], mod, w_gate, b_gate, n_ctx, tm_in, LANES, F32)
    oa = _na_attention(P, _na_bias_tables(rpb_a[0], T // GRID_W), q_norm_a[0], k_norm_a[0], n_ctx)
    cos, sin = _rope_tables(n_ctx, T)
    hf = _mlstm_scan(P, gates, cos, sin, n_ctx, reverse=False)
    hb = _mlstm_scan(P, gates, cos, sin, n_ctx, reverse=True)
    xc = _merge_ab(oa, hf, hb, P, xc, gate, h_norm_b[0], w_out_ab[0].astype(BF16), n_ctx, tm_out)

    mod, gate = modulation(1)
    sm = jax.nn.softmax(lb_c.astype(F32), axis=0)
    lb = (jnp.cumsum(sm, axis=0) - sm[0])[1].reshape(1, -1)
    hw = HG_HEADS * HG_DIM
    w_c = w_in_c[0].astype(BF16)
    w_qiz = jnp.concatenate([w_c[:, :hw], w_c[:, 3 * hw:]], axis=1)
    P = _inproj(xc, norm_w[1], mod, w_qiz, jnp.zeros((1, 3 * hw), F32), n_ctx, tm_in, hw, BF16)
    Pf = _inproj(xc, norm_w[1], mod, w_c[:, hw:3 * hw], jnp.zeros((1, 2 * hw), F32), n_ctx, tm_in, hw, F32)
    of = _gla_scan(P, Pf, lb, n_ctx, reverse=False)
    ob = _gla_scan(P, Pf, lb, n_ctx, reverse=True)
    return _merge_c(of, ob, P, xc, gate[:, 1:2], h_norm_c[0], w_out_c[0].astype(BF16), n_ctx, n_ctx)
```

```python
import functools

import numpy as np
import jax
import jax.numpy as jnp
from jax import lax
from jax.experimental import pallas as pl
from jax.experimental.pallas import tpu as pltpu

F32 = jnp.float32
BF16 = jnp.bfloat16
HIGHEST = lax.Precision.HIGHEST

GRID_W = 64
NA_HEADS = 16
NA_DIM = 64
NA_WIN_H = 8
NA_WIN_W = 16
ML_HEADS = 4
ML_DIM = 256
HG_HEADS = 16
HG_DIM = 128
ROPE_BASE = 10000.0
EPS = 1e-6
NEG = -1e30
LOG2E = 1.4426950408889634

LANES = 128
V7X_VMEM_LIMIT = 56 * 1024 * 1024

NA_QROWS = 4
NA_KROWS = NA_QROWS + NA_WIN_H - 1
ML_CHUNK = 256
HG_CHUNK = 64
HG_BLOCK = 256
HG_GROUP = 8

NT = (((1,), (1,)), ((), ()))
TN = (((0,), (0,)), ((), ()))


def _params(*sem):
    return pltpu.CompilerParams(dimension_semantics=sem, vmem_limit_bytes=V7X_VMEM_LIMIT)


def _sigmoid(z):
    return 0.5 * jnp.tanh(0.5 * z) + 0.5


def _silu(z):
    h = 0.5 * z
    return h + h * jnp.tanh(h)


def _log_sigmoid(z):
    return jnp.minimum(z, 0.0) - jnp.log1p(jnp.exp(-jnp.abs(z)))


def _split_bf16(x, terms):
    out = []
    for _ in range(terms - 1):
        out.append(x.astype(BF16))
        x = x - out[-1].astype(F32)
    return out + [x.astype(BF16)]


def _cumsum_rows(tri_bf16, x, terms=3):
    return sum(jnp.dot(tri_bf16, t, preferred_element_type=F32) for t in _split_bf16(x, terms))


def _cumsum_cols(x, tri_bf16, terms=3):
    return sum(lax.dot_general(t, tri_bf16, NT, preferred_element_type=F32) for t in _split_bf16(x, terms))


def _ada_kernel(c_ref, w_ref, b_ref, o_ref):
    s = _silu(c_ref[...])
    o_ref[...] = jnp.dot(s.astype(BF16), w_ref[...].astype(BF16), preferred_element_type=F32) + b_ref[...]


def _ada(cvec, w, b):
    R, D = cvec.shape
    N = w.shape[1]
    tn = D
    return pl.pallas_call(
        _ada_kernel, grid=(N // tn,),
        in_specs=[pl.BlockSpec((R, D), lambda n: (0, 0)),
                  pl.BlockSpec((D, tn), lambda n: (0, n)),
                  pl.BlockSpec((1, tn), lambda n: (0, n))],
        out_specs=pl.BlockSpec((R, tn), lambda n: (0, n)),
        out_shape=jax.ShapeDtypeStruct((R, N), F32),
        compiler_params=_params("arbitrary"), name="ada",
    )(cvec, w, b.reshape(1, N))


def _inproj_kernel(x_ref, nw_ref, mod_ref, w_ref, bias_ref, *rest, n_ctx, rope_tiles):
    if rope_tiles is None:
        o_ref, hn_ref = rest
    else:
        cos_ref, sin_ref, o_ref, hn_ref = rest
    i = pl.program_id(1)
    n = pl.program_id(2)
    tm = x_ref.shape[1]

    @pl.when(pl.program_id(2) == 0)
    def _():
        x = x_ref[0]
        y = x * lax.rsqrt(jnp.mean(x * x, axis=-1, keepdims=True) + EPS) * nw_ref[...]
        row = i * tm + lax.broadcasted_iota(jnp.int32, (tm, 1), 0)
        is_ctx = row < n_ctx
        shift = jnp.where(is_ctx, mod_ref[0, 0:1, :], mod_ref[0, 2:3, :])
        scale = jnp.where(is_ctx, mod_ref[0, 1:2, :], mod_ref[0, 3:4, :])
        hn_ref[...] = (y * (1.0 + scale) + shift).astype(BF16)

    acc = jnp.dot(hn_ref[...], w_ref[...], preferred_element_type=F32) + bias_ref[...]
    if rope_tiles is None:
        o_ref[0] = acc.astype(o_ref.dtype)
    else:
        q_tile, k_tile = rope_tiles
        rotated = (n == q_tile) | (n == k_tile)

        @pl.when(rotated)
        def _():
            cos, sin = cos_ref[...], sin_ref[...]
            scale = jnp.where(n == k_tile, ML_DIM ** -0.5, 1.0)
            for h in range(acc.shape[1] // ML_DIM):
                hs = slice(h * ML_DIM, (h + 1) * ML_DIM)
                o_ref[0, :, hs] = (_rope(acc[:, hs], cos, sin) * scale).astype(o_ref.dtype)

        @pl.when(jnp.logical_not(rotated))
        def _():
            o_ref[0] = acc.astype(o_ref.dtype)


def _inproj(xc, norm_w, mod, w, bias, n_ctx, tm, tn, out_dtype, rope=None):
    B, S, D = xc.shape
    N = w.shape[1]
    in_specs = [pl.BlockSpec((1, tm, D), lambda b, i, n: (b, i, 0)),
                pl.BlockSpec((1, D), lambda b, i, n: (0, 0)),
                pl.BlockSpec((1, 4, D), lambda b, i, n: (b, 0, 0)),
                pl.BlockSpec((D, tn), lambda b, i, n: (0, n)),
                pl.BlockSpec((1, tn), lambda b, i, n: (0, n))]
    args = [xc, norm_w.reshape(1, D), mod, w, bias]
    if rope is not None:
        assert tn == ML_HEADS * ML_DIM
        in_specs += [pl.BlockSpec((tm, ML_DIM), lambda b, i, n: (i, 0))] * 2
        args += list(rope[:2])
    return pl.pallas_call(
        functools.partial(_inproj_kernel, n_ctx=n_ctx, rope_tiles=None if rope is None else tuple(rope[2:])),
        grid=(B, S // tm, N // tn),
        in_specs=in_specs,
        out_specs=pl.BlockSpec((1, tm, tn), lambda b, i, n: (b, i, n)),
        out_shape=jax.ShapeDtypeStruct((B, S, N), out_dtype),
        scratch_shapes=[pltpu.VMEM((tm, D), BF16)],
        compiler_params=_params("parallel", "parallel", "arbitrary"), name="inproj",
    )(*args)


def _na_bias_tables(rpb, rows):
    H = rpb.shape[0]
    qr, ka = np.arange(NA_QROWS)[:, None], np.arange(NA_KROWS)[None, :]
    qc, kc = np.arange(GRID_W)[:, None], np.arange(GRID_W)[None, :]
    c0 = np.clip(qc - NA_WIN_W // 2, 0, GRID_W - NA_WIN_W)
    col_ok = (kc >= c0) & (kc < c0 + NA_WIN_W)
    col_sel = ((kc - qc + NA_WIN_W - 1)[..., None] == np.arange(2 * NA_WIN_W - 1)) & col_ok[..., None]
    nblk = rows // NA_QROWS
    tables = []
    for blk in (0, 1, nblk - 1):
        kr0 = int(np.clip(NA_QROWS * blk - NA_WIN_H // 2, 0, rows - NA_KROWS))
        r = NA_QROWS * blk + qr
        r0 = np.clip(r - NA_WIN_H // 2, 0, rows - NA_WIN_H)
        krow = kr0 + ka
        row_ok = (krow >= r0) & (krow < r0 + NA_WIN_H)
        row_sel = ((krow - r + NA_WIN_H - 1)[..., None] == np.arange(2 * NA_WIN_H - 1)) & row_ok[..., None]
        tab = jnp.einsum("rai,hij,qkj->hrqak", row_sel.astype(np.float32), rpb.astype(F32),
                         col_sel.astype(np.float32), precision=HIGHEST)
        valid = row_ok[:, None, :, None] & col_ok[None, :, None, :]
        tables.append(jnp.where(valid[None], tab * LOG2E, NEG).reshape(H, NA_QROWS * GRID_W, NA_KROWS * GRID_W))
    return jnp.stack(tables).astype(F32)


def _na_kernel(q_ref, k_ref, v_ref, bias_ref, qn_ref, kn_ref, ones_ref, o_ref, k_s, q0_s, q1_s, v0_s, v1_s,
               *, n_ctx, rows):
    nq = NA_QROWS * GRID_W
    win = NA_KROWS * GRID_W
    nblk = rows // NA_QROWS
    lane = lax.broadcasted_iota(jnp.int32, (1, LANES), 1)
    in_h0 = lane < NA_DIM
    qh_s, vh_s = (q0_s, q1_s), (v0_s, v1_s)
    den_lane = (NA_DIM, 0)

    def rms(x, w):
        ss = jnp.dot((x * x).astype(BF16), ones_ref[...], preferred_element_type=F32)
        return x * lax.rsqrt(ss * (1.0 / NA_DIM) + EPS) * w

    k_s[...] = rms(k_ref[0].astype(F32), kn_ref[...]).astype(BF16)
    q = rms(q_ref[0].astype(F32), qn_ref[...]) * (NA_DIM ** -0.5 * LOG2E)
    q0_s[...] = jnp.where(in_h0, q, 0.0).astype(BF16)
    q1_s[...] = jnp.where(in_h0, 0.0, q).astype(BF16)
    v = v_ref[0].astype(F32)
    v0_s[...] = jnp.where(in_h0, v, jnp.where(lane == den_lane[0], 1.0, 0.0)).astype(BF16)
    v1_s[...] = jnp.where(in_h0, jnp.where(lane == den_lane[1], 1.0, 0.0), v).astype(BF16)

    def softmax_pv(h, s_parts, bias, v_starts):
        if bias is not None:
            s_parts = [s_parts[0] + bias] + s_parts[1:]
        m = functools.reduce(jnp.maximum, [jnp.max(s, axis=1, keepdims=True) for s in s_parts])
        return sum(jnp.dot(jnp.exp2(s - m).astype(BF16), vh_s[h][pl.ds(st, s.shape[1]), :],
                           preferred_element_type=F32) for s, st in zip(s_parts, v_starts))

    def store(q_start, o):
        den = [o[h][:, den_lane[h]:den_lane[h] + 1] for h in range(2)]
        o_ref[0, pl.ds(q_start, nq), :] = jnp.where(in_h0, o[0] / den[0], o[1] / den[1]).astype(o_ref.dtype)

    kc = k_s[0:n_ctx, :]
    s_ctx = [lax.dot_general(qh_s[h][0:n_ctx, :], kc, NT, preferred_element_type=F32) for h in range(2)]
    store(0, [softmax_pv(h, [s_ctx[h]], None, [0]) for h in range(2)])

    def block_pos(blk):
        q_start = pl.multiple_of(n_ctx + nq * blk, nq)
        kr0 = jnp.clip(NA_QROWS * blk - NA_WIN_H // 2, 0, rows - NA_KROWS)
        return q_start, pl.multiple_of(n_ctx + GRID_W * kr0, GRID_W)

    def logits(h, blk):
        q_start, k_start = block_pos(blk)
        qh = qh_s[h][pl.ds(q_start, nq), :]
        return [lax.dot_general(qh, k_s[pl.ds(k_start, win), :], NT, preferred_element_type=F32),
                lax.dot_general(qh, kc, NT, preferred_element_type=F32)]

    def attend(h, blk, s_parts):
        case = jnp.where(blk == 0, 0, jnp.where(blk == nblk - 1, 2, 1))
        return softmax_pv(h, s_parts, bias_ref[case, h], [block_pos(blk)[1], 0])

    def two_blocks(it, carry):
        b0, b1 = 2 * it, 2 * it + 1
        s00, s10, s01 = logits(0, b0), logits(1, b0), logits(0, b1)
        o00 = attend(0, b0, s00)
        s11 = logits(1, b1)
        o10 = attend(1, b0, s10)
        o01 = attend(0, b1, s01)
        o11 = attend(1, b1, s11)
        store(block_pos(b0)[0], [o00, o10])
        store(block_pos(b1)[0], [o01, o11])
        return carry

    lax.fori_loop(0, nblk // 2, two_blocks, 0)


def _na_attention(P, bias_tab, q_norm, k_norm, n_ctx):
    B, S, _ = P.shape
    width = NA_HEADS * NA_DIM
    npair = width // LANES
    nq = NA_QROWS * GRID_W
    rows = (S - n_ctx) // GRID_W
    assert n_ctx == nq and rows % (2 * NA_QROWS) == 0
    win = NA_KROWS * GRID_W
    hp = LANES // NA_DIM
    ones = jnp.asarray(np.kron(np.eye(hp), np.ones((NA_DIM, NA_DIM))), BF16)
    qn = jnp.tile(q_norm, hp).reshape(1, LANES)
    kn = jnp.tile(k_norm, hp).reshape(1, LANES)
    seq = lambda c: pl.BlockSpec((1, S, LANES), lambda b, p: (b, 0, c * npair + p))
    vec = pl.BlockSpec((1, LANES), lambda b, p: (0, 0))
    return pl.pallas_call(
        functools.partial(_na_kernel, n_ctx=n_ctx, rows=rows),
        grid=(B, npair),
        in_specs=[seq(0), seq(1), seq(2),
                  pl.BlockSpec((bias_tab.shape[0], hp, nq, win), lambda b, p: (0, p, 0, 0)),
                  vec, vec, pl.BlockSpec((LANES, LANES), lambda b, p: (0, 0))],
        out_specs=seq(0),
        out_shape=jax.ShapeDtypeStruct((B, S, width), BF16),
        scratch_shapes=[pltpu.VMEM((S, LANES), BF16)] * 5,
        compiler_params=_params("parallel", "parallel"), name="na_attention",
    )(P, P, P, bias_tab, qn, kn, ones)


def _rope_tables(n_ctx, T):
    half = ML_DIM // 4
    t = jnp.arange(T)
    freqs = ROPE_BASE ** (-jnp.arange(half, dtype=F32) / half)
    ang_r = (t // GRID_W).astype(F32)[:, None] * freqs[None, :]
    ang_c = (t % GRID_W).astype(F32)[:, None] * freqs[None, :]
    cos = jnp.concatenate([jnp.cos(ang_r)] * 2 + [jnp.cos(ang_c)] * 2, axis=-1)
    sin = jnp.concatenate([-jnp.sin(ang_r), jnp.sin(ang_r), -jnp.sin(ang_c), jnp.sin(ang_c)], axis=-1)
    cos = jnp.concatenate([jnp.ones((n_ctx, ML_DIM), F32), cos], axis=0)
    sin = jnp.concatenate([jnp.zeros((n_ctx, ML_DIM), F32), sin], axis=0)
    return cos, sin


def _rope(x, cos, sin):
    xr = jnp.concatenate([pltpu.roll(x[:, a * LANES:(a + 1) * LANES], LANES // 2, axis=1)
                          for a in range(x.shape[1] // LANES)], axis=1)
    return x * cos + xr * sin


def _mlstm_kernel(q_ref, k_ref, v_ref, g_ref, o_ref, c_ref, n_ref, m_ref, *, reverse):
    L = q_ref.shape[1]

    @pl.when(pl.program_id(1) == 0)
    def _():
        c_ref[...] = jnp.zeros_like(c_ref)
        n_ref[...] = jnp.zeros_like(n_ref)
        m_ref[...] = jnp.zeros_like(m_ref)

    t_idx = lax.broadcasted_iota(jnp.int32, (L, L), 0)
    s_idx = lax.broadcasted_iota(jnp.int32, (L, L), 1)
    mask = (s_idx >= t_idx) if reverse else (s_idx <= t_idx)
    tri = mask.astype(BF16)
    last = 0 if reverse else L - 1

    g = g_ref[0]
    lsg = _log_sigmoid(g)
    g_t = g.T
    b_cols = _cumsum_rows(tri, lsg)
    b_rows = _cumsum_cols(lsg.T, tri)
    ones = jnp.ones((L, LANES), BF16)

    heads = range(ML_HEADS)
    hs = [slice(h * ML_DIM, (h + 1) * ML_DIM) for h in heads]
    qb, kb, vb, kwb, decay, m_t, carry_w, m_new, dec = ([None] * ML_HEADS for _ in range(9))
    for h in heads:
        ii = (2 if reverse else 0) * ML_HEADS + h
        fi = (3 if reverse else 1) * ML_HEADS + h
        ig_col, ig_row = g[:, ii:ii + 1], g_t[ii:ii + 1, :]
        b_col, b_row = b_cols[:, fi:fi + 1], b_rows[fi:fi + 1, :]
        b_last = b_col[last:last + 1, :]
        qb[h], kb[h], vb[h] = q_ref[0, :, hs[h]], k_ref[0, :, hs[h]], v_ref[0, :, hs[h]]
        m_prev = m_ref[h, 0:1, 0:1]
        dmat = jnp.where(mask, b_col - b_row + ig_row, NEG)
        m_inter = b_col + m_prev
        m_t[h] = jnp.maximum(jnp.max(dmat, axis=1, keepdims=True), m_inter)
        decay[h] = jnp.exp(dmat - m_t[h])
        carry_w[h] = jnp.broadcast_to(jnp.exp(m_inter - m_t[h]), (L, LANES))
        g_col = b_last - b_col + ig_col
        m_new[h] = jnp.maximum(b_last + m_prev, jnp.max(g_col, axis=0, keepdims=True))
        kwb[h] = (kb[h].astype(F32) * jnp.exp(g_col - m_new[h])).astype(BF16)
        dec[h] = jnp.exp(b_last + m_prev - m_new[h])

    qk = [lax.dot_general(qb[h], kb[h], NT, preferred_element_type=F32) for h in heads]
    qc = [jnp.dot(qb[h], c_ref[h].astype(BF16), preferred_element_type=F32) for h in heads]
    qn = [jnp.dot(qb[h], n_ref[h].astype(BF16), preferred_element_type=F32) for h in heads]
    upd = [lax.dot_general(kwb[h], vb[h], TN, preferred_element_type=F32) for h in heads]
    kw_sum = [lax.dot_general(kwb[h], ones, TN, preferred_element_type=F32) for h in heads]
    sb = [(qk[h] * decay[h]).astype(BF16) for h in heads]
    sv = [jnp.dot(sb[h], vb[h], preferred_element_type=F32) for h in heads]
    s_sum = [jnp.dot(sb[h], ones, preferred_element_type=F32) for h in heads]
    for h in heads:
        den = s_sum[h] + carry_w[h] * qn[h]
        inv = 1.0 / jnp.maximum(jnp.abs(den), jnp.exp(-m_t[h]))
        num = sv[h] + jnp.concatenate([carry_w[h]] * (ML_DIM // LANES), axis=1) * qc[h]
        o_ref[0, :, hs[h]] = (num * jnp.concatenate([inv] * (ML_DIM // LANES), axis=1)).astype(o_ref.dtype)
        c_ref[h] = dec[h] * c_ref[h] + upd[h]
        n_ref[h] = dec[h] * n_ref[h] + kw_sum[h]
        m_ref[h] = jnp.broadcast_to(m_new[h], m_ref.shape[1:])


def _mlstm_scan(P, gates, n_ctx, reverse):
    B, S, _ = P.shape
    L = ML_CHUNK
    width = ML_HEADS * ML_DIM
    assert n_ctx == L and S % L == 0
    nch = S // L

    def cidx(j):
        return jnp.where(j == 0, 0, nch - j) if reverse else j

    return pl.pallas_call(
        functools.partial(_mlstm_kernel, reverse=reverse),
        grid=(B, nch),
        in_specs=[pl.BlockSpec((1, L, width), lambda b, j: (b, cidx(j), 4)),
                  pl.BlockSpec((1, L, width), lambda b, j: (b, cidx(j), 5)),
                  pl.BlockSpec((1, L, width), lambda b, j: (b, cidx(j), 6)),
                  pl.BlockSpec((1, L, LANES), lambda b, j: (b, cidx(j), 0))],
        out_specs=pl.BlockSpec((1, L, width), lambda b, j: (b, cidx(j), 0)),
        out_shape=jax.ShapeDtypeStruct((B, S, width), BF16),
        scratch_shapes=[pltpu.VMEM((ML_HEADS, ML_DIM, ML_DIM), F32),
                        pltpu.VMEM((ML_HEADS, ML_DIM, LANES), F32),
                        pltpu.VMEM((ML_HEADS, 8, LANES), F32)],
        compiler_params=_params("parallel", "arbitrary"), name="mlstm_bwd" if reverse else "mlstm_fwd",
    )(P, P, P, gates)


def _merge_ab_kernel(oa_ref, za_ref, hf_ref, hb_ref, ob_ref, zb_ref, x_ref, gate_ref, hn_ref, w_ref, o_ref, *, n_ctx):
    tm = x_ref.shape[1]
    width = oa_ref.shape[2]
    ya = oa_ref[0].astype(F32) * _silu(za_ref[0].astype(F32))
    hb = (hf_ref[0].astype(F32) + hb_ref[0].astype(F32)) * _sigmoid(ob_ref[0].astype(F32))
    parts = []
    for h in range(ML_HEADS):
        seg = hb[:, h * ML_DIM:(h + 1) * ML_DIM]
        parts.append(seg * lax.rsqrt(jnp.mean(seg * seg, axis=1, keepdims=True) + EPS))
    yb = jnp.concatenate(parts, axis=1) * hn_ref[...] * _silu(zb_ref[0].astype(F32))
    y = (jnp.dot(ya.astype(BF16), w_ref[0:width, :], preferred_element_type=F32)
         + jnp.dot(yb.astype(BF16), w_ref[width:2 * width, :], preferred_element_type=F32))
    row = pl.program_id(1) * tm + lax.broadcasted_iota(jnp.int32, (tm, 1), 0)
    gate = jnp.where(row < n_ctx, gate_ref[0, 0:1, :], gate_ref[0, 1:2, :])
    o_ref[0] = x_ref[0] + gate * y


def _merge_ab(oa, hf, hb, P, xc, gate, h_norm, w_out, n_ctx, tm):
    B, S, D = xc.shape
    width = oa.shape[2]
    tok = lambda c: pl.BlockSpec((1, tm, width), lambda b, i: (b, i, c))
    return pl.pallas_call(
        functools.partial(_merge_ab_kernel, n_ctx=n_ctx),
        grid=(B, S // tm),
        in_specs=[tok(0), tok(3), tok(0), tok(0), tok(7), tok(8),
                  pl.BlockSpec((1, tm, D), lambda b, i: (b, i, 0)),
                  pl.BlockSpec((1, 2, D), lambda b, i: (b, 0, 0)),
                  pl.BlockSpec((1, width), lambda b, i: (0, 0)),
                  pl.BlockSpec((2 * width, D), lambda b, i: (0, 0))],
        out_specs=pl.BlockSpec((1, tm, D), lambda b, i: (b, i, 0)),
        out_shape=jax.ShapeDtypeStruct((B, S, D), F32),
        compiler_params=_params("parallel", "parallel"), name="merge_ab",
    )(oa, P, hf, hb, P, P, xc, gate, h_norm.reshape(1, width), w_out)


def _gla_kernel(q_ref, f_ref, v_ref, lb_ref, o_ref, st_ref, *, reverse):
    C = HG_CHUNK
    T = q_ref.shape[1]
    nsub = T // C
    gw = q_ref.shape[2]

    @pl.when(pl.program_id(2) == 0)
    def _():
        st_ref[...] = jnp.zeros_like(st_ref)

    def causal(n, same_chunk):
        t_idx = lax.broadcasted_iota(jnp.int32, (n, n), 0)
        s_idx = lax.broadcasted_iota(jnp.int32, (n, n), 1)
        m = (s_idx >= t_idx) if reverse else (s_idx <= t_idx)
        return m & (t_idx // C == s_idx // C) if same_chunk else m

    mask = causal(C, False)
    tri = causal(T, True).astype(BF16)
    last = 0 if reverse else C - 1
    mid = C // 2 if reverse else C // 2 - 1
    order = list(range(nsub - 1, -1, -1) if reverse else range(nsub))
    heads = [slice(g * HG_DIM, (g + 1) * HG_DIM) for g in range(HG_GROUP)]
    rows = [slice(c * C, (c + 1) * C) for c in range(nsub)]

    lb = lb_ref[...]
    half = 0.5 * (1.0 - lb)
    f = (lb + half) + half * jnp.tanh(0.5 * f_ref[0])
    b = _cumsum_rows(tri, jnp.log(f), terms=2)
    b_mid = [b[c * C + mid:c * C + mid + 1, :] for c in range(nsub)]
    b_last = [b[c * C + last:c * C + last + 1, :] for c in range(nsub)]
    b_mid_rows = jnp.concatenate([jnp.broadcast_to(r, (C, gw)) for r in b_mid], axis=0)
    qm = _silu(q_ref[0].astype(F32)) * jnp.exp(b - b_mid_rows)
    km = (1.0 - f) * jnp.exp(b_mid_rows - b)
    e_mid = [jnp.exp(r) for r in b_mid]
    e_last_mid = [jnp.exp(l - r) for l, r in zip(b_last, b_mid)]
    dec = [jnp.exp(l) for l in b_last]
    qmb, kmb, vb = qm.astype(BF16), km.astype(BF16), v_ref[0]

    a, upd = {}, {}
    for g, sl in enumerate(heads):
        for c, rs in enumerate(rows):
            s = lax.dot_general(qmb[rs, sl], kmb[rs, sl], NT, preferred_element_type=F32)
            a[g, c] = jnp.where(mask, s, 0.0).astype(BF16)
            kh = (km[rs, sl] * e_last_mid[c][:, sl]).astype(BF16)
            upd[g, c] = lax.dot_general(vb[rs, sl], kh, TN, preferred_element_type=F32)
    intra = {gc: jnp.dot(a[gc], vb[rows[gc[1]], heads[gc[0]]], preferred_element_type=F32) for gc in a}
    for g, sl in enumerate(heads):
        st = st_ref[g]
        for c in order:
            rs = rows[c]
            qi = (qm[rs, sl] * e_mid[c][:, sl]).astype(BF16)
            o_ref[0, rs, sl] = (intra[g, c] + lax.dot_general(qi, st.astype(BF16), NT, preferred_element_type=F32)
                                ).astype(o_ref.dtype)
            st = st * dec[c][:, sl] + upd[g, c]
        st_ref[g] = st


def _gla_scan(P, Pf, lb, n_ctx, reverse):
    B, S, _ = P.shape
    C = HG_BLOCK
    width = HG_HEADS * HG_DIM
    gw = HG_GROUP * HG_DIM
    ngrp = width // gw
    assert n_ctx % C == 0 and S % C == 0 and C % HG_CHUNK == 0
    nch, nctx_ch = S // C, n_ctx // C
    fcol = ngrp if reverse else 0

    def cidx(j):
        if not reverse:
            return j
        return jnp.where(j < nctx_ch, nctx_ch - 1 - j, nch + nctx_ch - 1 - j)

    def oidx(j):
        return jnp.maximum(cidx(j) - nctx_ch, 0) if not reverse else jnp.where(j < nctx_ch, nch - nctx_ch - 1,
                                                                                cidx(j) - nctx_ch)

    return pl.pallas_call(
        functools.partial(_gla_kernel, reverse=reverse),
        grid=(B, ngrp, nch),
        in_specs=[pl.BlockSpec((1, C, gw), lambda b, g, j: (b, cidx(j), g)),
                  pl.BlockSpec((1, C, gw), lambda b, g, j: (b, cidx(j), fcol + g)),
                  pl.BlockSpec((1, C, gw), lambda b, g, j: (b, cidx(j), ngrp + g)),
                  pl.BlockSpec((1, gw), lambda b, g, j: (0, g))],
        out_specs=pl.BlockSpec((1, C, gw), lambda b, g, j: (b, oidx(j), g)),
        out_shape=jax.ShapeDtypeStruct((B, S - n_ctx, width), BF16),
        scratch_shapes=[pltpu.VMEM((HG_GROUP, HG_DIM, HG_DIM), F32)],
        compiler_params=_params("parallel", "parallel", "arbitrary"), name="gla_bwd" if reverse else "gla_fwd",
    )(P, Pf, P, lb)


def _merge_c_kernel(of_ref, ob_ref, z_ref, x_ref, gate_ref, hn_ref, w_ref, o_ref):
    o = of_ref[0].astype(F32) + ob_ref[0].astype(F32)
    parts = []
    for h in range(HG_HEADS):
        seg = o[:, h * HG_DIM:(h + 1) * HG_DIM]
        parts.append(seg * lax.rsqrt(jnp.mean(seg * seg, axis=1, keepdims=True) + EPS))
    y = jnp.concatenate(parts, axis=1) * hn_ref[...] * _silu(z_ref[0].astype(F32))
    o_ref[0] = x_ref[0] + gate_ref[0] * jnp.dot(y.astype(BF16), w_ref[...], preferred_element_type=F32)


def _merge_c(of, ob, P, xc, gate, h_norm, w_out, n_ctx, tm):
    B, T, width = of.shape
    D = xc.shape[2]
    off = n_ctx // tm
    return pl.pallas_call(
        _merge_c_kernel,
        grid=(B, T // tm),
        in_specs=[pl.BlockSpec((1, tm, width), lambda b, i: (b, i, 0)),
                  pl.BlockSpec((1, tm, width), lambda b, i: (b, i, 0)),
                  pl.BlockSpec((1, tm, width), lambda b, i: (b, i + off, 2)),
                  pl.BlockSpec((1, tm, D), lambda b, i: (b, i + off, 0)),
                  pl.BlockSpec((1, 1, D), lambda b, i: (b, 0, 0)),
                  pl.BlockSpec((1, width), lambda b, i: (0, 0)),
                  pl.BlockSpec((width, D), lambda b, i: (0, 0))],
        out_specs=pl.BlockSpec((1, tm, D), lambda b, i: (b, i, 0)),
        out_shape=jax.ShapeDtypeStruct((B, T, D), F32),
        compiler_params=_params("parallel", "parallel"), name="merge_c",
    )(of, ob, P, xc, gate, h_norm.reshape(1, width), w_out)


def _pad_cols(w, n):
    return jnp.pad(w, ((0, 0), (0, n - w.shape[1])))


def kernel(x, c, ctx, c_ctx, norm_w, w_ada, b_ada, w_in_ab, b_gate_ab, q_norm_a, k_norm_a, rpb_a, h_norm_b, w_out_ab,
           w_in_c, lb_c, h_norm_c, w_out_c):
    B, T, D = x.shape
    n_ctx = ctx.shape[1]
    S = n_ctx + T
    tm_in = S // 4
    tm_out = S // 16
    xc = jnp.concatenate([ctx, x], axis=1)
    cvec = jnp.concatenate([c, c_ctx[None], jnp.zeros((16 - B - 1, D), F32)], axis=0)

    def modulation(l):
        m = _ada(cvec, w_ada[l], b_ada[l])
        shift, scale, gate = m[:, :D], m[:, D:2 * D], m[:, 2 * D:]
        ctx_row = lambda a: jnp.broadcast_to(a[B][None], (B, D))
        mod = jnp.stack([ctx_row(shift), ctx_row(scale), shift[:B], scale[:B]], axis=1)
        return mod, jnp.stack([ctx_row(gate), gate[:B]], axis=1)

    mod, gate = modulation(0)
    n_main = 9 * NA_HEADS * NA_DIM
    w_main = w_in_ab[0][:, :n_main].astype(BF16)
    w_gate = _pad_cols(w_in_ab[0][:, n_main:], LANES).astype(BF16)
    b_gate = _pad_cols(b_gate_ab[0][None], LANES)
    ml_w = ML_HEADS * ML_DIM
    P = _inproj(xc, norm_w[0], mod, w_main, jnp.zeros((1, n_main), F32), n_ctx, tm_in, ml_w, BF16,
                rope=_rope_tables(n_ctx, T) + (4, 5))
    gates = _inproj(xc, norm_w[0], mod, w_gate, b_gate, n_ctx, tm_in, LANES, F32)
    oa = _na_attention(P, _na_bias_tables(rpb_a[0], T // GRID_W), q_norm_a[0], k_norm_a[0], n_ctx)
    hf = _mlstm_scan(P, gates, n_ctx, reverse=False)
    hb = _mlstm_scan(P, gates, n_ctx, reverse=True)
    xc = _merge_ab(oa, hf, hb, P, xc, gate, h_norm_b[0], w_out_ab[0].astype(BF16), n_ctx, tm_out)

    mod, gate = modulation(1)
    sm = jax.nn.softmax(lb_c.astype(F32), axis=0)
    lb = (jnp.cumsum(sm, axis=0) - sm[0])[1].reshape(1, -1)
    hw = HG_HEADS * HG_DIM
    w_c = w_in_c[0].astype(BF16)
    w_qiz = jnp.concatenate([w_c[:, :hw], w_c[:, 3 * hw:]], axis=1)
    P = _inproj(xc, norm_w[1], mod, w_qiz, jnp.zeros((1, 3 * hw), F32), n_ctx, tm_in, hw, BF16)
    Pf = _inproj(xc, norm_w[1], mod, w_c[:, hw:3 * hw], jnp.zeros((1, 2 * hw), F32), n_ctx, tm_in, hw, F32)
    of = _gla_scan(P, Pf, lb, n_ctx, reverse=False)
    ob = _gla_scan(P, Pf, lb, n_ctx, reverse=True)
    return _merge_c(of, ob, P, xc, gate[:, 1:2], h_norm_c[0], w_out_c[0].astype(BF16), n_ctx, n_ctx)
```

```python
import functools

import numpy as np
import jax
import jax.numpy as jnp
from jax import lax
from jax.experimental import pallas as pl
from jax.experimental.pallas import tpu as pltpu

F32 = jnp.float32
BF16 = jnp.bfloat16
HIGHEST = lax.Precision.HIGHEST

GRID_W = 64
NA_HEADS = 16
NA_DIM = 64
NA_WIN_H = 8
NA_WIN_W = 16
ML_HEADS = 4
ML_DIM = 256
HG_HEADS = 16
HG_DIM = 128
ROPE_BASE = 10000.0
EPS = 1e-6
NEG = -1e30
LOG2E = 1.4426950408889634

LANES = 128
V7X_VMEM_LIMIT = 56 * 1024 * 1024

NA_QROWS = 4
NA_KROWS = NA_QROWS + NA_WIN_H - 1
ML_CHUNK = 256
HG_CHUNK = 64
HG_BLOCK = 256
HG_GROUP = 8

NT = (((1,), (1,)), ((), ()))
TN = (((0,), (0,)), ((), ()))


def _params(*sem):
    return pltpu.CompilerParams(dimension_semantics=sem, vmem_limit_bytes=V7X_VMEM_LIMIT)


def _sigmoid(z):
    return 0.5 * jnp.tanh(0.5 * z) + 0.5


def _silu(z):
    h = 0.5 * z
    return h + h * jnp.tanh(h)


def _log_sigmoid(z):
    return jnp.minimum(z, 0.0) - jnp.log1p(jnp.exp(-jnp.abs(z)))


def _split_bf16(x, terms):
    out = []
    for _ in range(terms - 1):
        out.append(x.astype(BF16))
        x = x - out[-1].astype(F32)
    return out + [x.astype(BF16)]


def _cumsum_rows(tri_bf16, x, terms=3):
    return sum(jnp.dot(tri_bf16, t, preferred_element_type=F32) for t in _split_bf16(x, terms))


def _cumsum_cols(x, tri_bf16, terms=3):
    return sum(lax.dot_general(t, tri_bf16, NT, preferred_element_type=F32) for t in _split_bf16(x, terms))


def _ada_kernel(c_ref, w_ref, b_ref, o_ref):
    s = _silu(c_ref[...])
    o_ref[...] = jnp.dot(s.astype(BF16), w_ref[...].astype(BF16), preferred_element_type=F32) + b_ref[...]


def _ada(cvec, w, b):
    R, D = cvec.shape
    N = w.shape[1]
    tn = D
    return pl.pallas_call(
        _ada_kernel, grid=(N // tn,),
        in_specs=[pl.BlockSpec((R, D), lambda n: (0, 0)),
                  pl.BlockSpec((D, tn), lambda n: (0, n)),
                  pl.BlockSpec((1, tn), lambda n: (0, n))],
        out_specs=pl.BlockSpec((R, tn), lambda n: (0, n)),
        out_shape=jax.ShapeDtypeStruct((R, N), F32),
        compiler_params=_params("arbitrary"), name="ada",
    )(cvec, w, b.reshape(1, N))


def _inproj_kernel(x_ref, nw_ref, mod_ref, w_ref, bias_ref, *rest, n_ctx, rope_cols):
    if rope_cols is None:
        o_ref, hn_ref = rest
    else:
        cos_ref, sin_ref, o_ref, hn_ref = rest
    i = pl.program_id(1)
    n = pl.program_id(2)
    tm = x_ref.shape[1]

    @pl.when(pl.program_id(2) == 0)
    def _():
        x = x_ref[0]
        y = x * lax.rsqrt(jnp.mean(x * x, axis=-1, keepdims=True) + EPS) * nw_ref[...]
        row = i * tm + lax.broadcasted_iota(jnp.int32, (tm, 1), 0)
        is_ctx = row < n_ctx
        shift = jnp.where(is_ctx, mod_ref[0, 0:1, :], mod_ref[0, 2:3, :])
        scale = jnp.where(is_ctx, mod_ref[0, 1:2, :], mod_ref[0, 3:4, :])
        hn_ref[...] = (y * (1.0 + scale) + shift).astype(BF16)

    def project():
        return jnp.dot(hn_ref[...], w_ref[...], preferred_element_type=F32) + bias_ref[...]

    if rope_cols is None:
        o_ref[0] = project().astype(o_ref.dtype)
    else:
        tile, q_col, k_col = rope_cols
        ml_w = ML_HEADS * ML_DIM

        @pl.when(n == tile)
        def _():
            acc = project()
            cos, sin = cos_ref[...], sin_ref[...]
            for c in range(0, acc.shape[1], ML_DIM):
                cs = slice(c, c + ML_DIM)
                if q_col <= c < q_col + ml_w:
                    o_ref[0, :, cs] = _rope(acc[:, cs], cos, sin).astype(o_ref.dtype)
                elif k_col <= c < k_col + ml_w:
                    o_ref[0, :, cs] = (_rope(acc[:, cs], cos, sin) * (ML_DIM ** -0.5)).astype(o_ref.dtype)
                else:
                    o_ref[0, :, cs] = acc[:, cs].astype(o_ref.dtype)

        @pl.when(n != tile)
        def _():
            o_ref[0] = project().astype(o_ref.dtype)


def _inproj(xc, norm_w, mod, w, bias, n_ctx, tm, tn, out_dtype, rope=None):
    B, S, D = xc.shape
    N = w.shape[1]
    in_specs = [pl.BlockSpec((1, tm, D), lambda b, i, n: (b, i, 0)),
                pl.BlockSpec((1, D), lambda b, i, n: (0, 0)),
                pl.BlockSpec((1, 4, D), lambda b, i, n: (b, 0, 0)),
                pl.BlockSpec((D, tn), lambda b, i, n: (0, n)),
                pl.BlockSpec((1, tn), lambda b, i, n: (0, n))]
    args = [xc, norm_w.reshape(1, D), mod, w, bias]
    if rope is not None:
        in_specs += [pl.BlockSpec((tm, ML_DIM), lambda b, i, n: (i, 0))] * 2
        args += list(rope[:2])
    return pl.pallas_call(
        functools.partial(_inproj_kernel, n_ctx=n_ctx, rope_cols=None if rope is None else tuple(rope[2:])),
        grid=(B, S // tm, N // tn),
        in_specs=in_specs,
        out_specs=pl.BlockSpec((1, tm, tn), lambda b, i, n: (b, i, n)),
        out_shape=jax.ShapeDtypeStruct((B, S, N), out_dtype),
        scratch_shapes=[pltpu.VMEM((tm, D), BF16)],
        compiler_params=_params("parallel", "parallel", "arbitrary"), name="inproj",
    )(*args)


def _na_bias_tables(rpb, rows):
    H = rpb.shape[0]
    qr, ka = np.arange(NA_QROWS)[:, None], np.arange(NA_KROWS)[None, :]
    qc, kc = np.arange(GRID_W)[:, None], np.arange(GRID_W)[None, :]
    c0 = np.clip(qc - NA_WIN_W // 2, 0, GRID_W - NA_WIN_W)
    col_ok = (kc >= c0) & (kc < c0 + NA_WIN_W)
    col_sel = ((kc - qc + NA_WIN_W - 1)[..., None] == np.arange(2 * NA_WIN_W - 1)) & col_ok[..., None]
    nblk = rows // NA_QROWS
    tables = []
    for blk in (0, 1, nblk - 1):
        kr0 = int(np.clip(NA_QROWS * blk - NA_WIN_H // 2, 0, rows - NA_KROWS))
        r = NA_QROWS * blk + qr
        r0 = np.clip(r - NA_WIN_H // 2, 0, rows - NA_WIN_H)
        krow = kr0 + ka
        row_ok = (krow >= r0) & (krow < r0 + NA_WIN_H)
        row_sel = ((krow - r + NA_WIN_H - 1)[..., None] == np.arange(2 * NA_WIN_H - 1)) & row_ok[..., None]
        tab = jnp.einsum("rai,hij,qkj->hrqak", row_sel.astype(np.float32), rpb.astype(F32),
                         col_sel.astype(np.float32), precision=HIGHEST)
        valid = row_ok[:, None, :, None] & col_ok[None, :, None, :]
        tables.append(jnp.where(valid[None], tab * LOG2E, NEG).reshape(H, NA_QROWS * GRID_W, NA_KROWS * GRID_W))
    return jnp.stack(tables).astype(F32)


def _na_kernel(q_ref, k_ref, v_ref, bias_ref, qn_ref, kn_ref, ones_ref, o_ref, k_s, q0_s, q1_s, v0_s, v1_s,
               *, n_ctx, rows):
    nq = NA_QROWS * GRID_W
    win = NA_KROWS * GRID_W
    nblk = rows // NA_QROWS
    lane = lax.broadcasted_iota(jnp.int32, (1, LANES), 1)
    in_h0 = lane < NA_DIM
    qh_s, vh_s = (q0_s, q1_s), (v0_s, v1_s)
    den_lane = (NA_DIM, 0)

    def rms(x, w):
        ss = jnp.dot((x * x).astype(BF16), ones_ref[...], preferred_element_type=F32)
        return x * lax.rsqrt(ss * (1.0 / NA_DIM) + EPS) * w

    k_s[...] = rms(k_ref[0].astype(F32), kn_ref[...]).astype(BF16)
    q = rms(q_ref[0].astype(F32), qn_ref[...]) * (NA_DIM ** -0.5 * LOG2E)
    q0_s[...] = jnp.where(in_h0, q, 0.0).astype(BF16)
    q1_s[...] = jnp.where(in_h0, 0.0, q).astype(BF16)
    v = v_ref[0].astype(F32)
    v0_s[...] = jnp.where(in_h0, v, jnp.where(lane == den_lane[0], 1.0, 0.0)).astype(BF16)
    v1_s[...] = jnp.where(in_h0, jnp.where(lane == den_lane[1], 1.0, 0.0), v).astype(BF16)

    def softmax_pv(h, s_parts, bias, v_starts):
        if bias is not None:
            s_parts = [s_parts[0] + bias] + s_parts[1:]
        m = functools.reduce(jnp.maximum, [jnp.max(s, axis=1, keepdims=True) for s in s_parts])
        return sum(jnp.dot(jnp.exp2(s - m).astype(BF16), vh_s[h][pl.ds(st, s.shape[1]), :],
                           preferred_element_type=F32) for s, st in zip(s_parts, v_starts))

    def store(q_start, o):
        den = [o[h][:, den_lane[h]:den_lane[h] + 1] for h in range(2)]
        o_ref[0, pl.ds(q_start, nq), :] = jnp.where(in_h0, o[0] / den[0], o[1] / den[1]).astype(o_ref.dtype)

    kc = k_s[0:n_ctx, :]
    s_ctx = [lax.dot_general(qh_s[h][0:n_ctx, :], kc, NT, preferred_element_type=F32) for h in range(2)]
    store(0, [softmax_pv(h, [s_ctx[h]], None, [0]) for h in range(2)])

    def block_pos(blk):
        q_start = pl.multiple_of(n_ctx + nq * blk, nq)
        kr0 = jnp.clip(NA_QROWS * blk - NA_WIN_H // 2, 0, rows - NA_KROWS)
        return q_start, pl.multiple_of(n_ctx + GRID_W * kr0, GRID_W)

    def logits(h, blk):
        q_start, k_start = block_pos(blk)
        qh = qh_s[h][pl.ds(q_start, nq), :]
        return [lax.dot_general(qh, k_s[pl.ds(k_start, win), :], NT, preferred_element_type=F32),
                lax.dot_general(qh, kc, NT, preferred_element_type=F32)]

    def attend(h, blk, s_parts):
        case = jnp.where(blk == 0, 0, jnp.where(blk == nblk - 1, 2, 1))
        return softmax_pv(h, s_parts, bias_ref[case, h], [block_pos(blk)[1], 0])

    def two_blocks(it, carry):
        b0, b1 = 2 * it, 2 * it + 1
        s00, s10, s01 = logits(0, b0), logits(1, b0), logits(0, b1)
        o00 = attend(0, b0, s00)
        s11 = logits(1, b1)
        o10 = attend(1, b0, s10)
        o01 = attend(0, b1, s01)
        o11 = attend(1, b1, s11)
        store(block_pos(b0)[0], [o00, o10])
        store(block_pos(b1)[0], [o01, o11])
        return carry

    lax.fori_loop(0, nblk // 2, two_blocks, 0)


def _na_attention(P, bias_tab, q_norm, k_norm, n_ctx):
    B, S, _ = P.shape
    width = NA_HEADS * NA_DIM
    npair = width // LANES
    nq = NA_QROWS * GRID_W
    rows = (S - n_ctx) // GRID_W
    assert n_ctx == nq and rows % (2 * NA_QROWS) == 0
    win = NA_KROWS * GRID_W
    hp = LANES // NA_DIM
    ones = jnp.asarray(np.kron(np.eye(hp), np.ones((NA_DIM, NA_DIM))), BF16)
    qn = jnp.tile(q_norm, hp).reshape(1, LANES)
    kn = jnp.tile(k_norm, hp).reshape(1, LANES)
    seq = lambda c: pl.BlockSpec((1, S, LANES), lambda b, p: (b, 0, c * npair + p))
    vec = pl.BlockSpec((1, LANES), lambda b, p: (0, 0))
    return pl.pallas_call(
        functools.partial(_na_kernel, n_ctx=n_ctx, rows=rows),
        grid=(B, npair),
        in_specs=[seq(0), seq(1), seq(2),
                  pl.BlockSpec((bias_tab.shape[0], hp, nq, win), lambda b, p: (0, p, 0, 0)),
                  vec, vec, pl.BlockSpec((LANES, LANES), lambda b, p: (0, 0))],
        out_specs=seq(0),
        out_shape=jax.ShapeDtypeStruct((B, S, width), BF16),
        scratch_shapes=[pltpu.VMEM((S, LANES), BF16)] * 5,
        compiler_params=_params("parallel", "parallel"), name="na_attention",
    )(P, P, P, bias_tab, qn, kn, ones)


def _rope_tables(n_ctx, T):
    half = ML_DIM // 4
    t = jnp.arange(T)
    freqs = ROPE_BASE ** (-jnp.arange(half, dtype=F32) / half)
    ang_r = (t // GRID_W).astype(F32)[:, None] * freqs[None, :]
    ang_c = (t % GRID_W).astype(F32)[:, None] * freqs[None, :]
    cos = jnp.concatenate([jnp.cos(ang_r)] * 2 + [jnp.cos(ang_c)] * 2, axis=-1)
    sin = jnp.concatenate([-jnp.sin(ang_r), jnp.sin(ang_r), -jnp.sin(ang_c), jnp.sin(ang_c)], axis=-1)
    cos = jnp.concatenate([jnp.ones((n_ctx, ML_DIM), F32), cos], axis=0)
    sin = jnp.concatenate([jnp.zeros((n_ctx, ML_DIM), F32), sin], axis=0)
    return cos, sin


def _rope(x, cos, sin):
    xr = jnp.concatenate([pltpu.roll(x[:, a * LANES:(a + 1) * LANES], LANES // 2, axis=1)
                          for a in range(x.shape[1] // LANES)], axis=1)
    return x * cos + xr * sin


def _mlstm_kernel(q_ref, k_ref, v_ref, g_ref, o_ref, c_ref, n_ref, m_ref, *, reverse):
    L = q_ref.shape[1]

    @pl.when(pl.program_id(1) == 0)
    def _():
        c_ref[...] = jnp.zeros_like(c_ref)
        n_ref[...] = jnp.zeros_like(n_ref)
        m_ref[...] = jnp.zeros_like(m_ref)

    t_idx = lax.broadcasted_iota(jnp.int32, (L, L), 0)
    s_idx = lax.broadcasted_iota(jnp.int32, (L, L), 1)
    mask = (s_idx >= t_idx) if reverse else (s_idx <= t_idx)
    tri = mask.astype(BF16)
    last = 0 if reverse else L - 1

    g = g_ref[0]
    lsg = _log_sigmoid(g)
    g_t = g.T
    b_cols = _cumsum_rows(tri, lsg)
    b_rows = _cumsum_cols(lsg.T, tri)
    ones = jnp.ones((L, LANES), BF16)

    heads = range(ML_HEADS)
    hs = [slice(h * ML_DIM, (h + 1) * ML_DIM) for h in heads]
    qb, kb, vb, kwb, decay, m_t, carry_w, m_new, dec = ([None] * ML_HEADS for _ in range(9))
    for h in heads:
        ii = (2 if reverse else 0) * ML_HEADS + h
        fi = (3 if reverse else 1) * ML_HEADS + h
        ig_col, ig_row = g[:, ii:ii + 1], g_t[ii:ii + 1, :]
        b_col, b_row = b_cols[:, fi:fi + 1], b_rows[fi:fi + 1, :]
        b_last = b_col[last:last + 1, :]
        qb[h], kb[h], vb[h] = q_ref[0, :, hs[h]], k_ref[0, :, hs[h]], v_ref[0, :, hs[h]]
        m_prev = m_ref[h, 0:1, 0:1]
        dmat = jnp.where(mask, b_col - b_row + ig_row, NEG)
        m_inter = b_col + m_prev
        m_t[h] = jnp.maximum(jnp.max(dmat, axis=1, keepdims=True), m_inter)
        decay[h] = jnp.exp(dmat - m_t[h])
        carry_w[h] = jnp.broadcast_to(jnp.exp(m_inter - m_t[h]), (L, LANES))
        g_col = b_last - b_col + ig_col
        m_new[h] = jnp.maximum(b_last + m_prev, jnp.max(g_col, axis=0, keepdims=True))
        kwb[h] = (kb[h].astype(F32) * jnp.exp(g_col - m_new[h])).astype(BF16)
        dec[h] = jnp.exp(b_last + m_prev - m_new[h])

    qk = [lax.dot_general(qb[h], kb[h], NT, preferred_element_type=F32) for h in heads]
    qc = [jnp.dot(qb[h], c_ref[h].astype(BF16), preferred_element_type=F32) for h in heads]
    qn = [jnp.dot(qb[h], n_ref[h].astype(BF16), preferred_element_type=F32) for h in heads]
    upd = [lax.dot_general(kwb[h], vb[h], TN, preferred_element_type=F32) for h in heads]
    kw_sum = [lax.dot_general(kwb[h], ones, TN, preferred_element_type=F32) for h in heads]
    sb = [(qk[h] * decay[h]).astype(BF16) for h in heads]
    sv = [jnp.dot(sb[h], vb[h], preferred_element_type=F32) for h in heads]
    s_sum = [jnp.dot(sb[h], ones, preferred_element_type=F32) for h in heads]
    for h in heads:
        den = s_sum[h] + carry_w[h] * qn[h]
        inv = 1.0 / jnp.maximum(jnp.abs(den), jnp.exp(-m_t[h]))
        num = sv[h] + jnp.concatenate([carry_w[h]] * (ML_DIM // LANES), axis=1) * qc[h]
        o_ref[0, :, hs[h]] = (num * jnp.concatenate([inv] * (ML_DIM // LANES), axis=1)).astype(o_ref.dtype)
        c_ref[h] = dec[h] * c_ref[h] + upd[h]
        n_ref[h] = dec[h] * n_ref[h] + kw_sum[h]
        m_ref[h] = jnp.broadcast_to(m_new[h], m_ref.shape[1:])


def _mlstm_scan(P, gates, n_ctx, reverse):
    B, S, _ = P.shape
    L = ML_CHUNK
    width = ML_HEADS * ML_DIM
    assert n_ctx == L and S % L == 0
    nch = S // L

    def cidx(j):
        return jnp.where(j == 0, 0, nch - j) if reverse else j

    return pl.pallas_call(
        functools.partial(_mlstm_kernel, reverse=reverse),
        grid=(B, nch),
        in_specs=[pl.BlockSpec((1, L, width), lambda b, j: (b, cidx(j), 4)),
                  pl.BlockSpec((1, L, width), lambda b, j: (b, cidx(j), 5)),
                  pl.BlockSpec((1, L, width), lambda b, j: (b, cidx(j), 6)),
                  pl.BlockSpec((1, L, LANES), lambda b, j: (b, cidx(j), 0))],
        out_specs=pl.BlockSpec((1, L, width), lambda b, j: (b, cidx(j), 0)),
        out_shape=jax.ShapeDtypeStruct((B, S, width), BF16),
        scratch_shapes=[pltpu.VMEM((ML_HEADS, ML_DIM, ML_DIM), F32),
                        pltpu.VMEM((ML_HEADS, ML_DIM, LANES), F32),
                        pltpu.VMEM((ML_HEADS, 8, LANES), F32)],
        compiler_params=_params("parallel", "arbitrary"), name="mlstm_bwd" if reverse else "mlstm_fwd",
    )(P, P, P, gates)


def _merge_ab_kernel(oa_ref, za_ref, hf_ref, hb_ref, ob_ref, zb_ref, x_ref, gate_ref, hn_ref, w_ref, o_ref, *, n_ctx):
    tm = x_ref.shape[1]
    width = oa_ref.shape[2]
    ya = oa_ref[0].astype(F32) * _silu(za_ref[0].astype(F32))
    hb = (hf_ref[0].astype(F32) + hb_ref[0].astype(F32)) * _sigmoid(ob_ref[0].astype(F32))
    parts = []
    for h in range(ML_HEADS):
        seg = hb[:, h * ML_DIM:(h + 1) * ML_DIM]
        parts.append(seg * lax.rsqrt(jnp.mean(seg * seg, axis=1, keepdims=True) + EPS))
    yb = jnp.concatenate(parts, axis=1) * hn_ref[...] * _silu(zb_ref[0].astype(F32))
    y = (jnp.dot(ya.astype(BF16), w_ref[0:width, :], preferred_element_type=F32)
         + jnp.dot(yb.astype(BF16), w_ref[width:2 * width, :], preferred_element_type=F32))
    row = pl.program_id(1) * tm + lax.broadcasted_iota(jnp.int32, (tm, 1), 0)
    gate = jnp.where(row < n_ctx, gate_ref[0, 0:1, :], gate_ref[0, 1:2, :])
    o_ref[0] = x_ref[0] + gate * y


def _merge_ab(oa, hf, hb, P, xc, gate, h_norm, w_out, n_ctx, tm):
    B, S, D = xc.shape
    width = oa.shape[2]
    tok = lambda c: pl.BlockSpec((1, tm, width), lambda b, i: (b, i, c))
    return pl.pallas_call(
        functools.partial(_merge_ab_kernel, n_ctx=n_ctx),
        grid=(B, S // tm),
        in_specs=[tok(0), tok(3), tok(0), tok(0), tok(7), tok(8),
                  pl.BlockSpec((1, tm, D), lambda b, i: (b, i, 0)),
                  pl.BlockSpec((1, 2, D), lambda b, i: (b, 0, 0)),
                  pl.BlockSpec((1, width), lambda b, i: (0, 0)),
                  pl.BlockSpec((2 * width, D), lambda b, i: (0, 0))],
        out_specs=pl.BlockSpec((1, tm, D), lambda b, i: (b, i, 0)),
        out_shape=jax.ShapeDtypeStruct((B, S, D), F32),
        compiler_params=_params("parallel", "parallel"), name="merge_ab",
    )(oa, P, hf, hb, P, P, xc, gate, h_norm.reshape(1, width), w_out)


def _gla_kernel(q_ref, f_ref, v_ref, lb_ref, o_ref, st_ref, *, reverse):
    C = HG_CHUNK
    T = q_ref.shape[1]
    nsub = T // C
    gw = q_ref.shape[2]

    @pl.when(pl.program_id(2) == 0)
    def _():
        st_ref[...] = jnp.zeros_like(st_ref)

    def causal(n, same_chunk):
        t_idx = lax.broadcasted_iota(jnp.int32, (n, n), 0)
        s_idx = lax.broadcasted_iota(jnp.int32, (n, n), 1)
        m = (s_idx >= t_idx) if reverse else (s_idx <= t_idx)
        return m & (t_idx // C == s_idx // C) if same_chunk else m

    mask = causal(C, False)
    tri = causal(T, True).astype(BF16)
    last = 0 if reverse else C - 1
    mid = C // 2 if reverse else C // 2 - 1
    order = list(range(nsub - 1, -1, -1) if reverse else range(nsub))
    heads = [slice(g * HG_DIM, (g + 1) * HG_DIM) for g in range(HG_GROUP)]
    rows = [slice(c * C, (c + 1) * C) for c in range(nsub)]

    lb = lb_ref[...]
    half = 0.5 * (1.0 - lb)
    f = (lb + half) + half * jnp.tanh(0.5 * f_ref[0].astype(F32))
    b = _cumsum_rows(tri, jnp.log(f), terms=2)
    b_mid = [b[c * C + mid:c * C + mid + 1, :] for c in range(nsub)]
    b_last = [b[c * C + last:c * C + last + 1, :] for c in range(nsub)]
    b_mid_rows = jnp.concatenate([jnp.broadcast_to(r, (C, gw)) for r in b_mid], axis=0)
    qm = _silu(q_ref[0].astype(F32)) * jnp.exp(b - b_mid_rows)
    km = (1.0 - f) * jnp.exp(b_mid_rows - b)
    e_mid = [jnp.exp(r) for r in b_mid]
    e_last_mid = [jnp.exp(l - r) for l, r in zip(b_last, b_mid)]
    dec = [jnp.exp(l) for l in b_last]
    qmb, kmb, vb = qm.astype(BF16), km.astype(BF16), v_ref[0]

    a, upd = {}, {}
    for g, sl in enumerate(heads):
        for c, rs in enumerate(rows):
            s = lax.dot_general(qmb[rs, sl], kmb[rs, sl], NT, preferred_element_type=F32)
            a[g, c] = jnp.where(mask, s, 0.0).astype(BF16)
            kh = (km[rs, sl] * e_last_mid[c][:, sl]).astype(BF16)
            upd[g, c] = lax.dot_general(vb[rs, sl], kh, TN, preferred_element_type=F32)
    intra = {gc: jnp.dot(a[gc], vb[rows[gc[1]], heads[gc[0]]], preferred_element_type=F32) for gc in a}
    for g, sl in enumerate(heads):
        st = st_ref[g]
        for c in order:
            rs = rows[c]
            qi = (qm[rs, sl] * e_mid[c][:, sl]).astype(BF16)
            o_ref[0, rs, sl] = (intra[g, c] + lax.dot_general(qi, st.astype(BF16), NT, preferred_element_type=F32)
                                ).astype(o_ref.dtype)
            st = st * dec[c][:, sl] + upd[g, c]
        st_ref[g] = st


def _gla_scan(P, lb, n_ctx, reverse):
    B, S, _ = P.shape
    C = HG_BLOCK
    width = HG_HEADS * HG_DIM
    gw = HG_GROUP * HG_DIM
    ngrp = width // gw
    assert n_ctx % C == 0 and S % C == 0 and C % HG_CHUNK == 0
    nch, nctx_ch = S // C, n_ctx // C
    fcol = (2 if reverse else 1) * ngrp

    def cidx(j):
        if not reverse:
            return j
        return jnp.where(j < nctx_ch, nctx_ch - 1 - j, nch + nctx_ch - 1 - j)

    def oidx(j):
        return jnp.maximum(cidx(j) - nctx_ch, 0) if not reverse else jnp.where(j < nctx_ch, nch - nctx_ch - 1,
                                                                                cidx(j) - nctx_ch)

    return pl.pallas_call(
        functools.partial(_gla_kernel, reverse=reverse),
        grid=(B, ngrp, nch),
        in_specs=[pl.BlockSpec((1, C, gw), lambda b, g, j: (b, cidx(j), g)),
                  pl.BlockSpec((1, C, gw), lambda b, g, j: (b, cidx(j), fcol + g)),
                  pl.BlockSpec((1, C, gw), lambda b, g, j: (b, cidx(j), 3 * ngrp + g)),
                  pl.BlockSpec((1, gw), lambda b, g, j: (0, g))],
        out_specs=pl.BlockSpec((1, C, gw), lambda b, g, j: (b, oidx(j), g)),
        out_shape=jax.ShapeDtypeStruct((B, S - n_ctx, width), BF16),
        scratch_shapes=[pltpu.VMEM((HG_GROUP, HG_DIM, HG_DIM), F32)],
        compiler_params=_params("parallel", "parallel", "arbitrary"), name="gla_bwd" if reverse else "gla_fwd",
    )(P, P, P, lb)


def _merge_c_kernel(of_ref, ob_ref, z_ref, x_ref, gate_ref, hn_ref, w_ref, o_ref):
    o = of_ref[0].astype(F32) + ob_ref[0].astype(F32)
    parts = []
    for h in range(HG_HEADS):
        seg = o[:, h * HG_DIM:(h + 1) * HG_DIM]
        parts.append(seg * lax.rsqrt(jnp.mean(seg * seg, axis=1, keepdims=True) + EPS))
    y = jnp.concatenate(parts, axis=1) * hn_ref[...] * _silu(z_ref[0].astype(F32))
    o_ref[0] = x_ref[0] + gate_ref[0] * jnp.dot(y.astype(BF16), w_ref[...], preferred_element_type=F32)


def _merge_c(of, ob, P, xc, gate, h_norm, w_out, n_ctx, tm):
    B, T, width = of.shape
    D = xc.shape[2]
    off = n_ctx // tm
    return pl.pallas_call(
        _merge_c_kernel,
        grid=(B, T // tm),
        in_specs=[pl.BlockSpec((1, tm, width), lambda b, i: (b, i, 0)),
                  pl.BlockSpec((1, tm, width), lambda b, i: (b, i, 0)),
                  pl.BlockSpec((1, tm, width), lambda b, i: (b, i + off, 4)),
                  pl.BlockSpec((1, tm, D), lambda b, i: (b, i + off, 0)),
                  pl.BlockSpec((1, 1, D), lambda b, i: (b, 0, 0)),
                  pl.BlockSpec((1, width), lambda b, i: (0, 0)),
                  pl.BlockSpec((width, D), lambda b, i: (0, 0))],
        out_specs=pl.BlockSpec((1, tm, D), lambda b, i: (b, i, 0)),
        out_shape=jax.ShapeDtypeStruct((B, T, D), F32),
        compiler_params=_params("parallel", "parallel"), name="merge_c",
    )(of, ob, P, xc, gate, h_norm.reshape(1, width), w_out)


def _pad_cols(w, n):
    return jnp.pad(w, ((0, 0), (0, n - w.shape[1])))


def kernel(x, c, ctx, c_ctx, norm_w, w_ada, b_ada, w_in_ab, b_gate_ab, q_norm_a, k_norm_a, rpb_a, h_norm_b, w_out_ab,
           w_in_c, lb_c, h_norm_c, w_out_c):
    B, T, D = x.shape
    n_ctx = ctx.shape[1]
    S = n_ctx + T
    tm_in = S // 4
    tm_out = S // 16
    xc = jnp.concatenate([ctx, x], axis=1)
    cvec = jnp.concatenate([c, c_ctx[None], jnp.zeros((16 - B - 1, D), F32)], axis=0)

    def modulation(l):
        m = _ada(cvec, w_ada[l], b_ada[l])
        shift, scale, gate = m[:, :D], m[:, D:2 * D], m[:, 2 * D:]
        ctx_row = lambda a: jnp.broadcast_to(a[B][None], (B, D))
        mod = jnp.stack([ctx_row(shift), ctx_row(scale), shift[:B], scale[:B]], axis=1)
        return mod, jnp.stack([ctx_row(gate), gate[:B]], axis=1)

    mod, gate = modulation(0)
    n_main = 9 * NA_HEADS * NA_DIM
    w_main = w_in_ab[0][:, :n_main].astype(BF16)
    w_gate = _pad_cols(w_in_ab[0][:, n_main:], LANES).astype(BF16)
    b_gate = _pad_cols(b_gate_ab[0][None], LANES)
    ml_w = ML_HEADS * ML_DIM
    P = _inproj(xc, norm_w[0], mod, w_main, jnp.zeros((1, n_main), F32), n_ctx, tm_in, n_main // 3, BF16,
                rope=_rope_tables(n_ctx, T) + (1, ml_w, 2 * ml_w))
    gates = _inproj(xc, norm_w[0], mod, w_gate, b_gate, n_ctx, tm_in, LANES, F32)
    oa = _na_attention(P, _na_bias_tables(rpb_a[0], T // GRID_W), q_norm_a[0], k_norm_a[0], n_ctx)
    hf = _mlstm_scan(P, gates, n_ctx, reverse=False)
    hb = _mlstm_scan(P, gates, n_ctx, reverse=True)
    xc = _merge_ab(oa, hf, hb, P, xc, gate, h_norm_b[0], w_out_ab[0].astype(BF16), n_ctx, tm_out)

    mod, gate = modulation(1)
    sm = jax.nn.softmax(lb_c.astype(F32), axis=0)
    lb = (jnp.cumsum(sm, axis=0) - sm[0])[1].reshape(1, -1)
    hw = HG_HEADS * HG_DIM
    P = _inproj(xc, norm_w[1], mod, w_in_c[0].astype(BF16), jnp.zeros((1, 5 * hw), F32), n_ctx, tm_in, hw, BF16)
    of = _gla_scan(P, lb, n_ctx, reverse=False)
    ob = _gla_scan(P, lb, n_ctx, reverse=True)
    return _merge_c(of, ob, P, xc, gate[:, 1:2], h_norm_c[0], w_out_c[0].astype(BF16), n_ctx, n_ctx)
```

```python
import functools

import numpy as np
import jax
import jax.numpy as jnp
from jax import lax
from jax.experimental import pallas as pl
from jax.experimental.pallas import tpu as pltpu

F32 = jnp.float32
BF16 = jnp.bfloat16
HIGHEST = lax.Precision.HIGHEST

GRID_W = 64
NA_HEADS = 16
NA_DIM = 64
NA_WIN_H = 8
NA_WIN_W = 16
ML_HEADS = 4
ML_DIM = 256
HG_HEADS = 16
HG_DIM = 128
ROPE_BASE = 10000.0
EPS = 1e-6
NEG = -1e30
LOG2E = 1.4426950408889634

LANES = 128
V7X_VMEM_LIMIT = 56 * 1024 * 1024

NA_QROWS = 4
NA_KROWS = NA_QROWS + NA_WIN_H
ML_CHUNK = 256
HG_CHUNK = 64
HG_BLOCK = 256
HG_GROUP = 8

NT = (((1,), (1,)), ((), ()))
TN = (((0,), (0,)), ((), ()))


def _params(*sem):
    return pltpu.CompilerParams(dimension_semantics=sem, vmem_limit_bytes=V7X_VMEM_LIMIT)


def _sigmoid(z):
    return 0.5 * jnp.tanh(0.5 * z) + 0.5


def _silu(z):
    h = 0.5 * z
    return h + h * jnp.tanh(h)


def _log_sigmoid(z):
    return jnp.minimum(z, 0.0) - jnp.log1p(jnp.exp(-jnp.abs(z)))


def _split_bf16(x, terms):
    out = []
    for _ in range(terms - 1):
        out.append(x.astype(BF16))
        x = x - out[-1].astype(F32)
    return out + [x.astype(BF16)]


def _cumsum_rows(tri_bf16, x, terms=3):
    return sum(jnp.dot(tri_bf16, t, preferred_element_type=F32) for t in _split_bf16(x, terms))


def _cumsum_cols(x, tri_bf16, terms=3):
    return sum(lax.dot_general(t, tri_bf16, NT, preferred_element_type=F32) for t in _split_bf16(x, terms))


def _ada_kernel(c_ref, w_ref, b_ref, o_ref):
    s = _silu(c_ref[...])
    o_ref[...] = jnp.dot(s.astype(BF16), w_ref[...].astype(BF16), preferred_element_type=F32) + b_ref[...]


def _ada(cvec, w, b):
    R, D = cvec.shape
    N = w.shape[1]
    tn = D
    return pl.pallas_call(
        _ada_kernel, grid=(N // tn,),
        in_specs=[pl.BlockSpec((R, D), lambda n: (0, 0)),
                  pl.BlockSpec((D, tn), lambda n: (0, n)),
                  pl.BlockSpec((1, tn), lambda n: (0, n))],
        out_specs=pl.BlockSpec((R, tn), lambda n: (0, n)),
        out_shape=jax.ShapeDtypeStruct((R, N), F32),
        compiler_params=_params("arbitrary"), name="ada",
    )(cvec, w, b.reshape(1, N))


def _inproj_kernel(x_ref, nw_ref, mod_ref, w_ref, bias_ref, *rest, n_ctx, rope_cols):
    if rope_cols is None:
        o_ref, hn_ref = rest
    else:
        cos_ref, sin_ref, o_ref, hn_ref = rest
    i = pl.program_id(1)
    n = pl.program_id(2)
    tm = x_ref.shape[1]

    @pl.when(pl.program_id(2) == 0)
    def _():
        x = x_ref[0]
        y = x * lax.rsqrt(jnp.mean(x * x, axis=-1, keepdims=True) + EPS) * nw_ref[...]
        row = i * tm + lax.broadcasted_iota(jnp.int32, (tm, 1), 0)
        is_ctx = row < n_ctx
        shift = jnp.where(is_ctx, mod_ref[0, 0:1, :], mod_ref[0, 2:3, :])
        scale = jnp.where(is_ctx, mod_ref[0, 1:2, :], mod_ref[0, 3:4, :])
        hn_ref[...] = (y * (1.0 + scale) + shift).astype(BF16)

    def project():
        return jnp.dot(hn_ref[...], w_ref[...], preferred_element_type=F32) + bias_ref[...]

    if rope_cols is None:
        o_ref[0] = project().astype(o_ref.dtype)
    else:
        tile, q_col, k_col = rope_cols
        ml_w = ML_HEADS * ML_DIM

        @pl.when(n == tile)
        def _():
            acc = project()
            cos, sin = cos_ref[...], sin_ref[...]
            for c in range(0, acc.shape[1], ML_DIM):
                cs = slice(c, c + ML_DIM)
                if q_col <= c < q_col + ml_w:
                    o_ref[0, :, cs] = _rope(acc[:, cs], cos, sin).astype(o_ref.dtype)
                elif k_col <= c < k_col + ml_w:
                    o_ref[0, :, cs] = (_rope(acc[:, cs], cos, sin) * (ML_DIM ** -0.5)).astype(o_ref.dtype)
                else:
                    o_ref[0, :, cs] = acc[:, cs].astype(o_ref.dtype)

        @pl.when(n != tile)
        def _():
            o_ref[0] = project().astype(o_ref.dtype)


def _inproj(xc, norm_w, mod, w, bias, n_ctx, tm, tn, out_dtype, rope=None):
    B, S, D = xc.shape
    N = w.shape[1]
    in_specs = [pl.BlockSpec((1, tm, D), lambda b, i, n: (b, i, 0)),
                pl.BlockSpec((1, D), lambda b, i, n: (0, 0)),
                pl.BlockSpec((1, 4, D), lambda b, i, n: (b, 0, 0)),
                pl.BlockSpec((D, tn), lambda b, i, n: (0, n)),
                pl.BlockSpec((1, tn), lambda b, i, n: (0, n))]
    args = [xc, norm_w.reshape(1, D), mod, w, bias]
    if rope is not None:
        in_specs += [pl.BlockSpec((tm, ML_DIM), lambda b, i, n: (i, 0))] * 2
        args += list(rope[:2])
    return pl.pallas_call(
        functools.partial(_inproj_kernel, n_ctx=n_ctx, rope_cols=None if rope is None else tuple(rope[2:])),
        grid=(B, S // tm, N // tn),
        in_specs=in_specs,
        out_specs=pl.BlockSpec((1, tm, tn), lambda b, i, n: (b, i, n)),
        out_shape=jax.ShapeDtypeStruct((B, S, N), out_dtype),
        scratch_shapes=[pltpu.VMEM((tm, D), BF16)],
        compiler_params=_params("parallel", "parallel", "arbitrary"), name="inproj",
    )(*args)


def _na_bias_tables(rpb, rows):
    H = rpb.shape[0]
    qr, ka = np.arange(NA_QROWS)[:, None], np.arange(NA_KROWS)[None, :]
    qc, kc = np.arange(GRID_W)[:, None], np.arange(GRID_W)[None, :]
    c0 = np.clip(qc - NA_WIN_W // 2, 0, GRID_W - NA_WIN_W)
    col_ok = (kc >= c0) & (kc < c0 + NA_WIN_W)
    col_sel = ((kc - qc + NA_WIN_W - 1)[..., None] == np.arange(2 * NA_WIN_W - 1)) & col_ok[..., None]
    nblk = rows // NA_QROWS
    tables = []
    for blk in (0, 1, nblk - 1):
        kr0 = int(np.clip(NA_QROWS * blk - NA_WIN_H // 2, 0, rows - NA_KROWS))
        r = NA_QROWS * blk + qr
        r0 = np.clip(r - NA_WIN_H // 2, 0, rows - NA_WIN_H)
        krow = kr0 + ka
        row_ok = (krow >= r0) & (krow < r0 + NA_WIN_H)
        row_sel = ((krow - r + NA_WIN_H - 1)[..., None] == np.arange(2 * NA_WIN_H - 1)) & row_ok[..., None]
        tab = jnp.einsum("rai,hij,qkj->hrqak", row_sel.astype(np.float32), rpb.astype(F32),
                         col_sel.astype(np.float32), precision=HIGHEST)
        valid = row_ok[:, None, :, None] & col_ok[None, :, None, :]
        tables.append(jnp.where(valid[None], tab * LOG2E, NEG).reshape(H, NA_QROWS * GRID_W, NA_KROWS * GRID_W))
    return jnp.stack(tables).astype(F32)


def _na_kernel(q_ref, k_ref, v_ref, bias_ref, qn_ref, kn_ref, ones_ref, o_ref, kt_s, q0_s, q1_s, v0_s, v1_s,
               *, n_ctx, rows):
    nq = NA_QROWS * GRID_W
    win = NA_KROWS * GRID_W
    nblk = rows // NA_QROWS
    lane = lax.broadcasted_iota(jnp.int32, (1, LANES), 1)
    in_h0 = lane < NA_DIM
    qh_s, vh_s = (q0_s, q1_s), (v0_s, v1_s)
    den_lane = (NA_DIM, 0)

    def rms(x, w):
        ss = jnp.dot((x * x).astype(BF16), ones_ref[...], preferred_element_type=F32)
        return x * lax.rsqrt(ss * (1.0 / NA_DIM) + EPS) * w

    kt_s[...] = rms(k_ref[0].astype(F32), kn_ref[...]).T.astype(BF16)
    q = rms(q_ref[0].astype(F32), qn_ref[...]) * (NA_DIM ** -0.5 * LOG2E)
    q0_s[...] = jnp.where(in_h0, q, 0.0).astype(BF16)
    q1_s[...] = jnp.where(in_h0, 0.0, q).astype(BF16)
    v = v_ref[0].astype(F32)
    v0_s[...] = jnp.where(in_h0, v, jnp.where(lane == den_lane[0], 1.0, 0.0)).astype(BF16)
    v1_s[...] = jnp.where(in_h0, jnp.where(lane == den_lane[1], 1.0, 0.0), v).astype(BF16)

    def softmax_pv(h, s_parts, bias, v_starts):
        if bias is not None:
            s_parts = [s_parts[0] + bias] + s_parts[1:]
        m = functools.reduce(jnp.maximum, [jnp.max(s, axis=1, keepdims=True) for s in s_parts])
        return sum(jnp.dot(jnp.exp2(s - m).astype(BF16), vh_s[h][pl.ds(st, s.shape[1]), :],
                           preferred_element_type=F32) for s, st in zip(s_parts, v_starts))

    def store(q_start, o):
        den = [o[h][:, den_lane[h]:den_lane[h] + 1] for h in range(2)]
        o_ref[0, pl.ds(q_start, nq), :] = jnp.where(in_h0, o[0] / den[0], o[1] / den[1]).astype(o_ref.dtype)

    kc = kt_s[:, 0:n_ctx]
    s_ctx = [jnp.dot(qh_s[h][0:n_ctx, :], kc, preferred_element_type=F32) for h in range(2)]
    store(0, [softmax_pv(h, [s_ctx[h]], None, [0]) for h in range(2)])

    def block_pos(blk):
        q_start = pl.multiple_of(n_ctx + nq * blk, nq)
        kr0 = jnp.clip(NA_QROWS * blk - NA_WIN_H // 2, 0, rows - NA_KROWS)
        return q_start, pl.multiple_of(n_ctx + GRID_W * kr0, nq)

    def logits(h, blk):
        q_start, k_start = block_pos(blk)
        qh = qh_s[h][pl.ds(q_start, nq), :]
        return [jnp.dot(qh, kt_s[:, pl.ds(k_start, win)], preferred_element_type=F32),
                jnp.dot(qh, kc, preferred_element_type=F32)]

    def attend(h, blk, s_parts):
        case = jnp.where(blk == 0, 0, jnp.where(blk == nblk - 1, 2, 1))
        return softmax_pv(h, s_parts, bias_ref[case, h], [block_pos(blk)[1], 0])

    def two_blocks(it, carry):
        b0, b1 = 2 * it, 2 * it + 1
        s00, s10, s01 = logits(0, b0), logits(1, b0), logits(0, b1)
        o00 = attend(0, b0, s00)
        s11 = logits(1, b1)
        o10 = attend(1, b0, s10)
        o01 = attend(0, b1, s01)
        o11 = attend(1, b1, s11)
        store(block_pos(b0)[0], [o00, o10])
        store(block_pos(b1)[0], [o01, o11])
        return carry

    lax.fori_loop(0, nblk // 2, two_blocks, 0)


def _na_attention(P, bias_tab, q_norm, k_norm, n_ctx):
    B, S, _ = P.shape
    width = NA_HEADS * NA_DIM
    npair = width // LANES
    nq = NA_QROWS * GRID_W
    rows = (S - n_ctx) // GRID_W
    assert n_ctx == nq and rows % (2 * NA_QROWS) == 0
    win = NA_KROWS * GRID_W
    hp = LANES // NA_DIM
    ones = jnp.asarray(np.kron(np.eye(hp), np.ones((NA_DIM, NA_DIM))), BF16)
    qn = jnp.tile(q_norm, hp).reshape(1, LANES)
    kn = jnp.tile(k_norm, hp).reshape(1, LANES)
    seq = lambda c: pl.BlockSpec((1, S, LANES), lambda b, p: (b, 0, c * npair + p))
    vec = pl.BlockSpec((1, LANES), lambda b, p: (0, 0))
    return pl.pallas_call(
        functools.partial(_na_kernel, n_ctx=n_ctx, rows=rows),
        grid=(B, npair),
        in_specs=[seq(0), seq(1), seq(2),
                  pl.BlockSpec((bias_tab.shape[0], hp, nq, win), lambda b, p: (0, p, 0, 0)),
                  vec, vec, pl.BlockSpec((LANES, LANES), lambda b, p: (0, 0))],
        out_specs=seq(0),
        out_shape=jax.ShapeDtypeStruct((B, S, width), BF16),
        scratch_shapes=[pltpu.VMEM((LANES, S), BF16)] + [pltpu.VMEM((S, LANES), BF16)] * 4,
        compiler_params=_params("parallel", "parallel"), name="na_attention",
    )(P, P, P, bias_tab, qn, kn, ones)


def _rope_tables(n_ctx, T):
    half = ML_DIM // 4
    t = jnp.arange(T)
    freqs = ROPE_BASE ** (-jnp.arange(half, dtype=F32) / half)
    ang_r = (t // GRID_W).astype(F32)[:, None] * freqs[None, :]
    ang_c = (t % GRID_W).astype(F32)[:, None] * freqs[None, :]
    cos = jnp.concatenate([jnp.cos(ang_r)] * 2 + [jnp.cos(ang_c)] * 2, axis=-1)
    sin = jnp.concatenate([-jnp.sin(ang_r), jnp.sin(ang_r), -jnp.sin(ang_c), jnp.sin(ang_c)], axis=-1)
    cos = jnp.concatenate([jnp.ones((n_ctx, ML_DIM), F32), cos], axis=0)
    sin = jnp.concatenate([jnp.zeros((n_ctx, ML_DIM), F32), sin], axis=0)
    return cos, sin


def _rope(x, cos, sin):
    xr = jnp.concatenate([pltpu.roll(x[:, a * LANES:(a + 1) * LANES], LANES // 2, axis=1)
                          for a in range(x.shape[1] // LANES)], axis=1)
    return x * cos + xr * sin


def _mlstm_kernel(q_ref, k_ref, v_ref, g_ref, o_ref, c_ref, n_ref, m_ref, *, reverse):
    L = q_ref.shape[1]

    @pl.when(pl.program_id(1) == 0)
    def _():
        c_ref[...] = jnp.zeros_like(c_ref)
        n_ref[...] = jnp.zeros_like(n_ref)
        m_ref[...] = jnp.zeros_like(m_ref)

    t_idx = lax.broadcasted_iota(jnp.int32, (L, L), 0)
    s_idx = lax.broadcasted_iota(jnp.int32, (L, L), 1)
    mask = (s_idx >= t_idx) if reverse else (s_idx <= t_idx)
    tri = mask.astype(BF16)
    last = 0 if reverse else L - 1

    g = g_ref[0]
    lsg = _log_sigmoid(g)
    g_t = g.T
    b_cols = _cumsum_rows(tri, lsg)
    b_rows = _cumsum_cols(lsg.T, tri)
    ones = jnp.ones((L, LANES), BF16)

    heads = range(ML_HEADS)
    hs = [slice(h * ML_DIM, (h + 1) * ML_DIM) for h in heads]
    qb, kb, vb, kwb, decay, m_t, carry_w, m_new, dec = ([None] * ML_HEADS for _ in range(9))
    for h in heads:
        ii = (2 if reverse else 0) * ML_HEADS + h
        fi = (3 if reverse else 1) * ML_HEADS + h
        ig_col, ig_row = g[:, ii:ii + 1], g_t[ii:ii + 1, :]
        b_col, b_row = b_cols[:, fi:fi + 1], b_rows[fi:fi + 1, :]
        b_last = b_col[last:last + 1, :]
        qb[h], kb[h], vb[h] = q_ref[0, :, hs[h]], k_ref[0, :, hs[h]], v_ref[0, :, hs[h]]
        m_prev = m_ref[h, 0:1, 0:1]
        dmat = jnp.where(mask, b_col - b_row + ig_row, NEG)
        m_inter = b_col + m_prev
        m_t[h] = jnp.maximum(jnp.max(dmat, axis=1, keepdims=True), m_inter)
        decay[h] = jnp.exp(dmat - m_t[h])
        carry_w[h] = jnp.broadcast_to(jnp.exp(m_inter - m_t[h]), (L, LANES))
        g_col = b_last - b_col + ig_col
        m_new[h] = jnp.maximum(b_last + m_prev, jnp.max(g_col, axis=0, keepdims=True))
        kwb[h] = (kb[h].astype(F32) * jnp.exp(g_col - m_new[h])).astype(BF16)
        dec[h] = jnp.exp(b_last + m_prev - m_new[h])

    qk = [lax.dot_general(qb[h], kb[h], NT, preferred_element_type=F32) for h in heads]
    qc = [jnp.dot(qb[h], c_ref[h].astype(BF16), preferred_element_type=F32) for h in heads]
    qn = [jnp.dot(qb[h], n_ref[h].astype(BF16), preferred_element_type=F32) for h in heads]
    upd = [lax.dot_general(kwb[h], vb[h], TN, preferred_element_type=F32) for h in heads]
    kw_sum = [lax.dot_general(kwb[h], ones, TN, preferred_element_type=F32) for h in heads]
    sb = [(qk[h] * decay[h]).astype(BF16) for h in heads]
    sv = [jnp.dot(sb[h], vb[h], preferred_element_type=F32) for h in heads]
    s_sum = [jnp.dot(sb[h], ones, preferred_element_type=F32) for h in heads]
    for h in heads:
        den = s_sum[h] + carry_w[h] * qn[h]
        inv = 1.0 / jnp.maximum(jnp.abs(den), jnp.exp(-m_t[h]))
        num = sv[h] + jnp.concatenate([carry_w[h]] * (ML_DIM // LANES), axis=1) * qc[h]
        o_ref[0, :, hs[h]] = (num * jnp.concatenate([inv] * (ML_DIM // LANES), axis=1)).astype(o_ref.dtype)
        c_ref[h] = dec[h] * c_ref[h] + upd[h]
        n_ref[h] = dec[h] * n_ref[h] + kw_sum[h]
        m_ref[h] = jnp.broadcast_to(m_new[h], m_ref.shape[1:])


def _mlstm_scan(P, gates, n_ctx, reverse):
    B, S, _ = P.shape
    L = ML_CHUNK
    width = ML_HEADS * ML_DIM
    assert n_ctx == L and S % L == 0
    nch = S // L

    def cidx(j):
        return jnp.where(j == 0, 0, nch - j) if reverse else j

    return pl.pallas_call(
        functools.partial(_mlstm_kernel, reverse=reverse),
        grid=(B, nch),
        in_specs=[pl.BlockSpec((1, L, width), lambda b, j: (b, cidx(j), 4)),
                  pl.BlockSpec((1, L, width), lambda b, j: (b, cidx(j), 5)),
                  pl.BlockSpec((1, L, width), lambda b, j: (b, cidx(j), 6)),
                  pl.BlockSpec((1, L, LANES), lambda b, j: (b, cidx(j), 0))],
        out_specs=pl.BlockSpec((1, L, width), lambda b, j: (b, cidx(j), 0)),
        out_shape=jax.ShapeDtypeStruct((B, S, width), BF16),
        scratch_shapes=[pltpu.VMEM((ML_HEADS, ML_DIM, ML_DIM), F32),
                        pltpu.VMEM((ML_HEADS, ML_DIM, LANES), F32),
                        pltpu.VMEM((ML_HEADS, 8, LANES), F32)],
        compiler_params=_params("parallel", "arbitrary"), name="mlstm_bwd" if reverse else "mlstm_fwd",
    )(P, P, P, gates)


def _merge_ab_kernel(oa_ref, za_ref, hf_ref, hb_ref, ob_ref, zb_ref, x_ref, gate_ref, hn_ref, w_ref, o_ref, *, n_ctx):
    tm = x_ref.shape[1]
    width = oa_ref.shape[2]
    ya = oa_ref[0].astype(F32) * _silu(za_ref[0].astype(F32))
    hb = (hf_ref[0].astype(F32) + hb_ref[0].astype(F32)) * _sigmoid(ob_ref[0].astype(F32))
    parts = []
    for h in range(ML_HEADS):
        seg = hb[:, h * ML_DIM:(h + 1) * ML_DIM]
        parts.append(seg * lax.rsqrt(jnp.mean(seg * seg, axis=1, keepdims=True) + EPS))
    yb = jnp.concatenate(parts, axis=1) * hn_ref[...] * _silu(zb_ref[0].astype(F32))
    y = (jnp.dot(ya.astype(BF16), w_ref[0:width, :], preferred_element_type=F32)
         + jnp.dot(yb.astype(BF16), w_ref[width:2 * width, :], preferred_element_type=F32))
    row = pl.program_id(1) * tm + lax.broadcasted_iota(jnp.int32, (tm, 1), 0)
    gate = jnp.where(row < n_ctx, gate_ref[0, 0:1, :], gate_ref[0, 1:2, :])
    o_ref[0] = x_ref[0] + gate * y


def _merge_ab(oa, hf, hb, P, xc, gate, h_norm, w_out, n_ctx, tm):
    B, S, D = xc.shape
    width = oa.shape[2]
    tok = lambda c: pl.BlockSpec((1, tm, width), lambda b, i: (b, i, c))
    return pl.pallas_call(
        functools.partial(_merge_ab_kernel, n_ctx=n_ctx),
        grid=(B, S // tm),
        in_specs=[tok(0), tok(3), tok(0), tok(0), tok(7), tok(8),
                  pl.BlockSpec((1, tm, D), lambda b, i: (b, i, 0)),
                  pl.BlockSpec((1, 2, D), lambda b, i: (b, 0, 0)),
                  pl.BlockSpec((1, width), lambda b, i: (0, 0)),
                  pl.BlockSpec((2 * width, D), lambda b, i: (0, 0))],
        out_specs=pl.BlockSpec((1, tm, D), lambda b, i: (b, i, 0)),
        out_shape=jax.ShapeDtypeStruct((B, S, D), F32),
        compiler_params=_params("parallel", "parallel"), name="merge_ab",
    )(oa, P, hf, hb, P, P, xc, gate, h_norm.reshape(1, width), w_out)


def _gla_kernel(q_ref, f_ref, v_ref, lb_ref, o_ref, st_ref, *, reverse):
    C = HG_CHUNK
    T = q_ref.shape[1]
    nsub = T // C
    gw = q_ref.shape[2]

    @pl.when(pl.program_id(2) == 0)
    def _():
        st_ref[...] = jnp.zeros_like(st_ref)

    def causal(n, same_chunk):
        t_idx = lax.broadcasted_iota(jnp.int32, (n, n), 0)
        s_idx = lax.broadcasted_iota(jnp.int32, (n, n), 1)
        m = (s_idx >= t_idx) if reverse else (s_idx <= t_idx)
        return m & (t_idx // C == s_idx // C) if same_chunk else m

    mask = causal(C, False)
    tri = causal(T, True).astype(BF16)
    last = 0 if reverse else C - 1
    mid = C // 2 if reverse else C // 2 - 1
    order = list(range(nsub - 1, -1, -1) if reverse else range(nsub))
    heads = [slice(g * HG_DIM, (g + 1) * HG_DIM) for g in range(HG_GROUP)]
    rows = [slice(c * C, (c + 1) * C) for c in range(nsub)]

    lb = lb_ref[...]
    half = 0.5 * (1.0 - lb)
    f = (lb + half) + half * jnp.tanh(0.5 * f_ref[0].astype(F32))
    b = _cumsum_rows(tri, jnp.log(f), terms=2)
    b_mid = [b[c * C + mid:c * C + mid + 1, :] for c in range(nsub)]
    b_last = [b[c * C + last:c * C + last + 1, :] for c in range(nsub)]
    b_mid_rows = jnp.concatenate([jnp.broadcast_to(r, (C, gw)) for r in b_mid], axis=0)
    qm = _silu(q_ref[0].astype(F32)) * jnp.exp(b - b_mid_rows)
    km = (1.0 - f) * jnp.exp(b_mid_rows - b)
    e_mid = [jnp.exp(r) for r in b_mid]
    e_last_mid = [jnp.exp(l - r) for l, r in zip(b_last, b_mid)]
    dec = [jnp.exp(l) for l in b_last]
    qmb, kmb, vb = qm.astype(BF16), km.astype(BF16), v_ref[0]

    a, upd = {}, {}
    for g, sl in enumerate(heads):
        for c, rs in enumerate(rows):
            s = lax.dot_general(qmb[rs, sl], kmb[rs, sl], NT, preferred_element_type=F32)
            a[g, c] = jnp.where(mask, s, 0.0).astype(BF16)
            kh = (km[rs, sl] * e_last_mid[c][:, sl]).astype(BF16)
            upd[g, c] = lax.dot_general(vb[rs, sl], kh, TN, preferred_element_type=F32)
    intra = {gc: jnp.dot(a[gc], vb[rows[gc[1]], heads[gc[0]]], preferred_element_type=F32) for gc in a}
    for g, sl in enumerate(heads):
        st = st_ref[g]
        for c in order:
            rs = rows[c]
            qi = (qm[rs, sl] * e_mid[c][:, sl]).astype(BF16)
            o_ref[0, rs, sl] = (intra[g, c] + lax.dot_general(qi, st.astype(BF16), NT, preferred_element_type=F32)
                                ).astype(o_ref.dtype)
            st = st * dec[c][:, sl] + upd[g, c]
        st_ref[g] = st


def _gla_scan(P, lb, n_ctx, reverse):
    B, S, _ = P.shape
    C = HG_BLOCK
    width = HG_HEADS * HG_DIM
    gw = HG_GROUP * HG_DIM
    ngrp = width // gw
    assert n_ctx % C == 0 and S % C == 0 and C % HG_CHUNK == 0
    nch, nctx_ch = S // C, n_ctx // C
    fcol = (2 if reverse else 1) * ngrp

    def cidx(j):
        if not reverse:
            return j
        return jnp.where(j < nctx_ch, nctx_ch - 1 - j, nch + nctx_ch - 1 - j)

    def oidx(j):
        return jnp.maximum(cidx(j) - nctx_ch, 0) if not reverse else jnp.where(j < nctx_ch, nch - nctx_ch - 1,
                                                                                cidx(j) - nctx_ch)

    return pl.pallas_call(
        functools.partial(_gla_kernel, reverse=reverse),
        grid=(B, ngrp, nch),
        in_specs=[pl.BlockSpec((1, C, gw), lambda b, g, j: (b, cidx(j), g)),
                  pl.BlockSpec((1, C, gw), lambda b, g, j: (b, cidx(j), fcol + g)),
                  pl.BlockSpec((1, C, gw), lambda b, g, j: (b, cidx(j), 3 * ngrp + g)),
                  pl.BlockSpec((1, gw), lambda b, g, j: (0, g))],
        out_specs=pl.BlockSpec((1, C, gw), lambda b, g, j: (b, oidx(j), g)),
        out_shape=jax.ShapeDtypeStruct((B, S - n_ctx, width), BF16),
        scratch_shapes=[pltpu.VMEM((HG_GROUP, HG_DIM, HG_DIM), F32)],
        compiler_params=_params("parallel", "parallel", "arbitrary"), name="gla_bwd" if reverse else "gla_fwd",
    )(P, P, P, lb)


def _merge_c_kernel(of_ref, ob_ref, z_ref, x_ref, gate_ref, hn_ref, w_ref, o_ref):
    o = of_ref[0].astype(F32) + ob_ref[0].astype(F32)
    parts = []
    for h in range(HG_HEADS):
        seg = o[:, h * HG_DIM:(h + 1) * HG_DIM]
        parts.append(seg * lax.rsqrt(jnp.mean(seg * seg, axis=1, keepdims=True) + EPS))
    y = jnp.concatenate(parts, axis=1) * hn_ref[...] * _silu(z_ref[0].astype(F32))
    o_ref[0] = x_ref[0] + gate_ref[0] * jnp.dot(y.astype(BF16), w_ref[...], preferred_element_type=F32)


def _merge_c(of, ob, P, xc, gate, h_norm, w_out, n_ctx, tm):
    B, T, width = of.shape
    D = xc.shape[2]
    off = n_ctx // tm
    return pl.pallas_call(
        _merge_c_kernel,
        grid=(B, T // tm),
        in_specs=[pl.BlockSpec((1, tm, width), lambda b, i: (b, i, 0)),
                  pl.BlockSpec((1, tm, width), lambda b, i: (b, i, 0)),
                  pl.BlockSpec((1, tm, width), lambda b, i: (b, i + off, 4)),
                  pl.BlockSpec((1, tm, D), lambda b, i: (b, i + off, 0)),
                  pl.BlockSpec((1, 1, D), lambda b, i: (b, 0, 0)),
                  pl.BlockSpec((1, width), lambda b, i: (0, 0)),
                  pl.BlockSpec((width, D), lambda b, i: (0, 0))],
        out_specs=pl.BlockSpec((1, tm, D), lambda b, i: (b, i, 0)),
        out_shape=jax.ShapeDtypeStruct((B, T, D), F32),
        compiler_params=_params("parallel", "parallel"), name="merge_c",
    )(of, ob, P, xc, gate, h_norm.reshape(1, width), w_out)


def _pad_cols(w, n):
    return jnp.pad(w, ((0, 0), (0, n - w.shape[1])))


def kernel(x, c, ctx, c_ctx, norm_w, w_ada, b_ada, w_in_ab, b_gate_ab, q_norm_a, k_norm_a, rpb_a, h_norm_b, w_out_ab,
           w_in_c, lb_c, h_norm_c, w_out_c):
    B, T, D = x.shape
    n_ctx = ctx.shape[1]
    S = n_ctx + T
    tm_in = S // 4
    tm_out = S // 16
    xc = jnp.concatenate([ctx, x], axis=1)
    cvec = jnp.concatenate([c, c_ctx[None], jnp.zeros((16 - B - 1, D), F32)], axis=0)

    def modulation(l):
        m = _ada(cvec, w_ada[l], b_ada[l])
        shift, scale, gate = m[:, :D], m[:, D:2 * D], m[:, 2 * D:]
        ctx_row = lambda a: jnp.broadcast_to(a[B][None], (B, D))
        mod = jnp.stack([ctx_row(shift), ctx_row(scale), shift[:B], scale[:B]], axis=1)
        return mod, jnp.stack([ctx_row(gate), gate[:B]], axis=1)

    mod, gate = modulation(0)
    n_main = 9 * NA_HEADS * NA_DIM
    w_main = w_in_ab[0][:, :n_main].astype(BF16)
    w_gate = _pad_cols(w_in_ab[0][:, n_main:], LANES).astype(BF16)
    b_gate = _pad_cols(b_gate_ab[0][None], LANES)
    ml_w = ML_HEADS * ML_DIM
    P = _inproj(xc, norm_w[0], mod, w_main, jnp.zeros((1, n_main), F32), n_ctx, tm_in, n_main // 3, BF16,
                rope=_rope_tables(n_ctx, T) + (1, ml_w, 2 * ml_w))
    gates = _inproj(xc, norm_w[0], mod, w_gate, b_gate, n_ctx, tm_in, LANES, F32)
    oa = _na_attention(P, _na_bias_tables(rpb_a[0], T // GRID_W), q_norm_a[0], k_norm_a[0], n_ctx)
    hf = _mlstm_scan(P, gates, n_ctx, reverse=False)
    hb = _mlstm_scan(P, gates, n_ctx, reverse=True)
    xc = _merge_ab(oa, hf, hb, P, xc, gate, h_norm_b[0], w_out_ab[0].astype(BF16), n_ctx, tm_out)

    mod, gate = modulation(1)
    sm = jax.nn.softmax(lb_c.astype(F32), axis=0)
    lb = (jnp.cumsum(sm, axis=0) - sm[0])[1].reshape(1, -1)
    hw = HG_HEADS * HG_DIM
    P = _inproj(xc, norm_w[1], mod, w_in_c[0].astype(BF16), jnp.zeros((1, 5 * hw), F32), n_ctx, tm_in, hw, BF16)
    of = _gla_scan(P, lb, n_ctx, reverse=False)
    ob = _gla_scan(P, lb, n_ctx, reverse=True)
    return _merge_c(of, ob, P, xc, gate[:, 1:2], h_norm_c[0], w_out_c[0].astype(BF16), n_ctx, n_ctx)
```

```python
import functools

import numpy as np
import jax
import jax.numpy as jnp
from jax import lax
from jax.experimental import pallas as pl
from jax.experimental.pallas import tpu as pltpu

F32 = jnp.float32
BF16 = jnp.bfloat16
HIGHEST = lax.Precision.HIGHEST

GRID_W = 64
NA_HEADS = 16
NA_DIM = 64
NA_WIN_H = 8
NA_WIN_W = 16
ML_HEADS = 4
ML_DIM = 256
HG_HEADS = 16
HG_DIM = 128
ROPE_BASE = 10000.0
EPS = 1e-6
NEG = -1e30
LOG2E = 1.4426950408889634

LANES = 128
V7X_VMEM_LIMIT = 56 * 1024 * 1024

NA_QROWS = 4
NA_KROWS = NA_QROWS + NA_WIN_H
ML_CHUNK = 256
HG_CHUNK = 64
HG_BLOCK = 256
HG_GROUP = 8

NT = (((1,), (1,)), ((), ()))
TN = (((0,), (0,)), ((), ()))


def _params(*sem):
    return pltpu.CompilerParams(dimension_semantics=sem, vmem_limit_bytes=V7X_VMEM_LIMIT)


def _sigmoid(z):
    return 0.5 * jnp.tanh(0.5 * z) + 0.5


def _silu(z):
    h = 0.5 * z
    return h + h * jnp.tanh(h)


def _log_sigmoid(z):
    return jnp.minimum(z, 0.0) - jnp.log1p(jnp.exp(-jnp.abs(z)))


def _split_bf16(x, terms):
    out = []
    for _ in range(terms - 1):
        out.append(x.astype(BF16))
        x = x - out[-1].astype(F32)
    return out + [x.astype(BF16)]


def _cumsum_rows(tri_bf16, x, terms=3):
    return sum(jnp.dot(tri_bf16, t, preferred_element_type=F32) for t in _split_bf16(x, terms))


def _cumsum_cols(x, tri_bf16, terms=3):
    return sum(lax.dot_general(t, tri_bf16, NT, preferred_element_type=F32) for t in _split_bf16(x, terms))


def _ada_kernel(c_ref, w_ref, b_ref, o_ref):
    s = _silu(c_ref[...])
    o_ref[...] = jnp.dot(s.astype(BF16), w_ref[...].astype(BF16), preferred_element_type=F32) + b_ref[...]


def _ada(cvec, w, b):
    R, D = cvec.shape
    N = w.shape[1]
    tn = D
    return pl.pallas_call(
        _ada_kernel, grid=(N // tn,),
        in_specs=[pl.BlockSpec((R, D), lambda n: (0, 0)),
                  pl.BlockSpec((D, tn), lambda n: (0, n)),
                  pl.BlockSpec((1, tn), lambda n: (0, n))],
        out_specs=pl.BlockSpec((R, tn), lambda n: (0, n)),
        out_shape=jax.ShapeDtypeStruct((R, N), F32),
        compiler_params=_params("arbitrary"), name="ada",
    )(cvec, w, b.reshape(1, N))


def _inproj_kernel(x_ref, nw_ref, mod_ref, w_ref, bias_ref, *rest, n_ctx, rope_cols):
    if rope_cols is None:
        o_ref, hn_ref = rest
    else:
        cos_ref, sin_ref, o_ref, hn_ref = rest
    i = pl.program_id(1)
    n = pl.program_id(2)
    tm = x_ref.shape[1]

    @pl.when(pl.program_id(2) == 0)
    def _():
        x = x_ref[0]
        y = x * lax.rsqrt(jnp.mean(x * x, axis=-1, keepdims=True) + EPS) * nw_ref[...]
        row = i * tm + lax.broadcasted_iota(jnp.int32, (tm, 1), 0)
        is_ctx = row < n_ctx
        shift = jnp.where(is_ctx, mod_ref[0, 0:1, :], mod_ref[0, 2:3, :])
        scale = jnp.where(is_ctx, mod_ref[0, 1:2, :], mod_ref[0, 3:4, :])
        hn_ref[...] = (y * (1.0 + scale) + shift).astype(BF16)

    def project():
        return jnp.dot(hn_ref[...], w_ref[...], preferred_element_type=F32) + bias_ref[...]

    if rope_cols is None:
        o_ref[0] = project().astype(o_ref.dtype)
    else:
        tile, q_col, k_col = rope_cols
        ml_w = ML_HEADS * ML_DIM

        @pl.when(n == tile)
        def _():
            acc = project()
            cos, sin = cos_ref[...], sin_ref[...]
            for c in range(0, acc.shape[1], ML_DIM):
                cs = slice(c, c + ML_DIM)
                if q_col <= c < q_col + ml_w:
                    o_ref[0, :, cs] = _rope(acc[:, cs], cos, sin).astype(o_ref.dtype)
                elif k_col <= c < k_col + ml_w:
                    o_ref[0, :, cs] = (_rope(acc[:, cs], cos, sin) * (ML_DIM ** -0.5)).astype(o_ref.dtype)
                else:
                    o_ref[0, :, cs] = acc[:, cs].astype(o_ref.dtype)

        @pl.when(n != tile)
        def _():
            o_ref[0] = project().astype(o_ref.dtype)


def _inproj(xc, norm_w, mod, w, bias, n_ctx, tm, tn, out_dtype, rope=None):
    B, S, D = xc.shape
    N = w.shape[1]
    in_specs = [pl.BlockSpec((1, tm, D), lambda b, i, n: (b, i, 0)),
                pl.BlockSpec((1, D), lambda b, i, n: (0, 0)),
                pl.BlockSpec((1, 4, D), lambda b, i, n: (b, 0, 0)),
                pl.BlockSpec((D, tn), lambda b, i, n: (0, n)),
                pl.BlockSpec((1, tn), lambda b, i, n: (0, n))]
    args = [xc, norm_w.reshape(1, D), mod, w, bias]
    if rope is not None:
        in_specs += [pl.BlockSpec((tm, ML_DIM), lambda b, i, n: (i, 0))] * 2
        args += list(rope[:2])
    return pl.pallas_call(
        functools.partial(_inproj_kernel, n_ctx=n_ctx, rope_cols=None if rope is None else tuple(rope[2:])),
        grid=(B, S // tm, N // tn),
        in_specs=in_specs,
        out_specs=pl.BlockSpec((1, tm, tn), lambda b, i, n: (b, i, n)),
        out_shape=jax.ShapeDtypeStruct((B, S, N), out_dtype),
        scratch_shapes=[pltpu.VMEM((tm, D), BF16)],
        compiler_params=_params("parallel", "parallel", "arbitrary"), name="inproj",
    )(*args)


def _na_bias_tables(rpb, rows):
    H, nri, nci = rpb.shape
    qc, kc = np.arange(GRID_W)[:, None], np.arange(GRID_W)[None, :]
    c0 = np.clip(qc - NA_WIN_W // 2, 0, GRID_W - NA_WIN_W)
    col_ok = (kc >= c0) & (kc < c0 + NA_WIN_W)
    col_sel = ((kc - qc + NA_WIN_W - 1)[..., None] == np.arange(nci)) & col_ok[..., None]
    sel = np.einsum("de,qkj->qdkej", np.eye(2), col_sel).reshape(GRID_W, 2, GRID_W, 2 * nci).astype(np.float32)
    rp = jnp.pad(rpb.astype(F32), ((0, 0), (1, 1), (0, 0)))
    pair = jnp.stack([rp[:, :nri + 1], rp[:, 1:]], axis=2).reshape(H, nri + 1, 2 * nci)
    tiles = jnp.einsum("hic,qdkc->hiqdk", pair, sel, precision=HIGHEST).reshape(H, nri + 1, GRID_W, 2 * GRID_W)

    qr, ka = np.arange(NA_QROWS)[:, None], np.arange(NA_KROWS)[None, :]
    nblk = rows // NA_QROWS
    tables = []
    for blk in (0, 1, nblk - 1):
        kr0 = int(np.clip(NA_QROWS * blk - NA_WIN_H // 2, 0, rows - NA_KROWS))
        r = NA_QROWS * blk + qr
        r0 = np.clip(r - NA_WIN_H // 2, 0, rows - NA_WIN_H)
        krow = kr0 + ka
        row_ok = (krow >= r0) & (krow < r0 + NA_WIN_H)
        idx = np.clip((krow - r + NA_WIN_H - 1)[:, 0::2] + 1, 0, nri)
        valid = (row_ok.reshape(NA_QROWS, NA_KROWS // 2, 1, 2, 1) & col_ok[None, None, :, None, :])
        valid = valid.reshape(NA_QROWS, NA_KROWS // 2, GRID_W, 2 * GRID_W)
        tables.append(jnp.where(valid[None], tiles[:, idx] * LOG2E, NEG))
    return jnp.stack(tables)


def _na_kernel(q_ref, k_ref, v_ref, bias_ref, qn_ref, kn_ref, ones_ref, o_ref, kt_s, q0_s, q1_s, v0_s, v1_s,
               *, n_ctx, rows):
    nq = NA_QROWS * GRID_W
    win = NA_KROWS * GRID_W
    nblk = rows // NA_QROWS
    lane = lax.broadcasted_iota(jnp.int32, (1, LANES), 1)
    in_h0 = lane < NA_DIM
    qh_s, vh_s = (q0_s, q1_s), (v0_s, v1_s)
    den_lane = (NA_DIM, 0)

    def rms(x, w):
        ss = jnp.dot((x * x).astype(BF16), ones_ref[...], preferred_element_type=F32)
        return x * lax.rsqrt(ss * (1.0 / NA_DIM) + EPS) * w

    kt_s[...] = rms(k_ref[0].astype(F32), kn_ref[...]).T.astype(BF16)
    q = rms(q_ref[0].astype(F32), qn_ref[...]) * (NA_DIM ** -0.5 * LOG2E)
    q0_s[...] = jnp.where(in_h0, q, 0.0).astype(BF16)
    q1_s[...] = jnp.where(in_h0, 0.0, q).astype(BF16)
    v = v_ref[0].astype(F32)
    v0_s[...] = jnp.where(in_h0, v, jnp.where(lane == den_lane[0], 1.0, 0.0)).astype(BF16)
    v1_s[...] = jnp.where(in_h0, jnp.where(lane == den_lane[1], 1.0, 0.0), v).astype(BF16)

    def softmax_pv(h, s_parts, bias, v_starts):
        if bias is not None:
            s_parts = [s_parts[0] + bias] + s_parts[1:]
        m = functools.reduce(jnp.maximum, [jnp.max(s, axis=1, keepdims=True) for s in s_parts])
        return sum(jnp.dot(jnp.exp2(s - m).astype(BF16), vh_s[h][pl.ds(st, s.shape[1]), :],
                           preferred_element_type=F32) for s, st in zip(s_parts, v_starts))

    def store(q_start, o):
        den = [o[h][:, den_lane[h]:den_lane[h] + 1] for h in range(2)]
        o_ref[0, pl.ds(q_start, nq), :] = jnp.where(in_h0, o[0] / den[0], o[1] / den[1]).astype(o_ref.dtype)

    kc = kt_s[:, 0:n_ctx]
    s_ctx = [jnp.dot(qh_s[h][0:n_ctx, :], kc, preferred_element_type=F32) for h in range(2)]
    store(0, [softmax_pv(h, [s_ctx[h]], None, [0]) for h in range(2)])

    def block_pos(blk):
        q_start = pl.multiple_of(n_ctx + nq * blk, nq)
        kr0 = jnp.clip(NA_QROWS * blk - NA_WIN_H // 2, 0, rows - NA_KROWS)
        return q_start, pl.multiple_of(n_ctx + GRID_W * kr0, nq)

    def logits(h, blk):
        q_start, k_start = block_pos(blk)
        qh = qh_s[h][pl.ds(q_start, nq), :]
        return [jnp.dot(qh, kt_s[:, pl.ds(k_start, win)], preferred_element_type=F32),
                jnp.dot(qh, kc, preferred_element_type=F32)]

    def attend(h, blk, s_parts):
        case = jnp.where(blk == 0, 0, jnp.where(blk == nblk - 1, 2, 1))
        tiles = bias_ref[case, h]
        bias = jnp.concatenate([jnp.concatenate([tiles[r, a] for a in range(NA_KROWS // 2)], axis=1)
                                for r in range(NA_QROWS)], axis=0)
        return softmax_pv(h, s_parts, bias, [block_pos(blk)[1], 0])

    def two_blocks(it, carry):
        b0, b1 = 2 * it, 2 * it + 1
        s00, s10, s01 = logits(0, b0), logits(1, b0), logits(0, b1)
        o00 = attend(0, b0, s00)
        s11 = logits(1, b1)
        o10 = attend(1, b0, s10)
        o01 = attend(0, b1, s01)
        o11 = attend(1, b1, s11)
        store(block_pos(b0)[0], [o00, o10])
        store(block_pos(b1)[0], [o01, o11])
        return carry

    lax.fori_loop(0, nblk // 2, two_blocks, 0)


def _na_attention(P, bias_tab, q_norm, k_norm, n_ctx):
    B, S, _ = P.shape
    width = NA_HEADS * NA_DIM
    npair = width // LANES
    nq = NA_QROWS * GRID_W
    rows = (S - n_ctx) // GRID_W
    assert n_ctx == nq and rows % (2 * NA_QROWS) == 0
    win = NA_KROWS * GRID_W
    hp = LANES // NA_DIM
    ones = jnp.asarray(np.kron(np.eye(hp), np.ones((NA_DIM, NA_DIM))), BF16)
    qn = jnp.tile(q_norm, hp).reshape(1, LANES)
    kn = jnp.tile(k_norm, hp).reshape(1, LANES)
    seq = lambda c: pl.BlockSpec((1, S, LANES), lambda b, p: (b, 0, c * npair + p))
    vec = pl.BlockSpec((1, LANES), lambda b, p: (0, 0))
    return pl.pallas_call(
        functools.partial(_na_kernel, n_ctx=n_ctx, rows=rows),
        grid=(B, npair),
        in_specs=[seq(0), seq(1), seq(2),
                  pl.BlockSpec((bias_tab.shape[0], hp) + bias_tab.shape[2:], lambda b, p: (0, p, 0, 0, 0, 0)),
                  vec, vec, pl.BlockSpec((LANES, LANES), lambda b, p: (0, 0))],
        out_specs=seq(0),
        out_shape=jax.ShapeDtypeStruct((B, S, width), BF16),
        scratch_shapes=[pltpu.VMEM((LANES, S), BF16)] + [pltpu.VMEM((S, LANES), BF16)] * 4,
        compiler_params=_params("parallel", "parallel"), name="na_attention",
    )(P, P, P, bias_tab, qn, kn, ones)


def _rope_tables(n_ctx, T):
    half = ML_DIM // 4
    t = jnp.arange(T)
    freqs = ROPE_BASE ** (-jnp.arange(half, dtype=F32) / half)
    ang_r = (t // GRID_W).astype(F32)[:, None] * freqs[None, :]
    ang_c = (t % GRID_W).astype(F32)[:, None] * freqs[None, :]
    cos = jnp.concatenate([jnp.cos(ang_r)] * 2 + [jnp.cos(ang_c)] * 2, axis=-1)
    sin = jnp.concatenate([-jnp.sin(ang_r), jnp.sin(ang_r), -jnp.sin(ang_c), jnp.sin(ang_c)], axis=-1)
    cos = jnp.concatenate([jnp.ones((n_ctx, ML_DIM), F32), cos], axis=0)
    sin = jnp.concatenate([jnp.zeros((n_ctx, ML_DIM), F32), sin], axis=0)
    return cos, sin


def _rope(x, cos, sin):
    xr = jnp.concatenate([pltpu.roll(x[:, a * LANES:(a + 1) * LANES], LANES // 2, axis=1)
                          for a in range(x.shape[1] // LANES)], axis=1)
    return x * cos + xr * sin


def _mlstm_kernel(q_ref, k_ref, v_ref, g_ref, o_ref, c_ref, n_ref, m_ref, *, reverse):
    L = q_ref.shape[1]

    @pl.when(pl.program_id(1) == 0)
    def _():
        c_ref[...] = jnp.zeros_like(c_ref)
        n_ref[...] = jnp.zeros_like(n_ref)
        m_ref[...] = jnp.zeros_like(m_ref)

    t_idx = lax.broadcasted_iota(jnp.int32, (L, L), 0)
    s_idx = lax.broadcasted_iota(jnp.int32, (L, L), 1)
    mask = (s_idx >= t_idx) if reverse else (s_idx <= t_idx)
    tri = mask.astype(BF16)
    last = 0 if reverse else L - 1

    g = g_ref[0]
    lsg = _log_sigmoid(g)
    g_t = g.T
    b_cols = _cumsum_rows(tri, lsg)
    b_rows = _cumsum_cols(lsg.T, tri)
    ones = jnp.ones((L, LANES), BF16)

    heads = range(ML_HEADS)
    hs = [slice(h * ML_DIM, (h + 1) * ML_DIM) for h in heads]
    qb, kb, vb, kwb, decay, m_t, carry_w, m_new, dec = ([None] * ML_HEADS for _ in range(9))
    for h in heads:
        ii = (2 if reverse else 0) * ML_HEADS + h
        fi = (3 if reverse else 1) * ML_HEADS + h
        ig_col, ig_row = g[:, ii:ii + 1], g_t[ii:ii + 1, :]
        b_col, b_row = b_cols[:, fi:fi + 1], b_rows[fi:fi + 1, :]
        b_last = b_col[last:last + 1, :]
        qb[h], kb[h], vb[h] = q_ref[0, :, hs[h]], k_ref[0, :, hs[h]], v_ref[0, :, hs[h]]
        m_prev = m_ref[h, 0:1, 0:1]
        dmat = jnp.where(mask, b_col - b_row + ig_row, NEG)
        m_inter = b_col + m_prev
        m_t[h] = jnp.maximum(jnp.max(dmat, axis=1, keepdims=True), m_inter)
        decay[h] = jnp.exp(dmat - m_t[h])
        carry_w[h] = jnp.broadcast_to(jnp.exp(m_inter - m_t[h]), (L, LANES))
        g_col = b_last - b_col + ig_col
        m_new[h] = jnp.maximum(b_last + m_prev, jnp.max(g_col, axis=0, keepdims=True))
        kwb[h] = (kb[h].astype(F32) * jnp.exp(g_col - m_new[h])).astype(BF16)
        dec[h] = jnp.exp(b_last + m_prev - m_new[h])

    qk = [lax.dot_general(qb[h], kb[h], NT, preferred_element_type=F32) for h in heads]
    qc = [jnp.dot(qb[h], c_ref[h].astype(BF16), preferred_element_type=F32) for h in heads]
    qn = [jnp.dot(qb[h], n_ref[h].astype(BF16), preferred_element_type=F32) for h in heads]
    upd = [lax.dot_general(kwb[h], vb[h], TN, preferred_element_type=F32) for h in heads]
    kw_sum = [lax.dot_general(kwb[h], ones, TN, preferred_element_type=F32) for h in heads]
    sb = [(qk[h] * decay[h]).astype(BF16) for h in heads]
    sv = [jnp.dot(sb[h], vb[h], preferred_element_type=F32) for h in heads]
    s_sum = [jnp.dot(sb[h], ones, preferred_element_type=F32) for h in heads]
    for h in heads:
        den = s_sum[h] + carry_w[h] * qn[h]
        inv = 1.0 / jnp.maximum(jnp.abs(den), jnp.exp(-m_t[h]))
        num = sv[h] + jnp.concatenate([carry_w[h]] * (ML_DIM // LANES), axis=1) * qc[h]
        o_ref[0, :, hs[h]] = (num * jnp.concatenate([inv] * (ML_DIM // LANES), axis=1)).astype(o_ref.dtype)
        c_ref[h] = dec[h] * c_ref[h] + upd[h]
        n_ref[h] = dec[h] * n_ref[h] + kw_sum[h]
        m_ref[h] = jnp.broadcast_to(m_new[h], m_ref.shape[1:])


def _mlstm_scan(P, gates, n_ctx, reverse):
    B, S, _ = P.shape
    L = ML_CHUNK
    width = ML_HEADS * ML_DIM
    assert n_ctx == L and S % L == 0
    nch = S // L

    def cidx(j):
        return jnp.where(j == 0, 0, nch - j) if reverse else j

    return pl.pallas_call(
        functools.partial(_mlstm_kernel, reverse=reverse),
        grid=(B, nch),
        in_specs=[pl.BlockSpec((1, L, width), lambda b, j: (b, cidx(j), 4)),
                  pl.BlockSpec((1, L, width), lambda b, j: (b, cidx(j), 5)),
                  pl.BlockSpec((1, L, width), lambda b, j: (b, cidx(j), 6)),
                  pl.BlockSpec((1, L, LANES), lambda b, j: (b, cidx(j), 0))],
        out_specs=pl.BlockSpec((1, L, width), lambda b, j: (b, cidx(j), 0)),
        out_shape=jax.ShapeDtypeStruct((B, S, width), BF16),
        scratch_shapes=[pltpu.VMEM((ML_HEADS, ML_DIM, ML_DIM), F32),
                        pltpu.VMEM((ML_HEADS, ML_DIM, LANES), F32),
                        pltpu.VMEM((ML_HEADS, 8, LANES), F32)],
        compiler_params=_params("parallel", "arbitrary"), name="mlstm_bwd" if reverse else "mlstm_fwd",
    )(P, P, P, gates)


def _merge_ab_kernel(oa_ref, za_ref, hf_ref, hb_ref, ob_ref, zb_ref, x_ref, gate_ref, hn_ref, w_ref, o_ref, *, n_ctx):
    tm = x_ref.shape[1]
    width = oa_ref.shape[2]
    ya = oa_ref[0].astype(F32) * _silu(za_ref[0].astype(F32))
    hb = (hf_ref[0].astype(F32) + hb_ref[0].astype(F32)) * _sigmoid(ob_ref[0].astype(F32))
    parts = []
    for h in range(ML_HEADS):
        seg = hb[:, h * ML_DIM:(h + 1) * ML_DIM]
        parts.append(seg * lax.rsqrt(jnp.mean(seg * seg, axis=1, keepdims=True) + EPS))
    yb = jnp.concatenate(parts, axis=1) * hn_ref[...] * _silu(zb_ref[0].astype(F32))
    y = (jnp.dot(ya.astype(BF16), w_ref[0:width, :], preferred_element_type=F32)
         + jnp.dot(yb.astype(BF16), w_ref[width:2 * width, :], preferred_element_type=F32))
    row = pl.program_id(1) * tm + lax.broadcasted_iota(jnp.int32, (tm, 1), 0)
    gate = jnp.where(row < n_ctx, gate_ref[0, 0:1, :], gate_ref[0, 1:2, :])
    o_ref[0] = x_ref[0] + gate * y


def _merge_ab(oa, hf, hb, P, xc, gate, h_norm, w_out, n_ctx, tm):
    B, S, D = xc.shape
    width = oa.shape[2]
    tok = lambda c: pl.BlockSpec((1, tm, width), lambda b, i: (b, i, c))
    return pl.pallas_call(
        functools.partial(_merge_ab_kernel, n_ctx=n_ctx),
        grid=(B, S // tm),
        in_specs=[tok(0), tok(3), tok(0), tok(0), tok(7), tok(8),
                  pl.BlockSpec((1, tm, D), lambda b, i: (b, i, 0)),
                  pl.BlockSpec((1, 2, D), lambda b, i: (b, 0, 0)),
                  pl.BlockSpec((1, width), lambda b, i: (0, 0)),
                  pl.BlockSpec((2 * width, D), lambda b, i: (0, 0))],
        out_specs=pl.BlockSpec((1, tm, D), lambda b, i: (b, i, 0)),
        out_shape=jax.ShapeDtypeStruct((B, S, D), F32),
        compiler_params=_params("parallel", "parallel"), name="merge_ab",
    )(oa, P, hf, hb, P, P, xc, gate, h_norm.reshape(1, width), w_out)


def _gla_kernel(q_ref, f_ref, v_ref, lb_ref, o_ref, st_ref, *, reverse):
    C = HG_CHUNK
    T = q_ref.shape[1]
    nsub = T // C
    gw = q_ref.shape[2]

    @pl.when(pl.program_id(2) == 0)
    def _():
        st_ref[...] = jnp.zeros_like(st_ref)

    def causal(n, same_chunk):
        t_idx = lax.broadcasted_iota(jnp.int32, (n, n), 0)
        s_idx = lax.broadcasted_iota(jnp.int32, (n, n), 1)
        m = (s_idx >= t_idx) if reverse else (s_idx <= t_idx)
        return m & (t_idx // C == s_idx // C) if same_chunk else m

    mask = causal(C, False)
    tri = causal(T, True).astype(BF16)
    last = 0 if reverse else C - 1
    mid = C // 2 if reverse else C // 2 - 1
    order = list(range(nsub - 1, -1, -1) if reverse else range(nsub))
    rows = [slice(c * C, (c + 1) * C) for c in range(nsub)]
    hw = gw // 2
    nh = hw // HG_DIM

    def decay_stage(half_idx):
        cols = slice(half_idx * hw, (half_idx + 1) * hw)
        lb = lb_ref[:, cols]
        c1 = 0.5 * (1.0 - lb)
        f = (lb + c1) + c1 * jnp.tanh(0.5 * f_ref[0, :, cols].astype(F32))
        b = _cumsum_rows(tri, jnp.log(f), terms=2)
        b_mid = [b[c * C + mid:c * C + mid + 1, :] for c in range(nsub)]
        b_last = [b[c * C + last:c * C + last + 1, :] for c in range(nsub)]
        b_mid_rows = jnp.concatenate([jnp.broadcast_to(r, (C, hw)) for r in b_mid], axis=0)
        qm = _silu(q_ref[0, :, cols].astype(F32)) * jnp.exp(b - b_mid_rows)
        km = (1.0 - f) * jnp.exp(b_mid_rows - b)
        return dict(cols=cols, qm=qm, km=km, vb=v_ref[0, :, cols],
                    e_mid=[jnp.exp(r) for r in b_mid],
                    e_last_mid=[jnp.exp(l - r) for l, r in zip(b_last, b_mid)],
                    dec=[jnp.exp(l) for l in b_last])

    def score_stage(d):
        qmb, kmb, vb = d["qm"].astype(BF16), d["km"].astype(BF16), d["vb"]
        a, upd = {}, {}
        for g in range(nh):
            sl = slice(g * HG_DIM, (g + 1) * HG_DIM)
            for c, rs in enumerate(rows):
                s = lax.dot_general(qmb[rs, sl], kmb[rs, sl], NT, preferred_element_type=F32)
                a[g, c] = jnp.where(mask, s, 0.0).astype(BF16)
                kh = (d["km"][rs, sl] * d["e_last_mid"][c][:, sl]).astype(BF16)
                upd[g, c] = lax.dot_general(vb[rs, sl], kh, TN, preferred_element_type=F32)
        d["upd"] = upd
        d["intra"] = {(g, c): jnp.dot(a[g, c], vb[rows[c], g * HG_DIM:(g + 1) * HG_DIM],
                                      preferred_element_type=F32) for (g, c) in a}

    def state_stage(half_idx, d):
        for g in range(nh):
            sl = slice(g * HG_DIM, (g + 1) * HG_DIM)
            out_cols = slice(half_idx * hw + g * HG_DIM, half_idx * hw + (g + 1) * HG_DIM)
            st = st_ref[half_idx * nh + g]
            for c in order:
                rs = rows[c]
                qi = (d["qm"][rs, sl] * d["e_mid"][c][:, sl]).astype(BF16)
                o_ref[0, rs, out_cols] = (d["intra"][g, c] + lax.dot_general(
                    qi, st.astype(BF16), NT, preferred_element_type=F32)).astype(o_ref.dtype)
                st = st * d["dec"][c][:, sl] + d["upd"][g, c]
            st_ref[half_idx * nh + g] = st

    d0 = decay_stage(0)
    score_stage(d0)
    d1 = decay_stage(1)
    state_stage(0, d0)
    score_stage(d1)
    state_stage(1, d1)


def _gla_scan(P, lb, n_ctx, reverse):
    B, S, _ = P.shape
    C = HG_BLOCK
    width = HG_HEADS * HG_DIM
    gw = HG_GROUP * HG_DIM
    ngrp = width // gw
    assert n_ctx % C == 0 and S % C == 0 and C % HG_CHUNK == 0
    nch, nctx_ch = S // C, n_ctx // C
    fcol = (2 if reverse else 1) * ngrp

    def cidx(j):
        if not reverse:
            return j
        return jnp.where(j < nctx_ch, nctx_ch - 1 - j, nch + nctx_ch - 1 - j)

    def oidx(j):
        return jnp.maximum(cidx(j) - nctx_ch, 0) if not reverse else jnp.where(j < nctx_ch, nch - nctx_ch - 1,
                                                                                cidx(j) - nctx_ch)

    return pl.pallas_call(
        functools.partial(_gla_kernel, reverse=reverse),
        grid=(B, ngrp, nch),
        in_specs=[pl.BlockSpec((1, C, gw), lambda b, g, j: (b, cidx(j), g)),
                  pl.BlockSpec((1, C, gw), lambda b, g, j: (b, cidx(j), fcol + g)),
                  pl.BlockSpec((1, C, gw), lambda b, g, j: (b, cidx(j), 3 * ngrp + g)),
                  pl.BlockSpec((1, gw), lambda b, g, j: (0, g))],
        out_specs=pl.BlockSpec((1, C, gw), lambda b, g, j: (b, oidx(j), g)),
        out_shape=jax.ShapeDtypeStruct((B, S - n_ctx, width), BF16),
        scratch_shapes=[pltpu.VMEM((HG_GROUP, HG_DIM, HG_DIM), F32)],
        compiler_params=_params("parallel", "parallel", "arbitrary"), name="gla_bwd" if reverse else "gla_fwd",
    )(P, P, P, lb)


def _merge_c_kernel(of_ref, ob_ref, z_ref, x_ref, gate_ref, hn_ref, w_ref, o_ref):
    o = of_ref[0].astype(F32) + ob_ref[0].astype(F32)
    parts = []
    for h in range(HG_HEADS):
        seg = o[:, h * HG_DIM:(h + 1) * HG_DIM]
        parts.append(seg * lax.rsqrt(jnp.mean(seg * seg, axis=1, keepdims=True) + EPS))
    y = jnp.concatenate(parts, axis=1) * hn_ref[...] * _silu(z_ref[0].astype(F32))
    o_ref[0] = x_ref[0] + gate_ref[0] * jnp.dot(y.astype(BF16), w_ref[...], preferred_element_type=F32)


def _merge_c(of, ob, P, xc, gate, h_norm, w_out, n_ctx, tm):
    B, T, width = of.shape
    D = xc.shape[2]
    off = n_ctx // tm
    return pl.pallas_call(
        _merge_c_kernel,
        grid=(B, T // tm),
        in_specs=[pl.BlockSpec((1, tm, width), lambda b, i: (b, i, 0)),
                  pl.BlockSpec((1, tm, width), lambda b, i: (b, i, 0)),
                  pl.BlockSpec((1, tm, width), lambda b, i: (b, i + off, 4)),
                  pl.BlockSpec((1, tm, D), lambda b, i: (b, i + off, 0)),
                  pl.BlockSpec((1, 1, D), lambda b, i: (b, 0, 0)),
                  pl.BlockSpec((1, width), lambda b, i: (0, 0)),
                  pl.BlockSpec((width, D), lambda b, i: (0, 0))],
        out_specs=pl.BlockSpec((1, tm, D), lambda b, i: (b, i, 0)),
        out_shape=jax.ShapeDtypeStruct((B, T, D), F32),
        compiler_params=_params("parallel", "parallel"), name="merge_c",
    )(of, ob, P, xc, gate, h_norm.reshape(1, width), w_out)


def _pad_cols(w, n):
    return jnp.pad(w, ((0, 0), (0, n - w.shape[1])))


def kernel(x, c, ctx, c_ctx, norm_w, w_ada, b_ada, w_in_ab, b_gate_ab, q_norm_a, k_norm_a, rpb_a, h_norm_b, w_out_ab,
           w_in_c, lb_c, h_norm_c, w_out_c):
    B, T, D = x.shape
    n_ctx = ctx.shape[1]
    S = n_ctx + T
    tm_in = S // 4
    tm_out = S // 16
    xc = jnp.concatenate([ctx, x], axis=1)
    cvec = jnp.concatenate([c, c_ctx[None], jnp.zeros((16 - B - 1, D), F32)], axis=0)

    def modulation(l):
        m = _ada(cvec, w_ada[l], b_ada[l])
        shift, scale, gate = m[:, :D], m[:, D:2 * D], m[:, 2 * D:]
        ctx_row = lambda a: jnp.broadcast_to(a[B][None], (B, D))
        mod = jnp.stack([ctx_row(shift), ctx_row(scale), shift[:B], scale[:B]], axis=1)
        return mod, jnp.stack([ctx_row(gate), gate[:B]], axis=1)

    mod, gate = modulation(0)
    n_main = 9 * NA_HEADS * NA_DIM
    w_main = w_in_ab[0][:, :n_main].astype(BF16)
    w_gate = _pad_cols(w_in_ab[0][:, n_main:], LANES).astype(BF16)
    b_gate = _pad_cols(b_gate_ab[0][None], LANES)
    ml_w = ML_HEADS * ML_DIM
    P = _inproj(xc, norm_w[0], mod, w_main, jnp.zeros((1, n_main), F32), n_ctx, tm_in, n_main // 3, BF16,
                rope=_rope_tables(n_ctx, T) + (1, ml_w, 2 * ml_w))
    gates = _inproj(xc, norm_w[0], mod, w_gate, b_gate, n_ctx, tm_in, LANES, F32)
    oa = _na_attention(P, _na_bias_tables(rpb_a[0], T // GRID_W), q_norm_a[0], k_norm_a[0], n_ctx)
    hf = _mlstm_scan(P, gates, n_ctx, reverse=False)
    hb = _mlstm_scan(P, gates, n_ctx, reverse=True)
    xc = _merge_ab(oa, hf, hb, P, xc, gate, h_norm_b[0], w_out_ab[0].astype(BF16), n_ctx, tm_out)

    mod, gate = modulation(1)
    sm = jax.nn.softmax(lb_c.astype(F32), axis=0)
    lb = (jnp.cumsum(sm, axis=0) - sm[0])[1].reshape(1, -1)
    hw = HG_HEADS * HG_DIM
    P = _inproj(xc, norm_w[1], mod, w_in_c[0].astype(BF16), jnp.zeros((1, 5 * hw), F32), n_ctx, tm_in, hw, BF16)
    of = _gla_scan(P, lb, n_ctx, reverse=False)
    ob = _gla_scan(P, lb, n_ctx, reverse=True)
    return _merge_c(of, ob, P, xc, gate[:, 1:2], h_norm_c[0], w_out_c[0].astype(BF16), n_ctx, n_ctx)
```

```python
import functools

import numpy as np
import jax
import jax.numpy as jnp
from jax import lax
from jax.experimental import pallas as pl
from jax.experimental.pallas import tpu as pltpu

F32 = jnp.float32
BF16 = jnp.bfloat16
HIGHEST = lax.Precision.HIGHEST

GRID_W = 64
NA_HEADS = 16
NA_DIM = 64
NA_WIN_H = 8
NA_WIN_W = 16
ML_HEADS = 4
ML_DIM = 256
HG_HEADS = 16
HG_DIM = 128
ROPE_BASE = 10000.0
EPS = 1e-6
NEG = -1e30
LOG2E = 1.4426950408889634

LANES = 128
V7X_VMEM_LIMIT = 56 * 1024 * 1024

NA_QROWS = 4
NA_KROWS = NA_QROWS + NA_WIN_H
ML_CHUNK = 256
HG_CHUNK = 64
HG_BLOCK = 256
HG_GROUP = 8

NT = (((1,), (1,)), ((), ()))
TN = (((0,), (0,)), ((), ()))


def _params(*sem):
    return pltpu.CompilerParams(dimension_semantics=sem, vmem_limit_bytes=V7X_VMEM_LIMIT)


def _sigmoid(z):
    return 0.5 * jnp.tanh(0.5 * z) + 0.5


def _silu(z):
    h = 0.5 * z
    return h + h * jnp.tanh(h)


def _log_sigmoid(z):
    return jnp.minimum(z, 0.0) - jnp.log1p(jnp.exp(-jnp.abs(z)))


def _split_bf16(x, terms):
    out = []
    for _ in range(terms - 1):
        out.append(x.astype(BF16))
        x = x - out[-1].astype(F32)
    return out + [x.astype(BF16)]


def _cumsum_rows(tri_bf16, x, terms=3):
    return sum(jnp.dot(tri_bf16, t, preferred_element_type=F32) for t in _split_bf16(x, terms))


def _cumsum_cols(x, tri_bf16, terms=3):
    return sum(lax.dot_general(t, tri_bf16, NT, preferred_element_type=F32) for t in _split_bf16(x, terms))


def _ada_kernel(c_ref, w_ref, b_ref, o_ref):
    s = _silu(c_ref[...])
    o_ref[...] = jnp.dot(s.astype(BF16), w_ref[...].astype(BF16), preferred_element_type=F32) + b_ref[...]


def _ada(cvec, w, b):
    R, D = cvec.shape
    N = w.shape[1]
    tn = D
    return pl.pallas_call(
        _ada_kernel, grid=(N // tn,),
        in_specs=[pl.BlockSpec((R, D), lambda n: (0, 0)),
                  pl.BlockSpec((D, tn), lambda n: (0, n)),
                  pl.BlockSpec((1, tn), lambda n: (0, n))],
        out_specs=pl.BlockSpec((R, tn), lambda n: (0, n)),
        out_shape=jax.ShapeDtypeStruct((R, N), F32),
        compiler_params=_params("arbitrary"), name="ada",
    )(cvec, w, b.reshape(1, N))


def _inproj_kernel(x_ref, nw_ref, mod_ref, w_ref, bias_ref, *rest, n_ctx, rope_cols):
    if rope_cols is None:
        o_ref, hn_ref = rest
    else:
        cos_ref, sin_ref, o_ref, hn_ref = rest
    i = pl.program_id(1)
    n = pl.program_id(2)
    tm = x_ref.shape[1]

    @pl.when(pl.program_id(2) == 0)
    def _():
        x = x_ref[0]
        y = x * lax.rsqrt(jnp.mean(x * x, axis=-1, keepdims=True) + EPS) * nw_ref[...]
        row = i * tm + lax.broadcasted_iota(jnp.int32, (tm, 1), 0)
        is_ctx = row < n_ctx
        shift = jnp.where(is_ctx, mod_ref[0, 0:1, :], mod_ref[0, 2:3, :])
        scale = jnp.where(is_ctx, mod_ref[0, 1:2, :], mod_ref[0, 3:4, :])
        hn_ref[...] = (y * (1.0 + scale) + shift).astype(BF16)

    def project():
        return jnp.dot(hn_ref[...], w_ref[...], preferred_element_type=F32) + bias_ref[...]

    if rope_cols is None:
        o_ref[0] = project().astype(o_ref.dtype)
    else:
        tile, q_col, k_col = rope_cols
        ml_w = ML_HEADS * ML_DIM

        @pl.when(n == tile)
        def _():
            acc = project()
            cos, sin = cos_ref[...], sin_ref[...]
            for c in range(0, acc.shape[1], ML_DIM):
                cs = slice(c, c + ML_DIM)
                if q_col <= c < q_col + ml_w:
                    o_ref[0, :, cs] = _rope(acc[:, cs], cos, sin).astype(o_ref.dtype)
                elif k_col <= c < k_col + ml_w:
                    o_ref[0, :, cs] = (_rope(acc[:, cs], cos, sin) * (ML_DIM ** -0.5)).astype(o_ref.dtype)
                else:
                    o_ref[0, :, cs] = acc[:, cs].astype(o_ref.dtype)

        @pl.when(n != tile)
        def _():
            o_ref[0] = project().astype(o_ref.dtype)


def _inproj(xc, norm_w, mod, w, bias, n_ctx, tm, tn, out_dtype, rope=None):
    B, S, D = xc.shape
    N = w.shape[1]
    in_specs = [pl.BlockSpec((1, tm, D), lambda b, i, n: (b, i, 0)),
                pl.BlockSpec((1, D), lambda b, i, n: (0, 0)),
                pl.BlockSpec((1, 4, D), lambda b, i, n: (b, 0, 0)),
                pl.BlockSpec((D, tn), lambda b, i, n: (0, n)),
                pl.BlockSpec((1, tn), lambda b, i, n: (0, n))]
    args = [xc, norm_w.reshape(1, D), mod, w, bias]
    if rope is not None:
        in_specs += [pl.BlockSpec((tm, ML_DIM), lambda b, i, n: (i, 0))] * 2
        args += list(rope[:2])
    return pl.pallas_call(
        functools.partial(_inproj_kernel, n_ctx=n_ctx, rope_cols=None if rope is None else tuple(rope[2:])),
        grid=(B, S // tm, N // tn),
        in_specs=in_specs,
        out_specs=pl.BlockSpec((1, tm, tn), lambda b, i, n: (b, i, n)),
        out_shape=jax.ShapeDtypeStruct((B, S, N), out_dtype),
        scratch_shapes=[pltpu.VMEM((tm, D), BF16)],
        compiler_params=_params("parallel", "parallel", "arbitrary"), name="inproj",
    )(*args)


def _na_bias_tables(rpb, rows):
    H, nri, nci = rpb.shape
    qc, kc = np.arange(GRID_W)[:, None], np.arange(GRID_W)[None, :]
    c0 = np.clip(qc - NA_WIN_W // 2, 0, GRID_W - NA_WIN_W)
    col_ok = (kc >= c0) & (kc < c0 + NA_WIN_W)
    col_sel = ((kc - qc + NA_WIN_W - 1)[..., None] == np.arange(nci)) & col_ok[..., None]
    sel = np.einsum("de,qkj->qdkej", np.eye(2), col_sel).reshape(GRID_W, 2, GRID_W, 2 * nci).astype(np.float32)
    rp = jnp.pad(rpb.astype(F32), ((0, 0), (1, 1), (0, 0)))
    pair = jnp.stack([rp[:, :nri + 1], rp[:, 1:]], axis=2).reshape(H, nri + 1, 2 * nci)
    tiles = jnp.einsum("hic,qdkc->hiqdk", pair, sel, precision=HIGHEST).reshape(H, nri + 1, GRID_W, 2 * GRID_W)

    qr, ka = np.arange(NA_QROWS)[:, None], np.arange(NA_KROWS)[None, :]
    nblk = rows // NA_QROWS
    tables = []
    for blk in (0, 1, nblk - 1):
        kr0 = int(np.clip(NA_QROWS * blk - NA_WIN_H // 2, 0, rows - NA_KROWS))
        r = NA_QROWS * blk + qr
        r0 = np.clip(r - NA_WIN_H // 2, 0, rows - NA_WIN_H)
        krow = kr0 + ka
        row_ok = (krow >= r0) & (krow < r0 + NA_WIN_H)
        idx = np.clip((krow - r + NA_WIN_H - 1)[:, 0::2] + 1, 0, nri)
        valid = (row_ok.reshape(NA_QROWS, NA_KROWS // 2, 1, 2, 1) & col_ok[None, None, :, None, :])
        valid = valid.reshape(NA_QROWS, NA_KROWS // 2, GRID_W, 2 * GRID_W)
        tables.append(jnp.where(valid[None], tiles[:, idx] * LOG2E, NEG))
    return jnp.stack(tables)


def _na_kernel(q_ref, k_ref, v_ref, bias_ref, qn_ref, kn_ref, ones_ref, o_ref, kt_s, q0_s, q1_s, v0_s, v1_s,
               la0_s, la1_s, lb0_s, lb1_s, *, n_ctx, rows):
    la_s, lb_s = (la0_s, la1_s), (lb0_s, lb1_s)
    nq = NA_QROWS * GRID_W
    win = NA_KROWS * GRID_W
    nblk = rows // NA_QROWS
    lane = lax.broadcasted_iota(jnp.int32, (1, LANES), 1)
    in_h0 = lane < NA_DIM
    qh_s, vh_s = (q0_s, q1_s), (v0_s, v1_s)
    den_lane = (NA_DIM, 0)

    def rms(x, w):
        ss = jnp.dot((x * x).astype(BF16), ones_ref[...], preferred_element_type=F32)
        return x * lax.rsqrt(ss * (1.0 / NA_DIM) + EPS) * w

    kt_s[...] = rms(k_ref[0].astype(F32), kn_ref[...]).T.astype(BF16)
    q = rms(q_ref[0].astype(F32), qn_ref[...]) * (NA_DIM ** -0.5 * LOG2E)
    q0_s[...] = jnp.where(in_h0, q, 0.0).astype(BF16)
    q1_s[...] = jnp.where(in_h0, 0.0, q).astype(BF16)
    v = v_ref[0].astype(F32)
    v0_s[...] = jnp.where(in_h0, v, jnp.where(lane == den_lane[0], 1.0, 0.0)).astype(BF16)
    v1_s[...] = jnp.where(in_h0, jnp.where(lane == den_lane[1], 1.0, 0.0), v).astype(BF16)

    def softmax_pv(h, s_parts, bias, v_starts):
        if bias is not None:
            s_parts = [s_parts[0] + bias] + s_parts[1:]
        m = functools.reduce(jnp.maximum, [jnp.max(s, axis=1, keepdims=True) for s in s_parts])
        return sum(jnp.dot(jnp.exp2(s - m).astype(BF16), vh_s[h][pl.ds(st, s.shape[1]), :],
                           preferred_element_type=F32) for s, st in zip(s_parts, v_starts))

    def store(q_start, o):
        den = [o[h][:, den_lane[h]:den_lane[h] + 1] for h in range(2)]
        o_ref[0, pl.ds(q_start, nq), :] = jnp.where(in_h0, o[0] / den[0], o[1] / den[1]).astype(o_ref.dtype)

    kc = kt_s[:, 0:n_ctx]
    s_ctx = [jnp.dot(qh_s[h][0:n_ctx, :], kc, preferred_element_type=F32) for h in range(2)]
    store(0, [softmax_pv(h, [s_ctx[h]], None, [0]) for h in range(2)])

    def block_pos(blk):
        q_start = pl.multiple_of(n_ctx + nq * blk, nq)
        kr0 = jnp.clip(NA_QROWS * blk - NA_WIN_H // 2, 0, rows - NA_KROWS)
        return q_start, pl.multiple_of(n_ctx + GRID_W * kr0, nq)

    def logits_to(bufs, blk):
        q_start, k_start = block_pos(blk)
        for h in range(2):
            qh = qh_s[h][pl.ds(q_start, nq), :]
            bufs[h][:, 0:win] = jnp.dot(qh, kt_s[:, pl.ds(k_start, win)], preferred_element_type=F32)
            bufs[h][:, win:win + n_ctx] = jnp.dot(qh, kc, preferred_element_type=F32)

    def attend_from(bufs, blk):
        q_start, k_start = block_pos(blk)
        case = jnp.where(blk == 0, 0, jnp.where(blk == nblk - 1, 2, 1))
        o = []
        for h in range(2):
            tiles = bias_ref[case, h]
            bias = jnp.concatenate([jnp.concatenate([tiles[r, a] for a in range(NA_KROWS // 2)], axis=1)
                                    for r in range(NA_QROWS)], axis=0)
            o.append(softmax_pv(h, [bufs[h][:, 0:win], bufs[h][:, win:win + n_ctx]], bias, [k_start, 0]))
        store(q_start, o)

    logits_to(la_s, 0)

    def two_blocks(it, carry):
        b0 = 2 * it
        logits_to(lb_s, b0 + 1)
        attend_from(la_s, b0)
        logits_to(la_s, jnp.minimum(b0 + 2, nblk - 1))
        attend_from(lb_s, b0 + 1)
        return carry

    lax.fori_loop(0, nblk // 2, two_blocks, 0)


def _na_attention(P, bias_tab, q_norm, k_norm, n_ctx):
    B, S, _ = P.shape
    width = NA_HEADS * NA_DIM
    npair = width // LANES
    nq = NA_QROWS * GRID_W
    rows = (S - n_ctx) // GRID_W
    assert n_ctx == nq and rows % (2 * NA_QROWS) == 0
    win = NA_KROWS * GRID_W
    hp = LANES // NA_DIM
    ones = jnp.asarray(np.kron(np.eye(hp), np.ones((NA_DIM, NA_DIM))), BF16)
    qn = jnp.tile(q_norm, hp).reshape(1, LANES)
    kn = jnp.tile(k_norm, hp).reshape(1, LANES)
    seq = lambda c: pl.BlockSpec((1, S, LANES), lambda b, p: (b, 0, c * npair + p))
    vec = pl.BlockSpec((1, LANES), lambda b, p: (0, 0))
    return pl.pallas_call(
        functools.partial(_na_kernel, n_ctx=n_ctx, rows=rows),
        grid=(B, npair),
        in_specs=[seq(0), seq(1), seq(2),
                  pl.BlockSpec((bias_tab.shape[0], hp) + bias_tab.shape[2:], lambda b, p: (0, p, 0, 0, 0, 0)),
                  vec, vec, pl.BlockSpec((LANES, LANES), lambda b, p: (0, 0))],
        out_specs=seq(0),
        out_shape=jax.ShapeDtypeStruct((B, S, width), BF16),
        scratch_shapes=([pltpu.VMEM((LANES, S), BF16)] + [pltpu.VMEM((S, LANES), BF16)] * 4
                        + [pltpu.VMEM((nq, win + n_ctx), F32)] * 4),
        compiler_params=_params("parallel", "parallel"), name="na_attention",
    )(P, P, P, bias_tab, qn, kn, ones)


def _rope_tables(n_ctx, T):
    half = ML_DIM // 4
    t = jnp.arange(T)
    freqs = ROPE_BASE ** (-jnp.arange(half, dtype=F32) / half)
    ang_r = (t // GRID_W).astype(F32)[:, None] * freqs[None, :]
    ang_c = (t % GRID_W).astype(F32)[:, None] * freqs[None, :]
    cos = jnp.concatenate([jnp.cos(ang_r)] * 2 + [jnp.cos(ang_c)] * 2, axis=-1)
    sin = jnp.concatenate([-jnp.sin(ang_r), jnp.sin(ang_r), -jnp.sin(ang_c), jnp.sin(ang_c)], axis=-1)
    cos = jnp.concatenate([jnp.ones((n_ctx, ML_DIM), F32), cos], axis=0)
    sin = jnp.concatenate([jnp.zeros((n_ctx, ML_DIM), F32), sin], axis=0)
    return cos, sin


def _rope(x, cos, sin):
    xr = jnp.concatenate([pltpu.roll(x[:, a * LANES:(a + 1) * LANES], LANES // 2, axis=1)
                          for a in range(x.shape[1] // LANES)], axis=1)
    return x * cos + xr * sin


def _mlstm_kernel(q_ref, k_ref, v_ref, g_ref, o_ref, c_ref, n_ref, m_ref, *, reverse):
    L = q_ref.shape[1]

    @pl.when(pl.program_id(1) == 0)
    def _():
        c_ref[...] = jnp.zeros_like(c_ref)
        n_ref[...] = jnp.zeros_like(n_ref)
        m_ref[...] = jnp.zeros_like(m_ref)

    t_idx = lax.broadcasted_iota(jnp.int32, (L, L), 0)
    s_idx = lax.broadcasted_iota(jnp.int32, (L, L), 1)
    mask = (s_idx >= t_idx) if reverse else (s_idx <= t_idx)
    tri = mask.astype(BF16)
    last = 0 if reverse else L - 1

    g = g_ref[0]
    lsg = _log_sigmoid(g)
    g_t = g.T
    b_cols = _cumsum_rows(tri, lsg)
    b_rows = _cumsum_cols(lsg.T, tri)
    ones = jnp.ones((L, LANES), BF16)

    heads = range(ML_HEADS)
    hs = [slice(h * ML_DIM, (h + 1) * ML_DIM) for h in heads]
    qb, kb, vb, kwb, decay, m_t, carry_w, m_new, dec = ([None] * ML_HEADS for _ in range(9))
    for h in heads:
        ii = (2 if reverse else 0) * ML_HEADS + h
        fi = (3 if reverse else 1) * ML_HEADS + h
        ig_col, ig_row = g[:, ii:ii + 1], g_t[ii:ii + 1, :]
        b_col, b_row = b_cols[:, fi:fi + 1], b_rows[fi:fi + 1, :]
        b_last = b_col[last:last + 1, :]
        qb[h], kb[h], vb[h] = q_ref[0, :, hs[h]], k_ref[0, :, hs[h]], v_ref[0, :, hs[h]]
        m_prev = m_ref[h, 0:1, 0:1]
        dmat = jnp.where(mask, b_col - b_row + ig_row, NEG)
        m_inter = b_col + m_prev
        m_t[h] = jnp.maximum(jnp.max(dmat, axis=1, keepdims=True), m_inter)
        decay[h] = jnp.exp(dmat - m_t[h])
        carry_w[h] = jnp.broadcast_to(jnp.exp(m_inter - m_t[h]), (L, LANES))
        g_col = b_last - b_col + ig_col
        m_new[h] = jnp.maximum(b_last + m_prev, jnp.max(g_col, axis=0, keepdims=True))
        kwb[h] = (kb[h].astype(F32) * jnp.exp(g_col - m_new[h])).astype(BF16)
        dec[h] = jnp.exp(b_last + m_prev - m_new[h])

    qk = [lax.dot_general(qb[h], kb[h], NT, preferred_element_type=F32) for h in heads]
    qc = [jnp.dot(qb[h], c_ref[h].astype(BF16), preferred_element_type=F32) for h in heads]
    qn = [jnp.dot(qb[h], n_ref[h].astype(BF16), preferred_element_type=F32) for h in heads]
    upd = [lax.dot_general(kwb[h], vb[h], TN, preferred_element_type=F32) for h in heads]
    kw_sum = [lax.dot_general(kwb[h], ones, TN, preferred_element_type=F32) for h in heads]
    sb = [(qk[h] * decay[h]).astype(BF16) for h in heads]
    sv = [jnp.dot(sb[h], vb[h], preferred_element_type=F32) for h in heads]
    s_sum = [jnp.dot(sb[h], ones, preferred_element_type=F32) for h in heads]
    for h in heads:
        den = s_sum[h] + carry_w[h] * qn[h]
        inv = 1.0 / jnp.maximum(jnp.abs(den), jnp.exp(-m_t[h]))
        num = sv[h] + jnp.concatenate([carry_w[h]] * (ML_DIM // LANES), axis=1) * qc[h]
        o_ref[0, :, hs[h]] = (num * jnp.concatenate([inv] * (ML_DIM // LANES), axis=1)).astype(o_ref.dtype)
        c_ref[h] = dec[h] * c_ref[h] + upd[h]
        n_ref[h] = dec[h] * n_ref[h] + kw_sum[h]
        m_ref[h] = jnp.broadcast_to(m_new[h], m_ref.shape[1:])


def _mlstm_scan(P, gates, n_ctx, reverse):
    B, S, _ = P.shape
    L = ML_CHUNK
    width = ML_HEADS * ML_DIM
    assert n_ctx == L and S % L == 0
    nch = S // L

    def cidx(j):
        return jnp.where(j == 0, 0, nch - j) if reverse else j

    return pl.pallas_call(
        functools.partial(_mlstm_kernel, reverse=reverse),
        grid=(B, nch),
        in_specs=[pl.BlockSpec((1, L, width), lambda b, j: (b, cidx(j), 4)),
                  pl.BlockSpec((1, L, width), lambda b, j: (b, cidx(j), 5)),
                  pl.BlockSpec((1, L, width), lambda b, j: (b, cidx(j), 6)),
                  pl.BlockSpec((1, L, LANES), lambda b, j: (b, cidx(j), 0))],
        out_specs=pl.BlockSpec((1, L, width), lambda b, j: (b, cidx(j), 0)),
        out_shape=jax.ShapeDtypeStruct((B, S, width), BF16),
        scratch_shapes=[pltpu.VMEM((ML_HEADS, ML_DIM, ML_DIM), F32),
                        pltpu.VMEM((ML_HEADS, ML_DIM, LANES), F32),
                        pltpu.VMEM((ML_HEADS, 8, LANES), F32)],
        compiler_params=_params("parallel", "arbitrary"), name="mlstm_bwd" if reverse else "mlstm_fwd",
    )(P, P, P, gates)


def _merge_ab_kernel(oa_ref, za_ref, hf_ref, hb_ref, ob_ref, zb_ref, x_ref, gate_ref, hn_ref, w_ref, o_ref, *, n_ctx):
    tm = x_ref.shape[1]
    width = oa_ref.shape[2]
    ya = oa_ref[0].astype(F32) * _silu(za_ref[0].astype(F32))
    hb = (hf_ref[0].astype(F32) + hb_ref[0].astype(F32)) * _sigmoid(ob_ref[0].astype(F32))
    parts = []
    for h in range(ML_HEADS):
        seg = hb[:, h * ML_DIM:(h + 1) * ML_DIM]
        parts.append(seg * lax.rsqrt(jnp.mean(seg * seg, axis=1, keepdims=True) + EPS))
    yb = jnp.concatenate(parts, axis=1) * hn_ref[...] * _silu(zb_ref[0].astype(F32))
    y = (jnp.dot(ya.astype(BF16), w_ref[0:width, :], preferred_element_type=F32)
         + jnp.dot(yb.astype(BF16), w_ref[width:2 * width, :], preferred_element_type=F32))
    row = pl.program_id(1) * tm + lax.broadcasted_iota(jnp.int32, (tm, 1), 0)
    gate = jnp.where(row < n_ctx, gate_ref[0, 0:1, :], gate_ref[0, 1:2, :])
    o_ref[0] = x_ref[0] + gate * y


def _merge_ab(oa, hf, hb, P, xc, gate, h_norm, w_out, n_ctx, tm):
    B, S, D = xc.shape
    width = oa.shape[2]
    tok = lambda c: pl.BlockSpec((1, tm, width), lambda b, i: (b, i, c))
    return pl.pallas_call(
        functools.partial(_merge_ab_kernel, n_ctx=n_ctx),
        grid=(B, S // tm),
        in_specs=[tok(0), tok(3), tok(0), tok(0), tok(7), tok(8),
                  pl.BlockSpec((1, tm, D), lambda b, i: (b, i, 0)),
                  pl.BlockSpec((1, 2, D), lambda b, i: (b, 0, 0)),
                  pl.BlockSpec((1, width), lambda b, i: (0, 0)),
                  pl.BlockSpec((2 * width, D), lambda b, i: (0, 0))],
        out_specs=pl.BlockSpec((1, tm, D), lambda b, i: (b, i, 0)),
        out_shape=jax.ShapeDtypeStruct((B, S, D), F32),
        compiler_params=_params("parallel", "parallel"), name="merge_ab",
    )(oa, P, hf, hb, P, P, xc, gate, h_norm.reshape(1, width), w_out)


def _gla_kernel(q_ref, f_ref, v_ref, lb_ref, o_ref, st_ref, *, reverse):
    C = HG_CHUNK
    T = q_ref.shape[1]
    nsub = T // C
    gw = q_ref.shape[2]

    @pl.when(pl.program_id(2) == 0)
    def _():
        st_ref[...] = jnp.zeros_like(st_ref)

    def causal(n, same_chunk):
        t_idx = lax.broadcasted_iota(jnp.int32, (n, n), 0)
        s_idx = lax.broadcasted_iota(jnp.int32, (n, n), 1)
        m = (s_idx >= t_idx) if reverse else (s_idx <= t_idx)
        return m & (t_idx // C == s_idx // C) if same_chunk else m

    mask = causal(C, False)
    tri = causal(T, True).astype(BF16)
    last = 0 if reverse else C - 1
    mid = C // 2 if reverse else C // 2 - 1
    order = list(range(nsub - 1, -1, -1) if reverse else range(nsub))
    rows = [slice(c * C, (c + 1) * C) for c in range(nsub)]
    hw = gw // 2
    nh = hw // HG_DIM

    def decay_stage(half_idx):
        cols = slice(half_idx * hw, (half_idx + 1) * hw)
        lb = lb_ref[:, cols]
        c1 = 0.5 * (1.0 - lb)
        f = (lb + c1) + c1 * jnp.tanh(0.5 * f_ref[0, :, cols].astype(F32))
        b = _cumsum_rows(tri, jnp.log(f), terms=2)
        b_mid = [b[c * C + mid:c * C + mid + 1, :] for c in range(nsub)]
        b_last = [b[c * C + last:c * C + last + 1, :] for c in range(nsub)]
        b_mid_rows = jnp.concatenate([jnp.broadcast_to(r, (C, hw)) for r in b_mid], axis=0)
        qm = _silu(q_ref[0, :, cols].astype(F32)) * jnp.exp(b - b_mid_rows)
        km = (1.0 - f) * jnp.exp(b_mid_rows - b)
        return dict(cols=cols, qm=qm, km=km, vb=v_ref[0, :, cols],
                    e_mid=[jnp.exp(r) for r in b_mid],
                    e_last_mid=[jnp.exp(l - r) for l, r in zip(b_last, b_mid)],
                    dec=[jnp.exp(l) for l in b_last])

    def score_stage(d):
        qmb, kmb, vb = d["qm"].astype(BF16), d["km"].astype(BF16), d["vb"]
        a, upd = {}, {}
        for g in range(nh):
            sl = slice(g * HG_DIM, (g + 1) * HG_DIM)
            for c, rs in enumerate(rows):
                s = lax.dot_general(qmb[rs, sl], kmb[rs, sl], NT, preferred_element_type=F32)
                a[g, c] = jnp.where(mask, s, 0.0).astype(BF16)
                kh = (d["km"][rs, sl] * d["e_last_mid"][c][:, sl]).astype(BF16)
                upd[g, c] = lax.dot_general(vb[rs, sl], kh, TN, preferred_element_type=F32)
        d["upd"] = upd
        d["intra"] = {(g, c): jnp.dot(a[g, c], vb[rows[c], g * HG_DIM:(g + 1) * HG_DIM],
                                      preferred_element_type=F32) for (g, c) in a}

    def state_stage(half_idx, d):
        for g in range(nh):
            sl = slice(g * HG_DIM, (g + 1) * HG_DIM)
            out_cols = slice(half_idx * hw + g * HG_DIM, half_idx * hw + (g + 1) * HG_DIM)
            st = st_ref[half_idx * nh + g]
            for c in order:
                rs = rows[c]
                qi = (d["qm"][rs, sl] * d["e_mid"][c][:, sl]).astype(BF16)
                o_ref[0, rs, out_cols] = (d["intra"][g, c] + lax.dot_general(
                    qi, st.astype(BF16), NT, preferred_element_type=F32)).astype(o_ref.dtype)
                st = st * d["dec"][c][:, sl] + d["upd"][g, c]
            st_ref[half_idx * nh + g] = st

    d0 = decay_stage(0)
    score_stage(d0)
    d1 = decay_stage(1)
    state_stage(0, d0)
    score_stage(d1)
    state_stage(1, d1)


def _gla_scan(P, lb, n_ctx, reverse):
    B, S, _ = P.shape
    C = HG_BLOCK
    width = HG_HEADS * HG_DIM
    gw = HG_GROUP * HG_DIM
    ngrp = width // gw
    assert n_ctx % C == 0 and S % C == 0 and C % HG_CHUNK == 0
    nch, nctx_ch = S // C, n_ctx // C
    fcol = (2 if reverse else 1) * ngrp

    def cidx(j):
        if not reverse:
            return j
        return jnp.where(j < nctx_ch, nctx_ch - 1 - j, nch + nctx_ch - 1 - j)

    def oidx(j):
        return jnp.maximum(cidx(j) - nctx_ch, 0) if not reverse else jnp.where(j < nctx_ch, nch - nctx_ch - 1,
                                                                                cidx(j) - nctx_ch)

    return pl.pallas_call(
        functools.partial(_gla_kernel, reverse=reverse),
        grid=(B, ngrp, nch),
        in_specs=[pl.BlockSpec((1, C, gw), lambda b, g, j: (b, cidx(j), g)),
                  pl.BlockSpec((1, C, gw), lambda b, g, j: (b, cidx(j), fcol + g)),
                  pl.BlockSpec((1, C, gw), lambda b, g, j: (b, cidx(j), 3 * ngrp + g)),
                  pl.BlockSpec((1, gw), lambda b, g, j: (0, g))],
        out_specs=pl.BlockSpec((1, C, gw), lambda b, g, j: (b, oidx(j), g)),
        out_shape=jax.ShapeDtypeStruct((B, S - n_ctx, width), BF16),
        scratch_shapes=[pltpu.VMEM((HG_GROUP, HG_DIM, HG_DIM), F32)],
        compiler_params=_params("parallel", "parallel", "arbitrary"), name="gla_bwd" if reverse else "gla_fwd",
    )(P, P, P, lb)


def _merge_c_kernel(of_ref, ob_ref, z_ref, x_ref, gate_ref, hn_ref, w_ref, o_ref):
    o = of_ref[0].astype(F32) + ob_ref[0].astype(F32)
    parts = []
    for h in range(HG_HEADS):
        seg = o[:, h * HG_DIM:(h + 1) * HG_DIM]
        parts.append(seg * lax.rsqrt(jnp.mean(seg * seg, axis=1, keepdims=True) + EPS))
    y = jnp.concatenate(parts, axis=1) * hn_ref[...] * _silu(z_ref[0].astype(F32))
    o_ref[0] = x_ref[0] + gate_ref[0] * jnp.dot(y.astype(BF16), w_ref[...], preferred_element_type=F32)


def _merge_c(of, ob, P, xc, gate, h_norm, w_out, n_ctx, tm):
    B, T, width = of.shape
    D = xc.shape[2]
    off = n_ctx // tm
    return pl.pallas_call(
        _merge_c_kernel,
        grid=(B, T // tm),
        in_specs=[pl.BlockSpec((1, tm, width), lambda b, i: (b, i, 0)),
                  pl.BlockSpec((1, tm, width), lambda b, i: (b, i, 0)),
                  pl.BlockSpec((1, tm, width), lambda b, i: (b, i + off, 4)),
                  pl.BlockSpec((1, tm, D), lambda b, i: (b, i + off, 0)),
                  pl.BlockSpec((1, 1, D), lambda b, i: (b, 0, 0)),
                  pl.BlockSpec((1, width), lambda b, i: (0, 0)),
                  pl.BlockSpec((width, D), lambda b, i: (0, 0))],
        out_specs=pl.BlockSpec((1, tm, D), lambda b, i: (b, i, 0)),
        out_shape=jax.ShapeDtypeStruct((B, T, D), F32),
        compiler_params=_params("parallel", "parallel"), name="merge_c",
    )(of, ob, P, xc, gate, h_norm.reshape(1, width), w_out)


def _pad_cols(w, n):
    return jnp.pad(w, ((0, 0), (0, n - w.shape[1])))


def kernel(x, c, ctx, c_ctx, norm_w, w_ada, b_ada, w_in_ab, b_gate_ab, q_norm_a, k_norm_a, rpb_a, h_norm_b, w_out_ab,
           w_in_c, lb_c, h_norm_c, w_out_c):
    B, T, D = x.shape
    n_ctx = ctx.shape[1]
    S = n_ctx + T
    tm_in = S // 4
    tm_out = S // 16
    xc = jnp.concatenate([ctx, x], axis=1)
    cvec = jnp.concatenate([c, c_ctx[None], jnp.zeros((16 - B - 1, D), F32)], axis=0)

    def modulation(l):
        m = _ada(cvec, w_ada[l], b_ada[l])
        shift, scale, gate = m[:, :D], m[:, D:2 * D], m[:, 2 * D:]
        ctx_row = lambda a: jnp.broadcast_to(a[B][None], (B, D))
        mod = jnp.stack([ctx_row(shift), ctx_row(scale), shift[:B], scale[:B]], axis=1)
        return mod, jnp.stack([ctx_row(gate), gate[:B]], axis=1)

    mod, gate = modulation(0)
    n_main = 9 * NA_HEADS * NA_DIM
    w_main = w_in_ab[0][:, :n_main].astype(BF16)
    w_gate = _pad_cols(w_in_ab[0][:, n_main:], LANES).astype(BF16)
    b_gate = _pad_cols(b_gate_ab[0][None], LANES)
    ml_w = ML_HEADS * ML_DIM
    P = _inproj(xc, norm_w[0], mod, w_main, jnp.zeros((1, n_main), F32), n_ctx, tm_in, n_main // 3, BF16,
                rope=_rope_tables(n_ctx, T) + (1, ml_w, 2 * ml_w))
    gates = _inproj(xc, norm_w[0], mod, w_gate, b_gate, n_ctx, tm_in, LANES, F32)
    oa = _na_attention(P, _na_bias_tables(rpb_a[0], T // GRID_W), q_norm_a[0], k_norm_a[0], n_ctx)
    hf = _mlstm_scan(P, gates, n_ctx, reverse=False)
    hb = _mlstm_scan(P, gates, n_ctx, reverse=True)
    xc = _merge_ab(oa, hf, hb, P, xc, gate, h_norm_b[0], w_out_ab[0].astype(BF16), n_ctx, tm_out)

    mod, gate = modulation(1)
    sm = jax.nn.softmax(lb_c.astype(F32), axis=0)
    lb = (jnp.cumsum(sm, axis=0) - sm[0])[1].reshape(1, -1)
    hw = HG_HEADS * HG_DIM
    P = _inproj(xc, norm_w[1], mod, w_in_c[0].astype(BF16), jnp.zeros((1, 5 * hw), F32), n_ctx, tm_in, hw, BF16)
    of = _gla_scan(P, lb, n_ctx, reverse=False)
    ob = _gla_scan(P, lb, n_ctx, reverse=True)
    return _merge_c(of, ob, P, xc, gate[:, 1:2], h_norm_c[0], w_out_c[0].astype(BF16), n_ctx, n_ctx)
```

```python
import functools

import numpy as np
import jax
import jax.numpy as jnp
from jax import lax
from jax.experimental import pallas as pl
from jax.experimental.pallas import tpu as pltpu

F32 = jnp.float32
BF16 = jnp.bfloat16
HIGHEST = lax.Precision.HIGHEST

GRID_W = 64
NA_HEADS = 16
NA_DIM = 64
NA_WIN_H = 8
NA_WIN_W = 16
ML_HEADS = 4
ML_DIM = 256
HG_HEADS = 16
HG_DIM = 128
ROPE_BASE = 10000.0
EPS = 1e-6
NEG = -1e30
LOG2E = 1.4426950408889634

LANES = 128
V7X_VMEM_LIMIT = 56 * 1024 * 1024

NA_QROWS = 4
NA_KROWS = NA_QROWS + NA_WIN_H
ML_CHUNK = 256
HG_CHUNK = 64
HG_BLOCK = 256
HG_GROUP = 8

NT = (((1,), (1,)), ((), ()))
TN = (((0,), (0,)), ((), ()))


def _params(*sem):
    return pltpu.CompilerParams(dimension_semantics=sem, vmem_limit_bytes=V7X_VMEM_LIMIT)


def _sigmoid(z):
    return 0.5 * jnp.tanh(0.5 * z) + 0.5


def _silu(z):
    h = 0.5 * z
    return h + h * jnp.tanh(h)


def _log_sigmoid(z):
    return jnp.minimum(z, 0.0) - jnp.log1p(jnp.exp(-jnp.abs(z)))


def _split_bf16(x, terms):
    out = []
    for _ in range(terms - 1):
        out.append(x.astype(BF16))
        x = x - out[-1].astype(F32)
    return out + [x.astype(BF16)]


def _cumsum_rows(tri_bf16, x, terms=3):
    return sum(jnp.dot(tri_bf16, t, preferred_element_type=F32) for t in _split_bf16(x, terms))


def _cumsum_cols(x, tri_bf16, terms=3):
    return sum(lax.dot_general(t, tri_bf16, NT, preferred_element_type=F32) for t in _split_bf16(x, terms))


def _ada_kernel(c_ref, w_ref, b_ref, o_ref):
    s = _silu(c_ref[...])
    o_ref[...] = jnp.dot(s.astype(BF16), w_ref[...].astype(BF16), preferred_element_type=F32) + b_ref[...]


def _ada(cvec, w, b):
    R, D = cvec.shape
    N = w.shape[1]
    tn = D
    return pl.pallas_call(
        _ada_kernel, grid=(N // tn,),
        in_specs=[pl.BlockSpec((R, D), lambda n: (0, 0)),
                  pl.BlockSpec((D, tn), lambda n: (0, n)),
                  pl.BlockSpec((1, tn), lambda n: (0, n))],
        out_specs=pl.BlockSpec((R, tn), lambda n: (0, n)),
        out_shape=jax.ShapeDtypeStruct((R, N), F32),
        compiler_params=_params("arbitrary"), name="ada",
    )(cvec, w, b.reshape(1, N))


def _inproj_kernel(x_ref, nw_ref, mod_ref, w_ref, bias_ref, *rest, n_ctx, rope_cols):
    if rope_cols is None:
        o_ref, hn_ref = rest
    else:
        cos_ref, sin_ref, o_ref, hn_ref = rest
    i = pl.program_id(1)
    n = pl.program_id(2)
    tm = x_ref.shape[1]

    @pl.when(pl.program_id(2) == 0)
    def _():
        x = x_ref[0]
        y = x * lax.rsqrt(jnp.mean(x * x, axis=-1, keepdims=True) + EPS) * nw_ref[...]
        row = i * tm + lax.broadcasted_iota(jnp.int32, (tm, 1), 0)
        is_ctx = row < n_ctx
        shift = jnp.where(is_ctx, mod_ref[0, 0:1, :], mod_ref[0, 2:3, :])
        scale = jnp.where(is_ctx, mod_ref[0, 1:2, :], mod_ref[0, 3:4, :])
        hn_ref[...] = (y * (1.0 + scale) + shift).astype(BF16)

    def project():
        return jnp.dot(hn_ref[...], w_ref[...], preferred_element_type=F32) + bias_ref[...]

    if rope_cols is None:
        o_ref[0] = project().astype(o_ref.dtype)
    else:
        tile, q_col, k_col = rope_cols
        ml_w = ML_HEADS * ML_DIM

        @pl.when(n == tile)
        def _():
            acc = project()
            cos, sin = cos_ref[...], sin_ref[...]
            for c in range(0, acc.shape[1], ML_DIM):
                cs = slice(c, c + ML_DIM)
                if q_col <= c < q_col + ml_w:
                    o_ref[0, :, cs] = _rope(acc[:, cs], cos, sin).astype(o_ref.dtype)
                elif k_col <= c < k_col + ml_w:
                    o_ref[0, :, cs] = (_rope(acc[:, cs], cos, sin) * (ML_DIM ** -0.5)).astype(o_ref.dtype)
                else:
                    o_ref[0, :, cs] = acc[:, cs].astype(o_ref.dtype)

        @pl.when(n != tile)
        def _():
            o_ref[0] = project().astype(o_ref.dtype)


def _inproj(xc, norm_w, mod, w, bias, n_ctx, tm, tn, out_dtype, rope=None):
    B, S, D = xc.shape
    N = w.shape[1]
    in_specs = [pl.BlockSpec((1, tm, D), lambda b, i, n: (b, i, 0)),
                pl.BlockSpec((1, D), lambda b, i, n: (0, 0)),
                pl.BlockSpec((1, 4, D), lambda b, i, n: (b, 0, 0)),
                pl.BlockSpec((D, tn), lambda b, i, n: (0, n)),
                pl.BlockSpec((1, tn), lambda b, i, n: (0, n))]
    args = [xc, norm_w.reshape(1, D), mod, w, bias]
    if rope is not None:
        in_specs += [pl.BlockSpec((tm, ML_DIM), lambda b, i, n: (i, 0))] * 2
        args += list(rope[:2])
    return pl.pallas_call(
        functools.partial(_inproj_kernel, n_ctx=n_ctx, rope_cols=None if rope is None else tuple(rope[2:])),
        grid=(B, S // tm, N // tn),
        in_specs=in_specs,
        out_specs=pl.BlockSpec((1, tm, tn), lambda b, i, n: (b, i, n)),
        out_shape=jax.ShapeDtypeStruct((B, S, N), out_dtype),
        scratch_shapes=[pltpu.VMEM((tm, D), BF16)],
        compiler_params=_params("parallel", "parallel", "arbitrary"), name="inproj",
    )(*args)


def _na_bias_tables(rpb, rows):
    H, nri, nci = rpb.shape
    qc, kc = np.arange(GRID_W)[:, None], np.arange(GRID_W)[None, :]
    c0 = np.clip(qc - NA_WIN_W // 2, 0, GRID_W - NA_WIN_W)
    col_ok = (kc >= c0) & (kc < c0 + NA_WIN_W)
    col_sel = ((kc - qc + NA_WIN_W - 1)[..., None] == np.arange(nci)) & col_ok[..., None]
    sel = np.einsum("de,qkj->qdkej", np.eye(2), col_sel).reshape(GRID_W, 2, GRID_W, 2 * nci).astype(np.float32)
    rp = jnp.pad(rpb.astype(F32), ((0, 0), (1, 1), (0, 0)))
    pair = jnp.stack([rp[:, :nri + 1], rp[:, 1:]], axis=2).reshape(H, nri + 1, 2 * nci)
    tiles = jnp.einsum("hic,qdkc->hiqdk", pair, sel, precision=HIGHEST).reshape(H, nri + 1, GRID_W, 2 * GRID_W)

    qr, ka = np.arange(NA_QROWS)[:, None], np.arange(NA_KROWS)[None, :]
    nblk = rows // NA_QROWS
    tables = []
    for blk in (0, 1, nblk - 1):
        kr0 = int(np.clip(NA_QROWS * blk - NA_WIN_H // 2, 0, rows - NA_KROWS))
        r = NA_QROWS * blk + qr
        r0 = np.clip(r - NA_WIN_H // 2, 0, rows - NA_WIN_H)
        krow = kr0 + ka
        row_ok = (krow >= r0) & (krow < r0 + NA_WIN_H)
        idx = np.clip((krow - r + NA_WIN_H - 1)[:, 0::2] + 1, 0, nri)
        valid = (row_ok.reshape(NA_QROWS, NA_KROWS // 2, 1, 2, 1) & col_ok[None, None, :, None, :])
        valid = valid.reshape(NA_QROWS, NA_KROWS // 2, GRID_W, 2 * GRID_W)
        tables.append(jnp.where(valid[None], tiles[:, idx] * LOG2E, NEG))
    return jnp.stack(tables)


def _na_kernel(q_ref, k_ref, v_ref, bias_ref, qn_ref, kn_ref, ones_ref, o_ref, kt_s, q0_s, q1_s, v0_s, v1_s,
               la0_s, la1_s, lb0_s, lb1_s, *, n_ctx, rows):
    la_s, lb_s = (la0_s, la1_s), (lb0_s, lb1_s)
    nq = NA_QROWS * GRID_W
    win = NA_KROWS * GRID_W
    nblk = rows // NA_QROWS
    lane = lax.broadcasted_iota(jnp.int32, (1, LANES), 1)
    in_h0 = lane < NA_DIM
    qh_s, vh_s = (q0_s, q1_s), (v0_s, v1_s)
    den_lane = (NA_DIM, 0)

    def rms(x, w):
        ss = jnp.dot((x * x).astype(BF16), ones_ref[...], preferred_element_type=F32)
        return x * lax.rsqrt(ss * (1.0 / NA_DIM) + EPS) * w

    kt_s[...] = rms(k_ref[0].astype(F32), kn_ref[...]).T.astype(BF16)
    q = rms(q_ref[0].astype(F32), qn_ref[...]) * (NA_DIM ** -0.5 * LOG2E)
    q0_s[...] = jnp.where(in_h0, q, 0.0).astype(BF16)
    q1_s[...] = jnp.where(in_h0, 0.0, q).astype(BF16)
    v = v_ref[0].astype(F32)
    v0_s[...] = jnp.where(in_h0, v, jnp.where(lane == den_lane[0], 1.0, 0.0)).astype(BF16)
    v1_s[...] = jnp.where(in_h0, jnp.where(lane == den_lane[1], 1.0, 0.0), v).astype(BF16)

    def softmax_pv(h, s_parts, bias, v_starts):
        if bias is not None:
            s_parts = [s_parts[0] + bias] + s_parts[1:]
        m = functools.reduce(jnp.maximum, [jnp.max(s, axis=1, keepdims=True) for s in s_parts])
        return sum(jnp.dot(jnp.exp2(s - m).astype(BF16), vh_s[h][pl.ds(st, s.shape[1]), :],
                           preferred_element_type=F32) for s, st in zip(s_parts, v_starts))

    def store(q_start, o):
        den = [o[h][:, den_lane[h]:den_lane[h] + 1] for h in range(2)]
        o_ref[0, pl.ds(q_start, nq), :] = jnp.where(in_h0, o[0] / den[0], o[1] / den[1]).astype(o_ref.dtype)

    kc = kt_s[:, 0:n_ctx]
    s_ctx = [jnp.dot(qh_s[h][0:n_ctx, :], kc, preferred_element_type=F32) for h in range(2)]
    store(0, [softmax_pv(h, [s_ctx[h]], None, [0]) for h in range(2)])

    def block_pos(blk):
        q_start = pl.multiple_of(n_ctx + nq * blk, nq)
        kr0 = jnp.clip(NA_QROWS * blk - NA_WIN_H // 2, 0, rows - NA_KROWS)
        return q_start, pl.multiple_of(n_ctx + GRID_W * kr0, nq)

    def logits_to(bufs, blk):
        q_start, k_start = block_pos(blk)
        for h in range(2):
            qh = qh_s[h][pl.ds(q_start, nq), :]
            bufs[h][:, 0:win] = jnp.dot(qh, kt_s[:, pl.ds(k_start, win)], preferred_element_type=F32)
            bufs[h][:, win:win + n_ctx] = jnp.dot(qh, kc, preferred_element_type=F32)

    def attend_from(bufs, blk):
        q_start, k_start = block_pos(blk)
        case = jnp.where(blk == 0, 0, jnp.where(blk == nblk - 1, 2, 1))
        o = []
        for h in range(2):
            tiles = bias_ref[case, h]
            bias = jnp.concatenate([jnp.concatenate([tiles[r, a] for a in range(NA_KROWS // 2)], axis=1)
                                    for r in range(NA_QROWS)], axis=0)
            o.append(softmax_pv(h, [bufs[h][:, 0:win], bufs[h][:, win:win + n_ctx]], bias, [k_start, 0]))
        store(q_start, o)

    logits_to(la_s, 0)

    def two_blocks(it, carry):
        b0 = 2 * it
        logits_to(lb_s, b0 + 1)
        attend_from(la_s, b0)
        logits_to(la_s, jnp.minimum(b0 + 2, nblk - 1))
        attend_from(lb_s, b0 + 1)
        return carry

    lax.fori_loop(0, nblk // 2, two_blocks, 0)


def _na_attention(P, bias_tab, q_norm, k_norm, n_ctx):
    B, S, _ = P.shape
    width = NA_HEADS * NA_DIM
    npair = width // LANES
    nq = NA_QROWS * GRID_W
    rows = (S - n_ctx) // GRID_W
    assert n_ctx == nq and rows % (2 * NA_QROWS) == 0
    win = NA_KROWS * GRID_W
    hp = LANES // NA_DIM
    ones = jnp.asarray(np.kron(np.eye(hp), np.ones((NA_DIM, NA_DIM))), BF16)
    qn = jnp.tile(q_norm, hp).reshape(1, LANES)
    kn = jnp.tile(k_norm, hp).reshape(1, LANES)
    seq = lambda c: pl.BlockSpec((1, S, LANES), lambda b, p: (b, 0, c * npair + p))
    vec = pl.BlockSpec((1, LANES), lambda b, p: (0, 0))
    return pl.pallas_call(
        functools.partial(_na_kernel, n_ctx=n_ctx, rows=rows),
        grid=(B, npair),
        in_specs=[seq(0), seq(1), seq(2),
                  pl.BlockSpec((bias_tab.shape[0], hp) + bias_tab.shape[2:], lambda b, p: (0, p, 0, 0, 0, 0)),
                  vec, vec, pl.BlockSpec((LANES, LANES), lambda b, p: (0, 0))],
        out_specs=seq(0),
        out_shape=jax.ShapeDtypeStruct((B, S, width), BF16),
        scratch_shapes=([pltpu.VMEM((LANES, S), BF16)] + [pltpu.VMEM((S, LANES), BF16)] * 4
                        + [pltpu.VMEM((nq, win + n_ctx), F32)] * 4),
        compiler_params=_params("parallel", "parallel"), name="na_attention",
    )(P, P, P, bias_tab, qn, kn, ones)


def _rope_tables(n_ctx, T):
    half = ML_DIM // 4
    t = jnp.arange(T)
    freqs = ROPE_BASE ** (-jnp.arange(half, dtype=F32) / half)
    ang_r = (t // GRID_W).astype(F32)[:, None] * freqs[None, :]
    ang_c = (t % GRID_W).astype(F32)[:, None] * freqs[None, :]
    cos = jnp.concatenate([jnp.cos(ang_r)] * 2 + [jnp.cos(ang_c)] * 2, axis=-1)
    sin = jnp.concatenate([-jnp.sin(ang_r), jnp.sin(ang_r), -jnp.sin(ang_c), jnp.sin(ang_c)], axis=-1)
    cos = jnp.concatenate([jnp.ones((n_ctx, ML_DIM), F32), cos], axis=0)
    sin = jnp.concatenate([jnp.zeros((n_ctx, ML_DIM), F32), sin], axis=0)
    return cos, sin


def _rope(x, cos, sin):
    xr = jnp.concatenate([pltpu.roll(x[:, a * LANES:(a + 1) * LANES], LANES // 2, axis=1)
                          for a in range(x.shape[1] // LANES)], axis=1)
    return x * cos + xr * sin


def _mlstm_kernel(q_ref, k_ref, v_ref, g_ref, o_ref, c_ref, n_ref, m_ref, *, reverse):
    L = q_ref.shape[1]

    @pl.when(pl.program_id(1) == 0)
    def _():
        c_ref[...] = jnp.zeros_like(c_ref)
        n_ref[...] = jnp.zeros_like(n_ref)
        m_ref[...] = jnp.zeros_like(m_ref)

    t_idx = lax.broadcasted_iota(jnp.int32, (L, L), 0)
    s_idx = lax.broadcasted_iota(jnp.int32, (L, L), 1)
    mask = (s_idx >= t_idx) if reverse else (s_idx <= t_idx)
    tri = mask.astype(BF16)
    last = 0 if reverse else L - 1

    g = g_ref[0]
    lsg = _log_sigmoid(g)
    g_t = g.T
    b_cols = _cumsum_rows(tri, lsg)
    b_rows = _cumsum_cols(lsg.T, tri)
    ones = jnp.ones((L, LANES), BF16)

    heads = range(ML_HEADS)
    hs = [slice(h * ML_DIM, (h + 1) * ML_DIM) for h in heads]
    qb, kb, vb, kwb, decay, m_t, carry_w, m_new, dec = ([None] * ML_HEADS for _ in range(9))
    for h in heads:
        ii = (2 if reverse else 0) * ML_HEADS + h
        fi = (3 if reverse else 1) * ML_HEADS + h
        ig_col, ig_row = g[:, ii:ii + 1], g_t[ii:ii + 1, :]
        b_col, b_row = b_cols[:, fi:fi + 1], b_rows[fi:fi + 1, :]
        b_last = b_col[last:last + 1, :]
        qb[h], kb[h], vb[h] = q_ref[0, :, hs[h]], k_ref[0, :, hs[h]], v_ref[0, :, hs[h]]
        m_prev = m_ref[h, 0:1, 0:1]
        dmat = jnp.where(mask, b_col - b_row + ig_row, NEG)
        m_inter = b_col + m_prev
        m_t[h] = jnp.maximum(jnp.max(dmat, axis=1, keepdims=True), m_inter)
        decay[h] = jnp.exp(dmat - m_t[h])
        carry_w[h] = jnp.broadcast_to(jnp.exp(m_inter - m_t[h]), (L, LANES))
        g_col = b_last - b_col + ig_col
        m_new[h] = jnp.maximum(b_last + m_prev, jnp.max(g_col, axis=0, keepdims=True))
        kwb[h] = (kb[h].astype(F32) * jnp.exp(g_col - m_new[h])).astype(BF16)
        dec[h] = jnp.exp(b_last + m_prev - m_new[h])

    qk = [lax.dot_general(qb[h], kb[h], NT, preferred_element_type=F32) for h in heads]
    qc = [jnp.dot(qb[h], c_ref[h].astype(BF16), preferred_element_type=F32) for h in heads]
    qn = [jnp.dot(qb[h], n_ref[h].astype(BF16), preferred_element_type=F32) for h in heads]
    upd = [lax.dot_general(kwb[h], vb[h], TN, preferred_element_type=F32) for h in heads]
    kw_sum = [lax.dot_general(kwb[h], ones, TN, preferred_element_type=F32) for h in heads]
    sb = [(qk[h] * decay[h]).astype(BF16) for h in heads]
    sv = [jnp.dot(sb[h], vb[h], preferred_element_type=F32) for h in heads]
    s_sum = [jnp.dot(sb[h], ones, preferred_element_type=F32) for h in heads]
    for h in heads:
        den = s_sum[h] + carry_w[h] * qn[h]
        inv = 1.0 / jnp.maximum(jnp.abs(den), jnp.exp(-m_t[h]))
        num = sv[h] + jnp.concatenate([carry_w[h]] * (ML_DIM // LANES), axis=1) * qc[h]
        o_ref[0, :, hs[h]] = (num * jnp.concatenate([inv] * (ML_DIM // LANES), axis=1)).astype(o_ref.dtype)
        c_ref[h] = dec[h] * c_ref[h] + upd[h]
        n_ref[h] = dec[h] * n_ref[h] + kw_sum[h]
        m_ref[h] = jnp.broadcast_to(m_new[h], m_ref.shape[1:])


def _mlstm_scan(P, gates, n_ctx, reverse):
    B, S, _ = P.shape
    L = ML_CHUNK
    width = ML_HEADS * ML_DIM
    assert n_ctx == L and S % L == 0
    nch = S // L

    def cidx(j):
        return jnp.where(j == 0, 0, nch - j) if reverse else j

    return pl.pallas_call(
        functools.partial(_mlstm_kernel, reverse=reverse),
        grid=(B, nch),
        in_specs=[pl.BlockSpec((1, L, width), lambda b, j: (b, cidx(j), 4)),
                  pl.BlockSpec((1, L, width), lambda b, j: (b, cidx(j), 5)),
                  pl.BlockSpec((1, L, width), lambda b, j: (b, cidx(j), 6)),
                  pl.BlockSpec((1, L, LANES), lambda b, j: (b, cidx(j), 0))],
        out_specs=pl.BlockSpec((1, L, width), lambda b, j: (b, cidx(j), 0)),
        out_shape=jax.ShapeDtypeStruct((B, S, width), BF16),
        scratch_shapes=[pltpu.VMEM((ML_HEADS, ML_DIM, ML_DIM), F32),
                        pltpu.VMEM((ML_HEADS, ML_DIM, LANES), F32),
                        pltpu.VMEM((ML_HEADS, 8, LANES), F32)],
        compiler_params=_params("parallel", "arbitrary"), name="mlstm_bwd" if reverse else "mlstm_fwd",
    )(P, P, P, gates)


def _merge_ab_kernel(oa_ref, za_ref, hf_ref, hb_ref, ob_ref, zb_ref, x_ref, gate_ref, hn_ref, w_ref, o_ref, *, n_ctx):
    tm = x_ref.shape[1]
    width = oa_ref.shape[2]
    ya = oa_ref[0].astype(F32) * _silu(za_ref[0].astype(F32))
    hb = (hf_ref[0].astype(F32) + hb_ref[0].astype(F32)) * _sigmoid(ob_ref[0].astype(F32))
    parts = []
    for h in range(ML_HEADS):
        seg = hb[:, h * ML_DIM:(h + 1) * ML_DIM]
        parts.append(seg * lax.rsqrt(jnp.mean(seg * seg, axis=1, keepdims=True) + EPS))
    yb = jnp.concatenate(parts, axis=1) * hn_ref[...] * _silu(zb_ref[0].astype(F32))
    y = (jnp.dot(ya.astype(BF16), w_ref[0:width, :], preferred_element_type=F32)
         + jnp.dot(yb.astype(BF16), w_ref[width:2 * width, :], preferred_element_type=F32))
    row = pl.program_id(1) * tm + lax.broadcasted_iota(jnp.int32, (tm, 1), 0)
    gate = jnp.where(row < n_ctx, gate_ref[0, 0:1, :], gate_ref[0, 1:2, :])
    o_ref[0] = x_ref[0] + gate * y


def _merge_ab(oa, hf, hb, P, xc, gate, h_norm, w_out, n_ctx, tm):
    B, S, D = xc.shape
    width = oa.shape[2]
    tok = lambda c: pl.BlockSpec((1, tm, width), lambda b, i: (b, i, c))
    return pl.pallas_call(
        functools.partial(_merge_ab_kernel, n_ctx=n_ctx),
        grid=(B, S // tm),
        in_specs=[tok(0), tok(3), tok(0), tok(0), tok(7), tok(8),
                  pl.BlockSpec((1, tm, D), lambda b, i: (b, i, 0)),
                  pl.BlockSpec((1, 2, D), lambda b, i: (b, 0, 0)),
                  pl.BlockSpec((1, width), lambda b, i: (0, 0)),
                  pl.BlockSpec((2 * width, D), lambda b, i: (0, 0))],
        out_specs=pl.BlockSpec((1, tm, D), lambda b, i: (b, i, 0)),
        out_shape=jax.ShapeDtypeStruct((B, S, D), F32),
        compiler_params=_params("parallel", "parallel"), name="merge_ab",
    )(oa, P, hf, hb, P, P, xc, gate, h_norm.reshape(1, width), w_out)


def _gla_kernel(q_ref, f_ref, v_ref, lb_ref, o_ref, st_ref, qm_s, km_s, ev_s, *, reverse):
    C = HG_CHUNK
    T = q_ref.shape[1]
    nsub = T // C
    gw = q_ref.shape[2]

    @pl.when(pl.program_id(2) == 0)
    def _():
        st_ref[...] = jnp.zeros_like(st_ref)
        qm_s[...] = jnp.zeros_like(qm_s)
        km_s[...] = jnp.zeros_like(km_s)
        ev_s[...] = jnp.zeros_like(ev_s)

    def causal(n, same_chunk):
        t_idx = lax.broadcasted_iota(jnp.int32, (n, n), 0)
        s_idx = lax.broadcasted_iota(jnp.int32, (n, n), 1)
        m = (s_idx >= t_idx) if reverse else (s_idx <= t_idx)
        return m & (t_idx // C == s_idx // C) if same_chunk else m

    mask = causal(C, False)
    tri = causal(T, True).astype(BF16)
    last = 0 if reverse else C - 1
    mid = C // 2 if reverse else C // 2 - 1
    order = list(range(nsub - 1, -1, -1) if reverse else range(nsub))
    rows = [slice(c * C, (c + 1) * C) for c in range(nsub)]
    hw = gw // 2
    nh = hw // HG_DIM

    def decay_stage(half_idx):
        cols = slice(half_idx * hw, (half_idx + 1) * hw)
        lb = lb_ref[:, cols]
        c1 = 0.5 * (1.0 - lb)
        f = (lb + c1) + c1 * jnp.tanh(0.5 * f_ref[0, :, cols].astype(F32))
        b = _cumsum_rows(tri, jnp.log(f), terms=2)
        b_mid = [b[c * C + mid:c * C + mid + 1, :] for c in range(nsub)]
        b_last = [b[c * C + last:c * C + last + 1, :] for c in range(nsub)]
        b_mid_rows = jnp.concatenate([jnp.broadcast_to(r, (C, hw)) for r in b_mid], axis=0)
        qm = _silu(q_ref[0, :, cols].astype(F32)) * jnp.exp(b - b_mid_rows)
        km = (1.0 - f) * jnp.exp(b_mid_rows - b)
        return dict(cols=cols, qm=qm, km=km,
                    e_mid=[jnp.exp(r) for r in b_mid],
                    e_last_mid=[jnp.exp(l - r) for l, r in zip(b_last, b_mid)],
                    dec=[jnp.exp(l) for l in b_last])

    ev_names = ("e_mid", "e_last_mid", "dec")

    def pending(half_idx):
        cols = slice(half_idx * hw, (half_idx + 1) * hw)
        d = dict(cols=cols, qm=qm_s[:, cols], km=km_s[:, cols], vb=v_ref[0, :, cols])
        for n, name in enumerate(ev_names):
            d[name] = [ev_s[n, c:c + 1, cols] for c in range(nsub)]
        return d

    def stash(d):
        qm_s[:, d["cols"]] = d["qm"]
        km_s[:, d["cols"]] = d["km"]
        for n, name in enumerate(ev_names):
            for c in range(nsub):
                ev_s[n, c:c + 1, d["cols"]] = d[name][c]

    def score_stage(d):
        qmb, kmb, vb = d["qm"].astype(BF16), d["km"].astype(BF16), d["vb"]
        a, upd = {}, {}
        for g in range(nh):
            sl = slice(g * HG_DIM, (g + 1) * HG_DIM)
            for c, rs in enumerate(rows):
                s = lax.dot_general(qmb[rs, sl], kmb[rs, sl], NT, preferred_element_type=F32)
                a[g, c] = jnp.where(mask, s, 0.0).astype(BF16)
                kh = (d["km"][rs, sl] * d["e_last_mid"][c][:, sl]).astype(BF16)
                upd[g, c] = lax.dot_general(vb[rs, sl], kh, TN, preferred_element_type=F32)
        d["upd"] = upd
        d["intra"] = {(g, c): jnp.dot(a[g, c], vb[rows[c], g * HG_DIM:(g + 1) * HG_DIM],
                                      preferred_element_type=F32) for (g, c) in a}

    def state_stage(half_idx, d):
        for g in range(nh):
            sl = slice(g * HG_DIM, (g + 1) * HG_DIM)
            out_cols = slice(half_idx * hw + g * HG_DIM, half_idx * hw + (g + 1) * HG_DIM)
            st = st_ref[half_idx * nh + g]
            for c in order:
                rs = rows[c]
                qi = (d["qm"][rs, sl] * d["e_mid"][c][:, sl]).astype(BF16)
                inter_t = lax.dot_general(st.astype(BF16), qi, NT, preferred_element_type=F32)
                o_ref[0, rs, out_cols] = (d["intra"][g, c] + inter_t.T).astype(o_ref.dtype)
                st = st * d["dec"][c][:, sl] + d["upd"][g, c]
            st_ref[half_idx * nh + g] = st

    previous = [pending(0), pending(1)]
    score_stage(previous[0])
    score_stage(previous[1])
    state_stage(0, previous[0])
    state_stage(1, previous[1])
    stash(decay_stage(0))
    stash(decay_stage(1))


def _gla_scan(P, lb, n_ctx, reverse):
    B, S, _ = P.shape
    C = HG_BLOCK
    width = HG_HEADS * HG_DIM
    gw = HG_GROUP * HG_DIM
    ngrp = width // gw
    assert n_ctx % C == 0 and S % C == 0 and C % HG_CHUNK == 0
    nch, nctx_ch = S // C, n_ctx // C
    fcol = (2 if reverse else 1) * ngrp

    def cidx(j):
        if not reverse:
            return j
        return jnp.where(j < nctx_ch, nctx_ch - 1 - j, nch + nctx_ch - 1 - j)

    def oidx(j):
        return jnp.maximum(cidx(j) - nctx_ch, 0) if not reverse else jnp.where(j < nctx_ch, nch - nctx_ch - 1,
                                                                                cidx(j) - nctx_ch)

    head = lambda j: cidx(jnp.minimum(j, nch - 1))
    tail = lambda j: jnp.maximum(j - 1, 0)
    return pl.pallas_call(
        functools.partial(_gla_kernel, reverse=reverse),
        grid=(B, ngrp, nch + 1),
        in_specs=[pl.BlockSpec((1, C, gw), lambda b, g, j: (b, head(j), g)),
                  pl.BlockSpec((1, C, gw), lambda b, g, j: (b, head(j), fcol + g)),
                  pl.BlockSpec((1, C, gw), lambda b, g, j: (b, cidx(tail(j)), 3 * ngrp + g)),
                  pl.BlockSpec((1, gw), lambda b, g, j: (0, g))],
        out_specs=pl.BlockSpec((1, C, gw), lambda b, g, j: (b, oidx(tail(j)), g)),
        out_shape=jax.ShapeDtypeStruct((B, S - n_ctx, width), BF16),
        scratch_shapes=[pltpu.VMEM((HG_GROUP, HG_DIM, HG_DIM), F32),
                        pltpu.VMEM((C, gw), F32), pltpu.VMEM((C, gw), F32),
                        pltpu.VMEM((3, 8, gw), F32)],
        compiler_params=_params("parallel", "parallel", "arbitrary"), name="gla_bwd" if reverse else "gla_fwd",
    )(P, P, P, lb)


def _merge_c_kernel(of_ref, ob_ref, z_ref, x_ref, gate_ref, hn_ref, w_ref, o_ref):
    o = of_ref[0].astype(F32) + ob_ref[0].astype(F32)
    parts = []
    for h in range(HG_HEADS):
        seg = o[:, h * HG_DIM:(h + 1) * HG_DIM]
        parts.append(seg * lax.rsqrt(jnp.mean(seg * seg, axis=1, keepdims=True) + EPS))
    y = jnp.concatenate(parts, axis=1) * hn_ref[...] * _silu(z_ref[0].astype(F32))
    o_ref[0] = x_ref[0] + gate_ref[0] * jnp.dot(y.astype(BF16), w_ref[...], preferred_element_type=F32)


def _merge_c(of, ob, P, xc, gate, h_norm, w_out, n_ctx, tm):
    B, T, width = of.shape
    D = xc.shape[2]
    off = n_ctx // tm
    return pl.pallas_call(
        _merge_c_kernel,
        grid=(B, T // tm),
        in_specs=[pl.BlockSpec((1, tm, width), lambda b, i: (b, i, 0)),
                  pl.BlockSpec((1, tm, width), lambda b, i: (b, i, 0)),
                  pl.BlockSpec((1, tm, width), lambda b, i: (b, i + off, 4)),
                  pl.BlockSpec((1, tm, D), lambda b, i: (b, i + off, 0)),
                  pl.BlockSpec((1, 1, D), lambda b, i: (b, 0, 0)),
                  pl.BlockSpec((1, width), lambda b, i: (0, 0)),
                  pl.BlockSpec((width, D), lambda b, i: (0, 0))],
        out_specs=pl.BlockSpec((1, tm, D), lambda b, i: (b, i, 0)),
        out_shape=jax.ShapeDtypeStruct((B, T, D), F32),
        compiler_params=_params("parallel", "parallel"), name="merge_c",
    )(of, ob, P, xc, gate, h_norm.reshape(1, width), w_out)


def _pad_cols(w, n):
    return jnp.pad(w, ((0, 0), (0, n - w.shape[1])))


def kernel(x, c, ctx, c_ctx, norm_w, w_ada, b_ada, w_in_ab, b_gate_ab, q_norm_a, k_norm_a, rpb_a, h_norm_b, w_out_ab,
           w_in_c, lb_c, h_norm_c, w_out_c):
    B, T, D = x.shape
    n_ctx = ctx.shape[1]
    S = n_ctx + T
    tm_in = S // 4
    tm_out = S // 16
    xc = jnp.concatenate([ctx, x], axis=1)
    cvec = jnp.concatenate([c, c_ctx[None], jnp.zeros((16 - B - 1, D), F32)], axis=0)

    def modulation(l):
        m = _ada(cvec, w_ada[l], b_ada[l])
        shift, scale, gate = m[:, :D], m[:, D:2 * D], m[:, 2 * D:]
        ctx_row = lambda a: jnp.broadcast_to(a[B][None], (B, D))
        mod = jnp.stack([ctx_row(shift), ctx_row(scale), shift[:B], scale[:B]], axis=1)
        return mod, jnp.stack([ctx_row(gate), gate[:B]], axis=1)

    mod, gate = modulation(0)
    n_main = 9 * NA_HEADS * NA_DIM
    w_main = w_in_ab[0][:, :n_main].astype(BF16)
    w_gate = _pad_cols(w_in_ab[0][:, n_main:], LANES).astype(BF16)
    b_gate = _pad_cols(b_gate_ab[0][None], LANES)
    ml_w = ML_HEADS * ML_DIM
    P = _inproj(xc, norm_w[0], mod, w_main, jnp.zeros((1, n_main), F32), n_ctx, tm_in, n_main // 3, BF16,
                rope=_rope_tables(n_ctx, T) + (1, ml_w, 2 * ml_w))
    gates = _inproj(xc, norm_w[0], mod, w_gate, b_gate, n_ctx, tm_in, LANES, F32)
    oa = _na_attention(P, _na_bias_tables(rpb_a[0], T // GRID_W), q_norm_a[0], k_norm_a[0], n_ctx)
    hf = _mlstm_scan(P, gates, n_ctx, reverse=False)
    hb = _mlstm_scan(P, gates, n_ctx, reverse=True)
    xc = _merge_ab(oa, hf, hb, P, xc, gate, h_norm_b[0], w_out_ab[0].astype(BF16), n_ctx, tm_out)

    mod, gate = modulation(1)
    sm = jax.nn.softmax(lb_c.astype(F32), axis=0)
    lb = (jnp.cumsum(sm, axis=0) - sm[0])[1].reshape(1, -1)
    hw = HG_HEADS * HG_DIM
    P = _inproj(xc, norm_w[1], mod, w_in_c[0].astype(BF16), jnp.zeros((1, 5 * hw), F32), n_ctx, tm_in, hw, BF16)
    of = _gla_scan(P, lb, n_ctx, reverse=False)
    ob = _gla_scan(P, lb, n_ctx, reverse=True)
    return _merge_c(of, ob, P, xc, gate[:, 1:2], h_norm_c[0], w_out_c[0].astype(BF16), n_ctx, n_ctx)
```

```python
import functools

import numpy as np
import jax
import jax.numpy as jnp
from jax import lax
from jax.experimental import pallas as pl
from jax.experimental.pallas import tpu as pltpu

F32 = jnp.float32
BF16 = jnp.bfloat16
HIGHEST = lax.Precision.HIGHEST

GRID_W = 64
NA_HEADS = 16
NA_DIM = 64
NA_WIN_H = 8
NA_WIN_W = 16
ML_HEADS = 4
ML_DIM = 256
HG_HEADS = 16
HG_DIM = 128
ROPE_BASE = 10000.0
EPS = 1e-6
NEG = -1e30
LOG2E = 1.4426950408889634

LANES = 128
V7X_VMEM_LIMIT = 56 * 1024 * 1024

NA_QROWS = 4
NA_KROWS = NA_QROWS + NA_WIN_H
ML_CHUNK = 256
HG_CHUNK = 64
HG_BLOCK = 256
HG_GROUP = 8

NT = (((1,), (1,)), ((), ()))
TN = (((0,), (0,)), ((), ()))


def _params(*sem):
    return pltpu.CompilerParams(dimension_semantics=sem, vmem_limit_bytes=V7X_VMEM_LIMIT)


def _sigmoid(z):
    return 0.5 * jnp.tanh(0.5 * z) + 0.5


def _silu(z):
    h = 0.5 * z
    return h + h * jnp.tanh(h)


def _log_sigmoid(z):
    return jnp.minimum(z, 0.0) - jnp.log1p(jnp.exp(-jnp.abs(z)))


def _split_bf16(x, terms):
    out = []
    for _ in range(terms - 1):
        out.append(x.astype(BF16))
        x = x - out[-1].astype(F32)
    return out + [x.astype(BF16)]


def _cumsum_rows(tri_bf16, x, terms=3):
    return sum(jnp.dot(tri_bf16, t, preferred_element_type=F32) for t in _split_bf16(x, terms))


def _cumsum_cols(x, tri_bf16, terms=3):
    return sum(lax.dot_general(t, tri_bf16, NT, preferred_element_type=F32) for t in _split_bf16(x, terms))


def _ada_kernel(c_ref, w_ref, b_ref, o_ref):
    s = _silu(c_ref[...])
    o_ref[...] = jnp.dot(s.astype(BF16), w_ref[...].astype(BF16), preferred_element_type=F32) + b_ref[...]


def _ada(cvec, w, b):
    R, D = cvec.shape
    N = w.shape[1]
    tn = D
    return pl.pallas_call(
        _ada_kernel, grid=(N // tn,),
        in_specs=[pl.BlockSpec((R, D), lambda n: (0, 0)),
                  pl.BlockSpec((D, tn), lambda n: (0, n)),
                  pl.BlockSpec((1, tn), lambda n: (0, n))],
        out_specs=pl.BlockSpec((R, tn), lambda n: (0, n)),
        out_shape=jax.ShapeDtypeStruct((R, N), F32),
        compiler_params=_params("arbitrary"), name="ada",
    )(cvec, w, b.reshape(1, N))


def _inproj_kernel(x_ref, nw_ref, mod_ref, w_ref, bias_ref, *rest, n_ctx, rope_cols):
    if rope_cols is None:
        o_ref, hn_ref = rest
    else:
        cos_ref, sin_ref, o_ref, hn_ref = rest
    i = pl.program_id(1)
    n = pl.program_id(2)
    tm = x_ref.shape[1]

    def normed():
        x = x_ref[0]
        y = x * lax.rsqrt(jnp.mean(x * x, axis=-1, keepdims=True) + EPS) * nw_ref[...]
        row = i * tm + lax.broadcasted_iota(jnp.int32, (tm, 1), 0)
        is_ctx = row < n_ctx
        shift = jnp.where(is_ctx, mod_ref[0, 0:1, :], mod_ref[0, 2:3, :])
        scale = jnp.where(is_ctx, mod_ref[0, 1:2, :], mod_ref[0, 3:4, :])
        return (y * (1.0 + scale) + shift).astype(BF16)

    def project(h, rotate):
        acc = jnp.dot(h, w_ref[...], preferred_element_type=F32) + bias_ref[...]
        if not rotate:
            o_ref[0] = acc.astype(o_ref.dtype)
            return
        _, q_col, k_col = rope_cols
        ml_w = ML_HEADS * ML_DIM
        cos, sin = cos_ref[...], sin_ref[...]
        for c in range(0, acc.shape[1], ML_DIM):
            cs = slice(c, c + ML_DIM)
            if q_col <= c < q_col + ml_w:
                o_ref[0, :, cs] = _rope(acc[:, cs], cos, sin).astype(o_ref.dtype)
            elif k_col <= c < k_col + ml_w:
                o_ref[0, :, cs] = (_rope(acc[:, cs], cos, sin) * (ML_DIM ** -0.5)).astype(o_ref.dtype)
            else:
                o_ref[0, :, cs] = acc[:, cs].astype(o_ref.dtype)

    rope_tile = None if rope_cols is None else rope_cols[0]

    @pl.when(n == 0)
    def _():
        h = normed()
        hn_ref[...] = h
        project(h, rope_tile == 0)

    if rope_tile not in (None, 0):
        @pl.when(n == rope_tile)
        def _():
            project(hn_ref[...], True)

    @pl.when((n != 0) if rope_tile in (None, 0) else ((n != 0) & (n != rope_tile)))
    def _():
        project(hn_ref[...], False)


def _inproj(xc, norm_w, mod, w, bias, n_ctx, tm, tn, out_dtype, rope=None):
    B, S, D = xc.shape
    N = w.shape[1]
    in_specs = [pl.BlockSpec((1, tm, D), lambda b, i, n: (b, i, 0)),
                pl.BlockSpec((1, D), lambda b, i, n: (0, 0)),
                pl.BlockSpec((1, 4, D), lambda b, i, n: (b, 0, 0)),
                pl.BlockSpec((D, tn), lambda b, i, n: (0, n)),
                pl.BlockSpec((1, tn), lambda b, i, n: (0, n))]
    args = [xc, norm_w.reshape(1, D), mod, w, bias]
    if rope is not None:
        in_specs += [pl.BlockSpec((tm, ML_DIM), lambda b, i, n: (i, 0))] * 2
        args += list(rope[:2])
    return pl.pallas_call(
        functools.partial(_inproj_kernel, n_ctx=n_ctx, rope_cols=None if rope is None else tuple(rope[2:])),
        grid=(B, S // tm, N // tn),
        in_specs=in_specs,
        out_specs=pl.BlockSpec((1, tm, tn), lambda b, i, n: (b, i, n)),
        out_shape=jax.ShapeDtypeStruct((B, S, N), out_dtype),
        scratch_shapes=[pltpu.VMEM((tm, D), BF16)],
        compiler_params=_params("parallel", "parallel", "arbitrary"), name="inproj",
    )(*args)


def _na_bias_tables(rpb, rows):
    H, nri, nci = rpb.shape
    qc, kc = np.arange(GRID_W)[:, None], np.arange(GRID_W)[None, :]
    c0 = np.clip(qc - NA_WIN_W // 2, 0, GRID_W - NA_WIN_W)
    col_ok = (kc >= c0) & (kc < c0 + NA_WIN_W)
    col_sel = ((kc - qc + NA_WIN_W - 1)[..., None] == np.arange(nci)) & col_ok[..., None]
    sel = np.einsum("de,qkj->qdkej", np.eye(2), col_sel).reshape(GRID_W, 2, GRID_W, 2 * nci).astype(np.float32)
    rp = jnp.pad(rpb.astype(F32), ((0, 0), (1, 1), (0, 0)))
    pair = jnp.stack([rp[:, :nri + 1], rp[:, 1:]], axis=2).reshape(H, nri + 1, 2 * nci)
    tiles = jnp.einsum("hic,qdkc->hiqdk", pair, sel, precision=HIGHEST).reshape(H, nri + 1, GRID_W, 2 * GRID_W)

    qr, ka = np.arange(NA_QROWS)[:, None], np.arange(NA_KROWS)[None, :]
    nblk = rows // NA_QROWS
    tables = []
    for blk in (0, 1, nblk - 1):
        kr0 = int(np.clip(NA_QROWS * blk - NA_WIN_H // 2, 0, rows - NA_KROWS))
        r = NA_QROWS * blk + qr
        r0 = np.clip(r - NA_WIN_H // 2, 0, rows - NA_WIN_H)
        krow = kr0 + ka
        row_ok = (krow >= r0) & (krow < r0 + NA_WIN_H)
        idx = np.clip((krow - r + NA_WIN_H - 1)[:, 0::2] + 1, 0, nri)
        valid = (row_ok.reshape(NA_QROWS, NA_KROWS // 2, 1, 2, 1) & col_ok[None, None, :, None, :])
        valid = valid.reshape(NA_QROWS, NA_KROWS // 2, GRID_W, 2 * GRID_W)
        tables.append(jnp.where(valid[None], tiles[:, idx] * LOG2E, NEG))
    return jnp.stack(tables)


def _na_kernel(q_ref, k_ref, v_ref, bias_ref, qn_ref, kn_ref, ones_ref, o_ref, kt_s, q0_s, q1_s, v0_s, v1_s,
               la0_s, la1_s, lb0_s, lb1_s, *, n_ctx, rows):
    la_s, lb_s = (la0_s, la1_s), (lb0_s, lb1_s)
    nq = NA_QROWS * GRID_W
    win = NA_KROWS * GRID_W
    nblk = rows // NA_QROWS
    lane = lax.broadcasted_iota(jnp.int32, (1, LANES), 1)
    in_h0 = lane < NA_DIM
    qh_s, vh_s = (q0_s, q1_s), (v0_s, v1_s)
    den_lane = (NA_DIM, 0)

    def rms(x, w):
        ss = jnp.dot((x * x).astype(BF16), ones_ref[...], preferred_element_type=F32)
        return x * lax.rsqrt(ss * (1.0 / NA_DIM) + EPS) * w

    kt_s[...] = rms(k_ref[0].astype(F32), kn_ref[...]).T.astype(BF16)
    q = rms(q_ref[0].astype(F32), qn_ref[...]) * (NA_DIM ** -0.5 * LOG2E)
    q0_s[...] = jnp.where(in_h0, q, 0.0).astype(BF16)
    q1_s[...] = jnp.where(in_h0, 0.0, q).astype(BF16)
    v = v_ref[0].astype(F32)
    v0_s[...] = jnp.where(in_h0, v, jnp.where(lane == den_lane[0], 1.0, 0.0)).astype(BF16)
    v1_s[...] = jnp.where(in_h0, jnp.where(lane == den_lane[1], 1.0, 0.0), v).astype(BF16)

    def softmax_pv(h, s_parts, bias, v_starts):
        if bias is not None:
            s_parts = [s_parts[0] + bias] + s_parts[1:]
        m = functools.reduce(jnp.maximum, [jnp.max(s, axis=1, keepdims=True) for s in s_parts])
        return sum(jnp.dot(jnp.exp2(s - m).astype(BF16), vh_s[h][pl.ds(st, s.shape[1]), :],
                           preferred_element_type=F32) for s, st in zip(s_parts, v_starts))

    def store(q_start, o):
        den = [o[h][:, den_lane[h]:den_lane[h] + 1] for h in range(2)]
        o_ref[0, pl.ds(q_start, nq), :] = jnp.where(in_h0, o[0] / den[0], o[1] / den[1]).astype(o_ref.dtype)

    kc = kt_s[:, 0:n_ctx]
    s_ctx = [jnp.dot(qh_s[h][0:n_ctx, :], kc, preferred_element_type=F32) for h in range(2)]
    store(0, [softmax_pv(h, [s_ctx[h]], None, [0]) for h in range(2)])

    def block_pos(blk):
        q_start = pl.multiple_of(n_ctx + nq * blk, nq)
        kr0 = jnp.clip(NA_QROWS * blk - NA_WIN_H // 2, 0, rows - NA_KROWS)
        return q_start, pl.multiple_of(n_ctx + GRID_W * kr0, nq)

    def logits_to(bufs, blk):
        q_start, k_start = block_pos(blk)
        for h in range(2):
            qh = qh_s[h][pl.ds(q_start, nq), :]
            bufs[h][:, 0:win] = jnp.dot(qh, kt_s[:, pl.ds(k_start, win)], preferred_element_type=F32)
            bufs[h][:, win:win + n_ctx] = jnp.dot(qh, kc, preferred_element_type=F32)

    def attend_from(bufs, blk):
        q_start, k_start = block_pos(blk)
        case = jnp.where(blk == 0, 0, jnp.where(blk == nblk - 1, 2, 1))
        o = []
        for h in range(2):
            tiles = bias_ref[case, h]
            bias = jnp.concatenate([jnp.concatenate([tiles[r, a] for a in range(NA_KROWS // 2)], axis=1)
                                    for r in range(NA_QROWS)], axis=0)
            o.append(softmax_pv(h, [bufs[h][:, 0:win], bufs[h][:, win:win + n_ctx]], bias, [k_start, 0]))
        store(q_start, o)

    logits_to(la_s, 0)

    def two_blocks(it, carry):
        b0 = 2 * it
        logits_to(lb_s, b0 + 1)
        attend_from(la_s, b0)
        logits_to(la_s, jnp.minimum(b0 + 2, nblk - 1))
        attend_from(lb_s, b0 + 1)
        return carry

    lax.fori_loop(0, nblk // 2, two_blocks, 0)


def _na_attention(P, bias_tab, q_norm, k_norm, n_ctx):
    B, S, _ = P.shape
    width = NA_HEADS * NA_DIM
    npair = width // LANES
    nq = NA_QROWS * GRID_W
    rows = (S - n_ctx) // GRID_W
    assert n_ctx == nq and rows % (2 * NA_QROWS) == 0
    win = NA_KROWS * GRID_W
    hp = LANES // NA_DIM
    ones = jnp.asarray(np.kron(np.eye(hp), np.ones((NA_DIM, NA_DIM))), BF16)
    qn = jnp.tile(q_norm, hp).reshape(1, LANES)
    kn = jnp.tile(k_norm, hp).reshape(1, LANES)
    seq = lambda c: pl.BlockSpec((1, S, LANES), lambda b, p: (b, 0, c * npair + p))
    vec = pl.BlockSpec((1, LANES), lambda b, p: (0, 0))
    return pl.pallas_call(
        functools.partial(_na_kernel, n_ctx=n_ctx, rows=rows),
        grid=(B, npair),
        in_specs=[seq(0), seq(1), seq(2),
                  pl.BlockSpec((bias_tab.shape[0], hp) + bias_tab.shape[2:], lambda b, p: (0, p, 0, 0, 0, 0)),
                  vec, vec, pl.BlockSpec((LANES, LANES), lambda b, p: (0, 0))],
        out_specs=seq(0),
        out_shape=jax.ShapeDtypeStruct((B, S, width), BF16),
        scratch_shapes=([pltpu.VMEM((LANES, S), BF16)] + [pltpu.VMEM((S, LANES), BF16)] * 4
                        + [pltpu.VMEM((nq, win + n_ctx), F32)] * 4),
        compiler_params=_params("parallel", "parallel"), name="na_attention",
    )(P, P, P, bias_tab, qn, kn, ones)


def _rope_tables(n_ctx, T):
    half = ML_DIM // 4
    t = jnp.arange(T)
    freqs = ROPE_BASE ** (-jnp.arange(half, dtype=F32) / half)
    ang_r = (t // GRID_W).astype(F32)[:, None] * freqs[None, :]
    ang_c = (t % GRID_W).astype(F32)[:, None] * freqs[None, :]
    cos = jnp.concatenate([jnp.cos(ang_r)] * 2 + [jnp.cos(ang_c)] * 2, axis=-1)
    sin = jnp.concatenate([-jnp.sin(ang_r), jnp.sin(ang_r), -jnp.sin(ang_c), jnp.sin(ang_c)], axis=-1)
    cos = jnp.concatenate([jnp.ones((n_ctx, ML_DIM), F32), cos], axis=0)
    sin = jnp.concatenate([jnp.zeros((n_ctx, ML_DIM), F32), sin], axis=0)
    return cos, sin


def _rope(x, cos, sin):
    xr = jnp.concatenate([pltpu.roll(x[:, a * LANES:(a + 1) * LANES], LANES // 2, axis=1)
                          for a in range(x.shape[1] // LANES)], axis=1)
    return x * cos + xr * sin


def _mlstm_kernel(q_ref, k_ref, v_ref, g_ref, o_ref, c_ref, n_ref, m_ref, *, reverse):
    L = q_ref.shape[1]

    @pl.when(pl.program_id(1) == 0)
    def _():
        c_ref[...] = jnp.zeros_like(c_ref)
        n_ref[...] = jnp.zeros_like(n_ref)
        m_ref[...] = jnp.zeros_like(m_ref)

    t_idx = lax.broadcasted_iota(jnp.int32, (L, L), 0)
    s_idx = lax.broadcasted_iota(jnp.int32, (L, L), 1)
    mask = (s_idx >= t_idx) if reverse else (s_idx <= t_idx)
    tri = mask.astype(BF16)
    last = 0 if reverse else L - 1

    g = g_ref[0]
    lsg = _log_sigmoid(g)
    g_t = g.T
    b_cols = _cumsum_rows(tri, lsg)
    b_rows = _cumsum_cols(lsg.T, tri)
    ones = jnp.ones((L, LANES), BF16)

    heads = range(ML_HEADS)
    hs = [slice(h * ML_DIM, (h + 1) * ML_DIM) for h in heads]
    qb, kb, vb, kwb, decay, m_t, carry_w, m_new, dec = ([None] * ML_HEADS for _ in range(9))
    for h in heads:
        ii = (2 if reverse else 0) * ML_HEADS + h
        fi = (3 if reverse else 1) * ML_HEADS + h
        ig_col, ig_row = g[:, ii:ii + 1], g_t[ii:ii + 1, :]
        b_col, b_row = b_cols[:, fi:fi + 1], b_rows[fi:fi + 1, :]
        b_last = b_col[last:last + 1, :]
        qb[h], kb[h], vb[h] = q_ref[0, :, hs[h]], k_ref[0, :, hs[h]], v_ref[0, :, hs[h]]
        m_prev = m_ref[h, 0:1, 0:1]
        dmat = jnp.where(mask, b_col - b_row + ig_row, NEG)
        m_inter = b_col + m_prev
        m_t[h] = jnp.maximum(jnp.max(dmat, axis=1, keepdims=True), m_inter)
        decay[h] = jnp.exp(dmat - m_t[h])
        carry_w[h] = jnp.broadcast_to(jnp.exp(m_inter - m_t[h]), (L, LANES))
        g_col = b_last - b_col + ig_col
        m_new[h] = jnp.maximum(b_last + m_prev, jnp.max(g_col, axis=0, keepdims=True))
        kwb[h] = (kb[h].astype(F32) * jnp.exp(g_col - m_new[h])).astype(BF16)
        dec[h] = jnp.exp(b_last + m_prev - m_new[h])

    qk = [lax.dot_general(qb[h], kb[h], NT, preferred_element_type=F32) for h in heads]
    qc = [jnp.dot(qb[h], c_ref[h].astype(BF16), preferred_element_type=F32) for h in heads]
    qn = [jnp.dot(qb[h], n_ref[h].astype(BF16), preferred_element_type=F32) for h in heads]
    upd = [lax.dot_general(kwb[h], vb[h], TN, preferred_element_type=F32) for h in heads]
    kw_sum = [lax.dot_general(kwb[h], ones, TN, preferred_element_type=F32) for h in heads]
    sb = [(qk[h] * decay[h]).astype(BF16) for h in heads]
    sv = [jnp.dot(sb[h], vb[h], preferred_element_type=F32) for h in heads]
    s_sum = [jnp.dot(sb[h], ones, preferred_element_type=F32) for h in heads]
    for h in heads:
        den = s_sum[h] + carry_w[h] * qn[h]
        inv = 1.0 / jnp.maximum(jnp.abs(den), jnp.exp(-m_t[h]))
        num = sv[h] + jnp.concatenate([carry_w[h]] * (ML_DIM // LANES), axis=1) * qc[h]
        o_ref[0, :, hs[h]] = (num * jnp.concatenate([inv] * (ML_DIM // LANES), axis=1)).astype(o_ref.dtype)
        c_ref[h] = dec[h] * c_ref[h] + upd[h]
        n_ref[h] = dec[h] * n_ref[h] + kw_sum[h]
        m_ref[h] = jnp.broadcast_to(m_new[h], m_ref.shape[1:])


def _mlstm_scan(P, gates, n_ctx, reverse):
    B, S, _ = P.shape
    L = ML_CHUNK
    width = ML_HEADS * ML_DIM
    assert n_ctx == L and S % L == 0
    nch = S // L

    def cidx(j):
        return jnp.where(j == 0, 0, nch - j) if reverse else j

    return pl.pallas_call(
        functools.partial(_mlstm_kernel, reverse=reverse),
        grid=(B, nch),
        in_specs=[pl.BlockSpec((1, L, width), lambda b, j: (b, cidx(j), 4)),
                  pl.BlockSpec((1, L, width), lambda b, j: (b, cidx(j), 5)),
                  pl.BlockSpec((1, L, width), lambda b, j: (b, cidx(j), 6)),
                  pl.BlockSpec((1, L, LANES), lambda b, j: (b, cidx(j), 0))],
        out_specs=pl.BlockSpec((1, L, width), lambda b, j: (b, cidx(j), 0)),
        out_shape=jax.ShapeDtypeStruct((B, S, width), BF16),
        scratch_shapes=[pltpu.VMEM((ML_HEADS, ML_DIM, ML_DIM), F32),
                        pltpu.VMEM((ML_HEADS, ML_DIM, LANES), F32),
                        pltpu.VMEM((ML_HEADS, 8, LANES), F32)],
        compiler_params=_params("parallel", "arbitrary"), name="mlstm_bwd" if reverse else "mlstm_fwd",
    )(P, P, P, gates)


def _merge_ab_kernel(oa_ref, za_ref, hf_ref, hb_ref, ob_ref, zb_ref, x_ref, gate_ref, hn_ref, w_ref, o_ref, *, n_ctx):
    tm = x_ref.shape[1]
    width = oa_ref.shape[2]
    ya = oa_ref[0].astype(F32) * _silu(za_ref[0].astype(F32))
    hb = (hf_ref[0].astype(F32) + hb_ref[0].astype(F32)) * _sigmoid(ob_ref[0].astype(F32))
    parts = []
    for h in range(ML_HEADS):
        seg = hb[:, h * ML_DIM:(h + 1) * ML_DIM]
        parts.append(seg * lax.rsqrt(jnp.mean(seg * seg, axis=1, keepdims=True) + EPS))
    yb = jnp.concatenate(parts, axis=1) * hn_ref[...] * _silu(zb_ref[0].astype(F32))
    y = (jnp.dot(ya.astype(BF16), w_ref[0:width, :], preferred_element_type=F32)
         + jnp.dot(yb.astype(BF16), w_ref[width:2 * width, :], preferred_element_type=F32))
    row = pl.program_id(1) * tm + lax.broadcasted_iota(jnp.int32, (tm, 1), 0)
    gate = jnp.where(row < n_ctx, gate_ref[0, 0:1, :], gate_ref[0, 1:2, :])
    o_ref[0] = x_ref[0] + gate * y


def _merge_ab(oa, hf, hb, P, xc, gate, h_norm, w_out, n_ctx, tm):
    B, S, D = xc.shape
    width = oa.shape[2]
    tok = lambda c: pl.BlockSpec((1, tm, width), lambda b, i: (b, i, c))
    return pl.pallas_call(
        functools.partial(_merge_ab_kernel, n_ctx=n_ctx),
        grid=(B, S // tm),
        in_specs=[tok(0), tok(3), tok(0), tok(0), tok(7), tok(8),
                  pl.BlockSpec((1, tm, D), lambda b, i: (b, i, 0)),
                  pl.BlockSpec((1, 2, D), lambda b, i: (b, 0, 0)),
                  pl.BlockSpec((1, width), lambda b, i: (0, 0)),
                  pl.BlockSpec((2 * width, D), lambda b, i: (0, 0))],
        out_specs=pl.BlockSpec((1, tm, D), lambda b, i: (b, i, 0)),
        out_shape=jax.ShapeDtypeStruct((B, S, D), F32),
        compiler_params=_params("parallel", "parallel"), name="merge_ab",
    )(oa, P, hf, hb, P, P, xc, gate, h_norm.reshape(1, width), w_out)


def _gla_kernel(q_ref, f_ref, v_ref, lb_ref, o_ref, st_ref, qm_s, km_s, ev_s, *, reverse):
    C = HG_CHUNK
    T = q_ref.shape[1]
    nsub = T // C
    gw = q_ref.shape[2]

    @pl.when(pl.program_id(2) == 0)
    def _():
        st_ref[...] = jnp.zeros_like(st_ref)
        qm_s[...] = jnp.zeros_like(qm_s)
        km_s[...] = jnp.zeros_like(km_s)
        ev_s[...] = jnp.zeros_like(ev_s)

    def causal(n, same_chunk):
        t_idx = lax.broadcasted_iota(jnp.int32, (n, n), 0)
        s_idx = lax.broadcasted_iota(jnp.int32, (n, n), 1)
        m = (s_idx >= t_idx) if reverse else (s_idx <= t_idx)
        return m & (t_idx // C == s_idx // C) if same_chunk else m

    mask = causal(C, False)
    tri = causal(T, True).astype(BF16)
    last = 0 if reverse else C - 1
    mid = C // 2 if reverse else C // 2 - 1
    order = list(range(nsub - 1, -1, -1) if reverse else range(nsub))
    rows = [slice(c * C, (c + 1) * C) for c in range(nsub)]
    hw = gw // 2
    nh = hw // HG_DIM

    def decay_stage(half_idx):
        cols = slice(half_idx * hw, (half_idx + 1) * hw)
        lb = lb_ref[:, cols]
        c1 = 0.5 * (1.0 - lb)
        f = (lb + c1) + c1 * jnp.tanh(0.5 * f_ref[0, :, cols].astype(F32))
        b = _cumsum_rows(tri, jnp.log(f), terms=2)
        b_mid = [b[c * C + mid:c * C + mid + 1, :] for c in range(nsub)]
        b_last = [b[c * C + last:c * C + last + 1, :] for c in range(nsub)]
        b_mid_rows = jnp.concatenate([jnp.broadcast_to(r, (C, hw)) for r in b_mid], axis=0)
        qm = _silu(q_ref[0, :, cols].astype(F32)) * jnp.exp(b - b_mid_rows)
        km = (1.0 - f) * jnp.exp(b_mid_rows - b)
        return dict(cols=cols, qm=qm, km=km,
                    e_mid=[jnp.exp(r) for r in b_mid],
                    e_last_mid=[jnp.exp(l - r) for l, r in zip(b_last, b_mid)],
                    dec=[jnp.exp(l) for l in b_last])

    ev_names = ("e_mid", "e_last_mid", "dec")

    def pending(half_idx):
        cols = slice(half_idx * hw, (half_idx + 1) * hw)
        d = dict(cols=cols, qm=qm_s[:, cols], km=km_s[:, cols], vb=v_ref[0, :, cols])
        for n, name in enumerate(ev_names):
            d[name] = [ev_s[n, c:c + 1, cols] for c in range(nsub)]
        return d

    def stash(d):
        qm_s[:, d["cols"]] = d["qm"]
        km_s[:, d["cols"]] = d["km"]
        for n, name in enumerate(ev_names):
            for c in range(nsub):
                ev_s[n, c:c + 1, d["cols"]] = d[name][c]

    def score_stage(d):
        qmb, kmb, vb = d["qm"].astype(BF16), d["km"].astype(BF16), d["vb"]
        a, upd = {}, {}
        for g in range(nh):
            sl = slice(g * HG_DIM, (g + 1) * HG_DIM)
            for c, rs in enumerate(rows):
                s = lax.dot_general(qmb[rs, sl], kmb[rs, sl], NT, preferred_element_type=F32)
                a[g, c] = jnp.where(mask, s, 0.0).astype(BF16)
                kh = (d["km"][rs, sl] * d["e_last_mid"][c][:, sl]).astype(BF16)
                upd[g, c] = lax.dot_general(vb[rs, sl], kh, TN, preferred_element_type=F32)
        d["upd"] = upd
        d["intra"] = {(g, c): jnp.dot(a[g, c], vb[rows[c], g * HG_DIM:(g + 1) * HG_DIM],
                                      preferred_element_type=F32) for (g, c) in a}

    def state_stage(half_idx, d):
        for g in range(nh):
            sl = slice(g * HG_DIM, (g + 1) * HG_DIM)
            out_cols = slice(half_idx * hw + g * HG_DIM, half_idx * hw + (g + 1) * HG_DIM)
            st = st_ref[half_idx * nh + g]
            for c in order:
                rs = rows[c]
                qi = (d["qm"][rs, sl] * d["e_mid"][c][:, sl]).astype(BF16)
                inter_t = lax.dot_general(st.astype(BF16), qi, NT, preferred_element_type=F32)
                o_ref[0, rs, out_cols] = (d["intra"][g, c] + inter_t.T).astype(o_ref.dtype)
                st = st * d["dec"][c][:, sl] + d["upd"][g, c]
            st_ref[half_idx * nh + g] = st

    previous = [pending(0), pending(1)]
    score_stage(previous[0])
    score_stage(previous[1])
    state_stage(0, previous[0])
    state_stage(1, previous[1])
    stash(decay_stage(0))
    stash(decay_stage(1))


def _gla_scan(P, lb, n_ctx, reverse):
    B, S, _ = P.shape
    C = HG_BLOCK
    width = HG_HEADS * HG_DIM
    gw = HG_GROUP * HG_DIM
    ngrp = width // gw
    assert n_ctx % C == 0 and S % C == 0 and C % HG_CHUNK == 0
    nch, nctx_ch = S // C, n_ctx // C
    fcol = (2 if reverse else 1) * ngrp

    def cidx(j):
        if not reverse:
            return j
        return jnp.where(j < nctx_ch, nctx_ch - 1 - j, nch + nctx_ch - 1 - j)

    def oidx(j):
        return jnp.maximum(cidx(j) - nctx_ch, 0) if not reverse else jnp.where(j < nctx_ch, nch - nctx_ch - 1,
                                                                                cidx(j) - nctx_ch)

    head = lambda j: cidx(jnp.minimum(j, nch - 1))
    tail = lambda j: jnp.maximum(j - 1, 0)
    return pl.pallas_call(
        functools.partial(_gla_kernel, reverse=reverse),
        grid=(B, ngrp, nch + 1),
        in_specs=[pl.BlockSpec((1, C, gw), lambda b, g, j: (b, head(j), g)),
                  pl.BlockSpec((1, C, gw), lambda b, g, j: (b, head(j), fcol + g)),
                  pl.BlockSpec((1, C, gw), lambda b, g, j: (b, cidx(tail(j)), 3 * ngrp + g)),
                  pl.BlockSpec((1, gw), lambda b, g, j: (0, g))],
        out_specs=pl.BlockSpec((1, C, gw), lambda b, g, j: (b, oidx(tail(j)), g)),
        out_shape=jax.ShapeDtypeStruct((B, S - n_ctx, width), BF16),
        scratch_shapes=[pltpu.VMEM((HG_GROUP, HG_DIM, HG_DIM), F32),
                        pltpu.VMEM((C, gw), F32), pltpu.VMEM((C, gw), F32),
                        pltpu.VMEM((3, 8, gw), F32)],
        compiler_params=_params("parallel", "parallel", "arbitrary"), name="gla_bwd" if reverse else "gla_fwd",
    )(P, P, P, lb)


def _merge_c_kernel(of_ref, ob_ref, z_ref, x_ref, gate_ref, hn_ref, w_ref, o_ref):
    o = of_ref[0].astype(F32) + ob_ref[0].astype(F32)
    parts = []
    for h in range(HG_HEADS):
        seg = o[:, h * HG_DIM:(h + 1) * HG_DIM]
        parts.append(seg * lax.rsqrt(jnp.mean(seg * seg, axis=1, keepdims=True) + EPS))
    y = jnp.concatenate(parts, axis=1) * hn_ref[...] * _silu(z_ref[0].astype(F32))
    o_ref[0] = x_ref[0] + gate_ref[0] * jnp.dot(y.astype(BF16), w_ref[...], preferred_element_type=F32)


def _merge_c(of, ob, P, xc, gate, h_norm, w_out, n_ctx, tm):
    B, T, width = of.shape
    D = xc.shape[2]
    off = n_ctx // tm
    return pl.pallas_call(
        _merge_c_kernel,
        grid=(B, T // tm),
        in_specs=[pl.BlockSpec((1, tm, width), lambda b, i: (b, i, 0)),
                  pl.BlockSpec((1, tm, width), lambda b, i: (b, i, 0)),
                  pl.BlockSpec((1, tm, width), lambda b, i: (b, i + off, 4)),
                  pl.BlockSpec((1, tm, D), lambda b, i: (b, i + off, 0)),
                  pl.BlockSpec((1, 1, D), lambda b, i: (b, 0, 0)),
                  pl.BlockSpec((1, width), lambda b, i: (0, 0)),
                  pl.BlockSpec((width, D), lambda b, i: (0, 0))],
        out_specs=pl.BlockSpec((1, tm, D), lambda b, i: (b, i, 0)),
        out_shape=jax.ShapeDtypeStruct((B, T, D), F32),
        compiler_params=_params("parallel", "parallel"), name="merge_c",
    )(of, ob, P, xc, gate, h_norm.reshape(1, width), w_out)


def _pad_cols(w, n):
    return jnp.pad(w, ((0, 0), (0, n - w.shape[1])))


def kernel(x, c, ctx, c_ctx, norm_w, w_ada, b_ada, w_in_ab, b_gate_ab, q_norm_a, k_norm_a, rpb_a, h_norm_b, w_out_ab,
           w_in_c, lb_c, h_norm_c, w_out_c):
    B, T, D = x.shape
    n_ctx = ctx.shape[1]
    S = n_ctx + T
    tm_in = S // 4
    tm_out = S // 16
    xc = jnp.concatenate([ctx, x], axis=1)
    cvec = jnp.concatenate([c, c_ctx[None], jnp.zeros((16 - B - 1, D), F32)], axis=0)

    def modulation(l):
        m = _ada(cvec, w_ada[l], b_ada[l])
        shift, scale, gate = m[:, :D], m[:, D:2 * D], m[:, 2 * D:]
        ctx_row = lambda a: jnp.broadcast_to(a[B][None], (B, D))
        mod = jnp.stack([ctx_row(shift), ctx_row(scale), shift[:B], scale[:B]], axis=1)
        return mod, jnp.stack([ctx_row(gate), gate[:B]], axis=1)

    mod, gate = modulation(0)
    n_main = 9 * NA_HEADS * NA_DIM
    w_main = w_in_ab[0][:, :n_main].astype(BF16)
    w_gate = _pad_cols(w_in_ab[0][:, n_main:], LANES).astype(BF16)
    b_gate = _pad_cols(b_gate_ab[0][None], LANES)
    ml_w = ML_HEADS * ML_DIM
    P = _inproj(xc, norm_w[0], mod, w_main, jnp.zeros((1, n_main), F32), n_ctx, tm_in, n_main // 3, BF16,
                rope=_rope_tables(n_ctx, T) + (1, ml_w, 2 * ml_w))
    gates = _inproj(xc, norm_w[0], mod, w_gate, b_gate, n_ctx, tm_in, LANES, F32)
    oa = _na_attention(P, _na_bias_tables(rpb_a[0], T // GRID_W), q_norm_a[0], k_norm_a[0], n_ctx)
    hf = _mlstm_scan(P, gates, n_ctx, reverse=False)
    hb = _mlstm_scan(P, gates, n_ctx, reverse=True)
    xc = _merge_ab(oa, hf, hb, P, xc, gate, h_norm_b[0], w_out_ab[0].astype(BF16), n_ctx, tm_out)

    mod, gate = modulation(1)
    sm = jax.nn.softmax(lb_c.astype(F32), axis=0)
    lb = (jnp.cumsum(sm, axis=0) - sm[0])[1].reshape(1, -1)
    hw = HG_HEADS * HG_DIM
    P = _inproj(xc, norm_w[1], mod, w_in_c[0].astype(BF16), jnp.zeros((1, 5 * hw), F32), n_ctx, tm_in, hw, BF16)
    of = _gla_scan(P, lb, n_ctx, reverse=False)
    ob = _gla_scan(P, lb, n_ctx, reverse=True)
    return _merge_c(of, ob, P, xc, gate[:, 1:2], h_norm_c[0], w_out_c[0].astype(BF16), n_ctx, n_ctx)
```

```python
import functools

import numpy as np
import jax
import jax.numpy as jnp
from jax import lax
from jax.experimental import pallas as pl
from jax.experimental.pallas import tpu as pltpu

F32 = jnp.float32
BF16 = jnp.bfloat16
HIGHEST = lax.Precision.HIGHEST

GRID_W = 64
NA_HEADS = 16
NA_DIM = 64
NA_WIN_H = 8
NA_WIN_W = 16
ML_HEADS = 4
ML_DIM = 256
HG_HEADS = 16
HG_DIM = 128
ROPE_BASE = 10000.0
EPS = 1e-6
NEG = -1e30
LOG2E = 1.4426950408889634

LANES = 128
V7X_VMEM_LIMIT = 56 * 1024 * 1024

NA_QROWS = 4
NA_KROWS = NA_QROWS + NA_WIN_H
ML_CHUNK = 256
HG_CHUNK = 64
HG_BLOCK = 256
HG_GROUP = 8

NT = (((1,), (1,)), ((), ()))
TN = (((0,), (0,)), ((), ()))


def _params(*sem):
    return pltpu.CompilerParams(dimension_semantics=sem, vmem_limit_bytes=V7X_VMEM_LIMIT)


def _sigmoid(z):
    return 0.5 * jnp.tanh(0.5 * z) + 0.5


def _silu(z):
    h = 0.5 * z
    return h + h * jnp.tanh(h)


def _log_sigmoid(z):
    return jnp.minimum(z, 0.0) - jnp.log1p(jnp.exp(-jnp.abs(z)))


def _split_bf16(x, terms):
    out = []
    for _ in range(terms - 1):
        out.append(x.astype(BF16))
        x = x - out[-1].astype(F32)
    return out + [x.astype(BF16)]


def _cumsum_rows(tri_bf16, x, terms=3):
    return sum(jnp.dot(tri_bf16, t, preferred_element_type=F32) for t in _split_bf16(x, terms))


def _cumsum_cols(x, tri_bf16, terms=3):
    return sum(lax.dot_general(t, tri_bf16, NT, preferred_element_type=F32) for t in _split_bf16(x, terms))


def _ada_kernel(c_ref, w_ref, b_ref, o_ref):
    s = _silu(c_ref[...])
    o_ref[...] = jnp.dot(s.astype(BF16), w_ref[...].astype(BF16), preferred_element_type=F32) + b_ref[...]


def _ada(cvec, w, b):
    R, D = cvec.shape
    N = w.shape[1]
    tn = D
    return pl.pallas_call(
        _ada_kernel, grid=(N // tn,),
        in_specs=[pl.BlockSpec((R, D), lambda n: (0, 0)),
                  pl.BlockSpec((D, tn), lambda n: (0, n)),
                  pl.BlockSpec((1, tn), lambda n: (0, n))],
        out_specs=pl.BlockSpec((R, tn), lambda n: (0, n)),
        out_shape=jax.ShapeDtypeStruct((R, N), F32),
        compiler_params=_params("arbitrary"), name="ada",
    )(cvec, w, b.reshape(1, N))


def _inproj_kernel(x_ref, nw_ref, mod_ref, w_ref, bias_ref, *rest, n_ctx, rope_cols):
    if rope_cols is None:
        o_ref, hn_ref = rest
    else:
        cos_ref, sin_ref, ws_ref, bs_ref, o_ref, side_ref, hn_ref = rest
    i = pl.program_id(1)
    n = pl.program_id(2)
    tm = x_ref.shape[1]

    def normed():
        x = x_ref[0]
        y = x * lax.rsqrt(jnp.mean(x * x, axis=-1, keepdims=True) + EPS) * nw_ref[...]
        row = i * tm + lax.broadcasted_iota(jnp.int32, (tm, 1), 0)
        is_ctx = row < n_ctx
        shift = jnp.where(is_ctx, mod_ref[0, 0:1, :], mod_ref[0, 2:3, :])
        scale = jnp.where(is_ctx, mod_ref[0, 1:2, :], mod_ref[0, 3:4, :])
        return (y * (1.0 + scale) + shift).astype(BF16)

    def project(h, rotate):
        acc = jnp.dot(h, w_ref[...], preferred_element_type=F32) + bias_ref[...]
        if not rotate:
            o_ref[0] = acc.astype(o_ref.dtype)
            return
        _, q_col, k_col = rope_cols
        ml_w = ML_HEADS * ML_DIM
        cos, sin = cos_ref[...], sin_ref[...]
        for c in range(0, acc.shape[1], ML_DIM):
            cs = slice(c, c + ML_DIM)
            if q_col <= c < q_col + ml_w:
                o_ref[0, :, cs] = _rope(acc[:, cs], cos, sin).astype(o_ref.dtype)
            elif k_col <= c < k_col + ml_w:
                o_ref[0, :, cs] = (_rope(acc[:, cs], cos, sin) * (ML_DIM ** -0.5)).astype(o_ref.dtype)
            else:
                o_ref[0, :, cs] = acc[:, cs].astype(o_ref.dtype)

    rope_tile = None if rope_cols is None else rope_cols[0]

    @pl.when(n == 0)
    def _():
        h = normed()
        hn_ref[...] = h
        project(h, rope_tile == 0)
        if rope_cols is not None:
            side_ref[0] = jnp.dot(h, ws_ref[...], preferred_element_type=F32) + bs_ref[...]

    if rope_tile not in (None, 0):
        @pl.when(n == rope_tile)
        def _():
            project(hn_ref[...], True)

    @pl.when((n != 0) if rope_tile in (None, 0) else ((n != 0) & (n != rope_tile)))
    def _():
        project(hn_ref[...], False)


def _inproj(xc, norm_w, mod, w, bias, n_ctx, tm, tn, out_dtype, rope=None, side=None):
    B, S, D = xc.shape
    N = w.shape[1]
    in_specs = [pl.BlockSpec((1, tm, D), lambda b, i, n: (b, i, 0)),
                pl.BlockSpec((1, D), lambda b, i, n: (0, 0)),
                pl.BlockSpec((1, 4, D), lambda b, i, n: (b, 0, 0)),
                pl.BlockSpec((D, tn), lambda b, i, n: (0, n)),
                pl.BlockSpec((1, tn), lambda b, i, n: (0, n))]
    args = [xc, norm_w.reshape(1, D), mod, w, bias]
    out_specs = pl.BlockSpec((1, tm, tn), lambda b, i, n: (b, i, n))
    out_shape = jax.ShapeDtypeStruct((B, S, N), out_dtype)
    if rope is not None:
        in_specs += [pl.BlockSpec((tm, ML_DIM), lambda b, i, n: (i, 0))] * 2
        in_specs += [pl.BlockSpec((D, LANES), lambda b, i, n: (0, 0)), pl.BlockSpec((1, LANES), lambda b, i, n: (0, 0))]
        args += list(rope[:2]) + list(side)
        out_specs = (out_specs, pl.BlockSpec((1, tm, LANES), lambda b, i, n: (b, i, 0)))
        out_shape = (out_shape, jax.ShapeDtypeStruct((B, S, LANES), F32))
    return pl.pallas_call(
        functools.partial(_inproj_kernel, n_ctx=n_ctx, rope_cols=None if rope is None else tuple(rope[2:])),
        grid=(B, S // tm, N // tn),
        in_specs=in_specs,
        out_specs=out_specs,
        out_shape=out_shape,
        scratch_shapes=[pltpu.VMEM((tm, D), BF16)],
        compiler_params=_params("parallel", "parallel", "arbitrary"), name="inproj",
    )(*args)


def _na_bias_tables(rpb, rows):
    H, nri, nci = rpb.shape
    qc, kc = np.arange(GRID_W)[:, None], np.arange(GRID_W)[None, :]
    c0 = np.clip(qc - NA_WIN_W // 2, 0, GRID_W - NA_WIN_W)
    col_ok = (kc >= c0) & (kc < c0 + NA_WIN_W)
    col_sel = ((kc - qc + NA_WIN_W - 1)[..., None] == np.arange(nci)) & col_ok[..., None]
    sel = np.einsum("de,qkj->qdkej", np.eye(2), col_sel).reshape(GRID_W, 2, GRID_W, 2 * nci).astype(np.float32)
    rp = jnp.pad(rpb.astype(F32), ((0, 0), (1, 1), (0, 0)))
    pair = jnp.stack([rp[:, :nri + 1], rp[:, 1:]], axis=2).reshape(H, nri + 1, 2 * nci)
    tiles = jnp.einsum("hic,qdkc->hiqdk", pair, sel, precision=HIGHEST).reshape(H, nri + 1, GRID_W, 2 * GRID_W)

    qr, ka = np.arange(NA_QROWS)[:, None], np.arange(NA_KROWS)[None, :]
    nblk = rows // NA_QROWS
    tables = []
    for blk in (0, 1, nblk - 1):
        kr0 = int(np.clip(NA_QROWS * blk - NA_WIN_H // 2, 0, rows - NA_KROWS))
        r = NA_QROWS * blk + qr
        r0 = np.clip(r - NA_WIN_H // 2, 0, rows - NA_WIN_H)
        krow = kr0 + ka
        row_ok = (krow >= r0) & (krow < r0 + NA_WIN_H)
        idx = np.clip((krow - r + NA_WIN_H - 1)[:, 0::2] + 1, 0, nri)
        valid = (row_ok.reshape(NA_QROWS, NA_KROWS // 2, 1, 2, 1) & col_ok[None, None, :, None, :])
        valid = valid.reshape(NA_QROWS, NA_KROWS // 2, GRID_W, 2 * GRID_W)
        tables.append(jnp.where(valid[None], tiles[:, idx] * LOG2E, NEG))
    return jnp.stack(tables)


def _na_kernel(q_ref, k_ref, v_ref, bias_ref, qn_ref, kn_ref, ones_ref, o_ref, kt_s, q0_s, q1_s, v0_s, v1_s,
               la0_s, la1_s, lb0_s, lb1_s, *, n_ctx, rows):
    la_s, lb_s = (la0_s, la1_s), (lb0_s, lb1_s)
    nq = NA_QROWS * GRID_W
    win = NA_KROWS * GRID_W
    nblk = rows // NA_QROWS
    lane = lax.broadcasted_iota(jnp.int32, (1, LANES), 1)
    in_h0 = lane < NA_DIM
    qh_s, vh_s = (q0_s, q1_s), (v0_s, v1_s)
    den_lane = (NA_DIM, 0)

    def rms(x, w):
        ss = jnp.dot((x * x).astype(BF16), ones_ref[...], preferred_element_type=F32)
        return x * lax.rsqrt(ss * (1.0 / NA_DIM) + EPS) * w

    kt_s[...] = rms(k_ref[0].astype(F32), kn_ref[...]).T.astype(BF16)
    q = rms(q_ref[0].astype(F32), qn_ref[...]) * (NA_DIM ** -0.5 * LOG2E)
    q0_s[...] = jnp.where(in_h0, q, 0.0).astype(BF16)
    q1_s[...] = jnp.where(in_h0, 0.0, q).astype(BF16)
    v = v_ref[0].astype(F32)
    v0_s[...] = jnp.where(in_h0, v, jnp.where(lane == den_lane[0], 1.0, 0.0)).astype(BF16)
    v1_s[...] = jnp.where(in_h0, jnp.where(lane == den_lane[1], 1.0, 0.0), v).astype(BF16)

    def softmax_pv(h, s_parts, bias, v_starts):
        if bias is not None:
            s_parts = [s_parts[0] + bias] + s_parts[1:]
        m = functools.reduce(jnp.maximum, [jnp.max(s, axis=1, keepdims=True) for s in s_parts])
        return sum(jnp.dot(jnp.exp2(s - m).astype(BF16), vh_s[h][pl.ds(st, s.shape[1]), :],
                           preferred_element_type=F32) for s, st in zip(s_parts, v_starts))

    def store(q_start, o):
        den = [o[h][:, den_lane[h]:den_lane[h] + 1] for h in range(2)]
        o_ref[0, pl.ds(q_start, nq), :] = jnp.where(in_h0, o[0] / den[0], o[1] / den[1]).astype(o_ref.dtype)

    kc = kt_s[:, 0:n_ctx]
    s_ctx = [jnp.dot(qh_s[h][0:n_ctx, :], kc, preferred_element_type=F32) for h in range(2)]
    store(0, [softmax_pv(h, [s_ctx[h]], None, [0]) for h in range(2)])

    def block_pos(blk):
        q_start = pl.multiple_of(n_ctx + nq * blk, nq)
        kr0 = jnp.clip(NA_QROWS * blk - NA_WIN_H // 2, 0, rows - NA_KROWS)
        return q_start, pl.multiple_of(n_ctx + GRID_W * kr0, nq)

    def logits_to(bufs, blk):
        q_start, k_start = block_pos(blk)
        for h in range(2):
            qh = qh_s[h][pl.ds(q_start, nq), :]
            bufs[h][:, 0:win] = jnp.dot(qh, kt_s[:, pl.ds(k_start, win)], preferred_element_type=F32)
            bufs[h][:, win:win + n_ctx] = jnp.dot(qh, kc, preferred_element_type=F32)

    def attend_from(bufs, blk):
        q_start, k_start = block_pos(blk)
        case = jnp.where(blk == 0, 0, jnp.where(blk == nblk - 1, 2, 1))
        o = []
        for h in range(2):
            tiles = bias_ref[case, h]
            bias = jnp.concatenate([jnp.concatenate([tiles[r, a] for a in range(NA_KROWS // 2)], axis=1)
                                    for r in range(NA_QROWS)], axis=0)
            o.append(softmax_pv(h, [bufs[h][:, 0:win], bufs[h][:, win:win + n_ctx]], bias, [k_start, 0]))
        store(q_start, o)

    logits_to(la_s, 0)

    def two_blocks(it, carry):
        b0 = 2 * it
        logits_to(lb_s, b0 + 1)
        attend_from(la_s, b0)
        logits_to(la_s, jnp.minimum(b0 + 2, nblk - 1))
        attend_from(lb_s, b0 + 1)
        return carry

    lax.fori_loop(0, nblk // 2, two_blocks, 0)


def _na_attention(P, bias_tab, q_norm, k_norm, n_ctx):
    B, S, _ = P.shape
    width = NA_HEADS * NA_DIM
    npair = width // LANES
    nq = NA_QROWS * GRID_W
    rows = (S - n_ctx) // GRID_W
    assert n_ctx == nq and rows % (2 * NA_QROWS) == 0
    win = NA_KROWS * GRID_W
    hp = LANES // NA_DIM
    ones = jnp.asarray(np.kron(np.eye(hp), np.ones((NA_DIM, NA_DIM))), BF16)
    qn = jnp.tile(q_norm, hp).reshape(1, LANES)
    kn = jnp.tile(k_norm, hp).reshape(1, LANES)
    seq = lambda c: pl.BlockSpec((1, S, LANES), lambda b, p: (b, 0, c * npair + p))
    vec = pl.BlockSpec((1, LANES), lambda b, p: (0, 0))
    return pl.pallas_call(
        functools.partial(_na_kernel, n_ctx=n_ctx, rows=rows),
        grid=(B, npair),
        in_specs=[seq(0), seq(1), seq(2),
                  pl.BlockSpec((bias_tab.shape[0], hp) + bias_tab.shape[2:], lambda b, p: (0, p, 0, 0, 0, 0)),
                  vec, vec, pl.BlockSpec((LANES, LANES), lambda b, p: (0, 0))],
        out_specs=seq(0),
        out_shape=jax.ShapeDtypeStruct((B, S, width), BF16),
        scratch_shapes=([pltpu.VMEM((LANES, S), BF16)] + [pltpu.VMEM((S, LANES), BF16)] * 4
                        + [pltpu.VMEM((nq, win + n_ctx), F32)] * 4),
        compiler_params=_params("parallel", "parallel"), name="na_attention",
    )(P, P, P, bias_tab, qn, kn, ones)


def _rope_tables(n_ctx, T):
    half = ML_DIM // 4
    t = jnp.arange(T)
    freqs = ROPE_BASE ** (-jnp.arange(half, dtype=F32) / half)
    ang_r = (t // GRID_W).astype(F32)[:, None] * freqs[None, :]
    ang_c = (t % GRID_W).astype(F32)[:, None] * freqs[None, :]
    cos = jnp.concatenate([jnp.cos(ang_r)] * 2 + [jnp.cos(ang_c)] * 2, axis=-1)
    sin = jnp.concatenate([-jnp.sin(ang_r), jnp.sin(ang_r), -jnp.sin(ang_c), jnp.sin(ang_c)], axis=-1)
    cos = jnp.concatenate([jnp.ones((n_ctx, ML_DIM), F32), cos], axis=0)
    sin = jnp.concatenate([jnp.zeros((n_ctx, ML_DIM), F32), sin], axis=0)
    return cos, sin


def _rope(x, cos, sin):
    xr = jnp.concatenate([pltpu.roll(x[:, a * LANES:(a + 1) * LANES], LANES // 2, axis=1)
                          for a in range(x.shape[1] // LANES)], axis=1)
    return x * cos + xr * sin


def _mlstm_kernel(q_ref, k_ref, v_ref, g_ref, o_ref, c_ref, n_ref, m_ref, *, reverse):
    L = q_ref.shape[1]

    @pl.when(pl.program_id(1) == 0)
    def _():
        c_ref[...] = jnp.zeros_like(c_ref)
        n_ref[...] = jnp.zeros_like(n_ref)
        m_ref[...] = jnp.zeros_like(m_ref)

    t_idx = lax.broadcasted_iota(jnp.int32, (L, L), 0)
    s_idx = lax.broadcasted_iota(jnp.int32, (L, L), 1)
    mask = (s_idx >= t_idx) if reverse else (s_idx <= t_idx)
    tri = mask.astype(BF16)
    last = 0 if reverse else L - 1

    g = g_ref[0]
    lsg = _log_sigmoid(g)
    g_t = g.T
    b_cols = _cumsum_rows(tri, lsg)
    b_rows = _cumsum_cols(lsg.T, tri)
    ones = jnp.ones((L, LANES), BF16)

    heads = range(ML_HEADS)
    hs = [slice(h * ML_DIM, (h + 1) * ML_DIM) for h in heads]
    qb = [q_ref[0, :, hs[h]] for h in heads]
    kb = [k_ref[0, :, hs[h]] for h in heads]
    vb = [v_ref[0, :, hs[h]] for h in heads]
    qk = [lax.dot_general(qb[h], kb[h], NT, preferred_element_type=F32) for h in heads]
    qc = [jnp.dot(qb[h], c_ref[h].astype(BF16), preferred_element_type=F32) for h in heads]
    qn = [jnp.dot(qb[h], n_ref[h].astype(BF16), preferred_element_type=F32) for h in heads]
    kwb, decay, m_t, carry_w, m_new, dec = ([None] * ML_HEADS for _ in range(6))
    for h in heads:
        ii = (2 if reverse else 0) * ML_HEADS + h
        fi = (3 if reverse else 1) * ML_HEADS + h
        ig_col, ig_row = g[:, ii:ii + 1], g_t[ii:ii + 1, :]
        b_col, b_row = b_cols[:, fi:fi + 1], b_rows[fi:fi + 1, :]
        b_last = b_col[last:last + 1, :]
        m_prev = m_ref[h, 0:1, 0:1]
        dmat = jnp.where(mask, b_col - b_row + ig_row, NEG)
        m_inter = b_col + m_prev
        m_t[h] = jnp.maximum(jnp.max(dmat, axis=1, keepdims=True), m_inter)
        decay[h] = jnp.exp(dmat - m_t[h])
        carry_w[h] = jnp.broadcast_to(jnp.exp(m_inter - m_t[h]), (L, LANES))
        g_col = b_last - b_col + ig_col
        m_new[h] = jnp.maximum(b_last + m_prev, jnp.max(g_col, axis=0, keepdims=True))
        kwb[h] = (kb[h].astype(F32) * jnp.exp(g_col - m_new[h])).astype(BF16)
        dec[h] = jnp.exp(b_last + m_prev - m_new[h])

    upd = [lax.dot_general(kwb[h], vb[h], TN, preferred_element_type=F32) for h in heads]
    kw_sum = [lax.dot_general(kwb[h], ones, TN, preferred_element_type=F32) for h in heads]
    sb = [(qk[h] * decay[h]).astype(BF16) for h in heads]
    sv = [jnp.dot(sb[h], vb[h], preferred_element_type=F32) for h in heads]
    s_sum = [jnp.dot(sb[h], ones, preferred_element_type=F32) for h in heads]
    for h in heads:
        den = s_sum[h] + carry_w[h] * qn[h]
        inv = 1.0 / jnp.maximum(jnp.abs(den), jnp.exp(-m_t[h]))
        num = sv[h] + jnp.concatenate([carry_w[h]] * (ML_DIM // LANES), axis=1) * qc[h]
        o_ref[0, :, hs[h]] = (num * jnp.concatenate([inv] * (ML_DIM // LANES), axis=1)).astype(o_ref.dtype)
        c_ref[h] = dec[h] * c_ref[h] + upd[h]
        n_ref[h] = dec[h] * n_ref[h] + kw_sum[h]
        m_ref[h] = jnp.broadcast_to(m_new[h], m_ref.shape[1:])


def _mlstm_scan(P, gates, n_ctx, reverse):
    B, S, _ = P.shape
    L = ML_CHUNK
    width = ML_HEADS * ML_DIM
    assert n_ctx == L and S % L == 0
    nch = S // L

    def cidx(j):
        return jnp.where(j == 0, 0, nch - j) if reverse else j

    return pl.pallas_call(
        functools.partial(_mlstm_kernel, reverse=reverse),
        grid=(B, nch),
        in_specs=[pl.BlockSpec((1, L, width), lambda b, j: (b, cidx(j), 4)),
                  pl.BlockSpec((1, L, width), lambda b, j: (b, cidx(j), 5)),
                  pl.BlockSpec((1, L, width), lambda b, j: (b, cidx(j), 6)),
                  pl.BlockSpec((1, L, LANES), lambda b, j: (b, cidx(j), 0))],
        out_specs=pl.BlockSpec((1, L, width), lambda b, j: (b, cidx(j), 0)),
        out_shape=jax.ShapeDtypeStruct((B, S, width), BF16),
        scratch_shapes=[pltpu.VMEM((ML_HEADS, ML_DIM, ML_DIM), F32),
                        pltpu.VMEM((ML_HEADS, ML_DIM, LANES), F32),
                        pltpu.VMEM((ML_HEADS, 8, LANES), F32)],
        compiler_params=_params("parallel", "arbitrary"), name="mlstm_bwd" if reverse else "mlstm_fwd",
    )(P, P, P, gates)


def _merge_ab_kernel(oa_ref, za_ref, hf_ref, hb_ref, ob_ref, zb_ref, x_ref, gate_ref, hn_ref, w_ref, o_ref, *, n_ctx):
    tm = x_ref.shape[1]
    width = oa_ref.shape[2]
    ya = oa_ref[0] * _silu(za_ref[0])
    hb = (hf_ref[0].astype(F32) + hb_ref[0].astype(F32)) * _sigmoid(ob_ref[0]).astype(F32)
    parts = []
    for h in range(ML_HEADS):
        seg = hb[:, h * ML_DIM:(h + 1) * ML_DIM]
        parts.append(seg * lax.rsqrt(jnp.mean(seg * seg, axis=1, keepdims=True) + EPS))
    yb = jnp.concatenate(parts, axis=1) * hn_ref[...] * _silu(zb_ref[0]).astype(F32)
    y = (jnp.dot(ya, w_ref[0:width, :], preferred_element_type=F32)
         + jnp.dot(yb.astype(BF16), w_ref[width:2 * width, :], preferred_element_type=F32))
    row = pl.program_id(1) * tm + lax.broadcasted_iota(jnp.int32, (tm, 1), 0)
    gate = jnp.where(row < n_ctx, gate_ref[0, 0:1, :], gate_ref[0, 1:2, :])
    o_ref[0] = x_ref[0] + gate * y


def _merge_ab(oa, hf, hb, P, xc, gate, h_norm, w_out, n_ctx, tm):
    B, S, D = xc.shape
    width = oa.shape[2]
    tok = lambda c: pl.BlockSpec((1, tm, width), lambda b, i: (b, i, c))
    return pl.pallas_call(
        functools.partial(_merge_ab_kernel, n_ctx=n_ctx),
        grid=(B, S // tm),
        in_specs=[tok(0), tok(3), tok(0), tok(0), tok(7), tok(8),
                  pl.BlockSpec((1, tm, D), lambda b, i: (b, i, 0)),
                  pl.BlockSpec((1, 2, D), lambda b, i: (b, 0, 0)),
                  pl.BlockSpec((1, width), lambda b, i: (0, 0)),
                  pl.BlockSpec((2 * width, D), lambda b, i: (0, 0))],
        out_specs=pl.BlockSpec((1, tm, D), lambda b, i: (b, i, 0)),
        out_shape=jax.ShapeDtypeStruct((B, S, D), F32),
        compiler_params=_params("parallel", "parallel"), name="merge_ab",
    )(oa, P, hf, hb, P, P, xc, gate, h_norm.reshape(1, width), w_out)


def _gla_kernel(q_ref, f_ref, v_ref, lb_ref, o_ref, st_ref, qm_s, km_s, ev_s, *, reverse):
    C = HG_CHUNK
    T = q_ref.shape[1]
    nsub = T // C
    gw = q_ref.shape[2]

    @pl.when(pl.program_id(2) == 0)
    def _():
        st_ref[...] = jnp.zeros_like(st_ref)
        qm_s[...] = jnp.zeros_like(qm_s)
        km_s[...] = jnp.zeros_like(km_s)
        ev_s[...] = jnp.zeros_like(ev_s)

    def causal(n, same_chunk):
        t_idx = lax.broadcasted_iota(jnp.int32, (n, n), 0)
        s_idx = lax.broadcasted_iota(jnp.int32, (n, n), 1)
        m = (s_idx >= t_idx) if reverse else (s_idx <= t_idx)
        return m & (t_idx // C == s_idx // C) if same_chunk else m

    mask = causal(C, False)
    tri = causal(T, True).astype(BF16)
    last = 0 if reverse else C - 1
    mid = C // 2 if reverse else C // 2 - 1
    order = list(range(nsub - 1, -1, -1) if reverse else range(nsub))
    rows = [slice(c * C, (c + 1) * C) for c in range(nsub)]
    hw = gw // 2
    nh = hw // HG_DIM

    def decay_stage(half_idx):
        cols = slice(half_idx * hw, (half_idx + 1) * hw)
        lb = lb_ref[:, cols]
        c1 = 0.5 * (1.0 - lb)
        f = (lb + c1) + c1 * jnp.tanh(0.5 * f_ref[0, :, cols].astype(F32))
        b = _cumsum_rows(tri, jnp.log(f), terms=2)
        b_mid = [b[c * C + mid:c * C + mid + 1, :] for c in range(nsub)]
        b_last = [b[c * C + last:c * C + last + 1, :] for c in range(nsub)]
        b_mid_rows = jnp.concatenate([jnp.broadcast_to(r, (C, hw)) for r in b_mid], axis=0)
        qm = _silu(q_ref[0, :, cols].astype(F32)) * jnp.exp(b - b_mid_rows)
        km = (1.0 - f) * jnp.exp(b_mid_rows - b)
        return dict(cols=cols, qm=qm, km=km,
                    e_mid=[jnp.exp(r) for r in b_mid],
                    e_last_mid=[jnp.exp(l - r) for l, r in zip(b_last, b_mid)],
                    dec=[jnp.exp(l) for l in b_last])

    ev_names = ("e_mid", "e_last_mid", "dec")

    def pending(half_idx):
        cols = slice(half_idx * hw, (half_idx + 1) * hw)
        d = dict(cols=cols, qm=qm_s[:, cols], km=km_s[:, cols], vb=v_ref[0, :, cols])
        for n, name in enumerate(ev_names):
            d[name] = [ev_s[n, c:c + 1, cols] for c in range(nsub)]
        return d

    def stash(d):
        qm_s[:, d["cols"]] = d["qm"]
        km_s[:, d["cols"]] = d["km"]
        for n, name in enumerate(ev_names):
            for c in range(nsub):
                ev_s[n, c:c + 1, d["cols"]] = d[name][c]

    def score_stage(d):
        qmb, kmb, vb = d["qm"].astype(BF16), d["km"].astype(BF16), d["vb"]
        a, upd = {}, {}
        for g in range(nh):
            sl = slice(g * HG_DIM, (g + 1) * HG_DIM)
            for c, rs in enumerate(rows):
                s = lax.dot_general(qmb[rs, sl], kmb[rs, sl], NT, preferred_element_type=F32)
                a[g, c] = jnp.where(mask, s, 0.0).astype(BF16)
                kh = (d["km"][rs, sl] * d["e_last_mid"][c][:, sl]).astype(BF16)
                upd[g, c] = lax.dot_general(vb[rs, sl], kh, TN, preferred_element_type=F32)
        d["upd"] = upd
        d["intra"] = {(g, c): jnp.dot(a[g, c], vb[rows[c], g * HG_DIM:(g + 1) * HG_DIM],
                                      preferred_element_type=F32) for (g, c) in a}

    def state_stage(half_idx, d):
        for g in range(nh):
            sl = slice(g * HG_DIM, (g + 1) * HG_DIM)
            out_cols = slice(half_idx * hw + g * HG_DIM, half_idx * hw + (g + 1) * HG_DIM)
            st = st_ref[half_idx * nh + g]
            for c in order:
                rs = rows[c]
                qi = (d["qm"][rs, sl] * d["e_mid"][c][:, sl]).astype(BF16)
                inter_t = lax.dot_general(st.astype(BF16), qi, NT, preferred_element_type=F32)
                o_ref[0, rs, out_cols] = (d["intra"][g, c] + inter_t.T).astype(o_ref.dtype)
                st = st * d["dec"][c][:, sl] + d["upd"][g, c]
            st_ref[half_idx * nh + g] = st

    previous = [pending(0), pending(1)]
    score_stage(previous[0])
    score_stage(previous[1])
    state_stage(0, previous[0])
    state_stage(1, previous[1])
    stash(decay_stage(0))
    stash(decay_stage(1))


def _gla_scan(P, lb, n_ctx, reverse):
    B, S, _ = P.shape
    C = HG_BLOCK
    width = HG_HEADS * HG_DIM
    gw = HG_GROUP * HG_DIM
    ngrp = width // gw
    assert n_ctx % C == 0 and S % C == 0 and C % HG_CHUNK == 0
    nch, nctx_ch = S // C, n_ctx // C
    fcol = (2 if reverse else 1) * ngrp

    def cidx(j):
        if not reverse:
            return j
        return jnp.where(j < nctx_ch, nctx_ch - 1 - j, nch + nctx_ch - 1 - j)

    def oidx(j):
        return jnp.maximum(cidx(j) - nctx_ch, 0) if not reverse else jnp.where(j < nctx_ch, nch - nctx_ch - 1,
                                                                                cidx(j) - nctx_ch)

    head = lambda j: cidx(jnp.minimum(j, nch - 1))
    tail = lambda j: jnp.maximum(j - 1, 0)
    return pl.pallas_call(
        functools.partial(_gla_kernel, reverse=reverse),
        grid=(B, ngrp, nch + 1),
        in_specs=[pl.BlockSpec((1, C, gw), lambda b, g, j: (b, head(j), g)),
                  pl.BlockSpec((1, C, gw), lambda b, g, j: (b, head(j), fcol + g)),
                  pl.BlockSpec((1, C, gw), lambda b, g, j: (b, cidx(tail(j)), 3 * ngrp + g)),
                  pl.BlockSpec((1, gw), lambda b, g, j: (0, g))],
        out_specs=pl.BlockSpec((1, C, gw), lambda b, g, j: (b, oidx(tail(j)), g)),
        out_shape=jax.ShapeDtypeStruct((B, S - n_ctx, width), BF16),
        scratch_shapes=[pltpu.VMEM((HG_GROUP, HG_DIM, HG_DIM), F32),
                        pltpu.VMEM((C, gw), F32), pltpu.VMEM((C, gw), F32),
                        pltpu.VMEM((3, 8, gw), F32)],
        compiler_params=_params("parallel", "parallel", "arbitrary"), name="gla_bwd" if reverse else "gla_fwd",
    )(P, P, P, lb)


def _merge_c_kernel(of_ref, ob_ref, z_ref, x_ref, gate_ref, hn_ref, w_ref, o_ref):
    o = of_ref[0].astype(F32) + ob_ref[0].astype(F32)
    parts = []
    for h in range(HG_HEADS):
        seg = o[:, h * HG_DIM:(h + 1) * HG_DIM]
        parts.append(seg * lax.rsqrt(jnp.mean(seg * seg, axis=1, keepdims=True) + EPS))
    y = jnp.concatenate(parts, axis=1) * hn_ref[...] * _silu(z_ref[0]).astype(F32)
    o_ref[0] = x_ref[0] + gate_ref[0] * jnp.dot(y.astype(BF16), w_ref[...], preferred_element_type=F32)


def _merge_c(of, ob, P, xc, gate, h_norm, w_out, n_ctx, tm):
    B, T, width = of.shape
    D = xc.shape[2]
    off = n_ctx // tm
    return pl.pallas_call(
        _merge_c_kernel,
        grid=(B, T // tm),
        in_specs=[pl.BlockSpec((1, tm, width), lambda b, i: (b, i, 0)),
                  pl.BlockSpec((1, tm, width), lambda b, i: (b, i, 0)),
                  pl.BlockSpec((1, tm, width), lambda b, i: (b, i + off, 4)),
                  pl.BlockSpec((1, tm, D), lambda b, i: (b, i + off, 0)),
                  pl.BlockSpec((1, 1, D), lambda b, i: (b, 0, 0)),
                  pl.BlockSpec((1, width), lambda b, i: (0, 0)),
                  pl.BlockSpec((width, D), lambda b, i: (0, 0))],
        out_specs=pl.BlockSpec((1, tm, D), lambda b, i: (b, i, 0)),
        out_shape=jax.ShapeDtypeStruct((B, T, D), F32),
        compiler_params=_params("parallel", "parallel"), name="merge_c",
    )(of, ob, P, xc, gate, h_norm.reshape(1, width), w_out)


def _pad_cols(w, n):
    return jnp.pad(w, ((0, 0), (0, n - w.shape[1])))


def kernel(x, c, ctx, c_ctx, norm_w, w_ada, b_ada, w_in_ab, b_gate_ab, q_norm_a, k_norm_a, rpb_a, h_norm_b, w_out_ab,
           w_in_c, lb_c, h_norm_c, w_out_c):
    B, T, D = x.shape
    n_ctx = ctx.shape[1]
    S = n_ctx + T
    tm_in = S // 4
    tm_out = S // 16
    xc = jnp.concatenate([ctx, x], axis=1)
    cvec = jnp.concatenate([c, c_ctx[None], jnp.zeros((16 - B - 1, D), F32)], axis=0)

    def modulation(l):
        m = _ada(cvec, w_ada[l], b_ada[l])
        shift, scale, gate = m[:, :D], m[:, D:2 * D], m[:, 2 * D:]
        ctx_row = lambda a: jnp.broadcast_to(a[B][None], (B, D))
        mod = jnp.stack([ctx_row(shift), ctx_row(scale), shift[:B], scale[:B]], axis=1)
        return mod, jnp.stack([ctx_row(gate), gate[:B]], axis=1)

    mod, gate = modulation(0)
    n_main = 9 * NA_HEADS * NA_DIM
    w_main = w_in_ab[0][:, :n_main].astype(BF16)
    w_gate = _pad_cols(w_in_ab[0][:, n_main:], LANES).astype(BF16)
    b_gate = _pad_cols(b_gate_ab[0][None], LANES)
    ml_w = ML_HEADS * ML_DIM
    P, gates = _inproj(xc, norm_w[0], mod, w_main, jnp.zeros((1, n_main), F32), n_ctx, tm_in, n_main // 3, BF16,
                       rope=_rope_tables(n_ctx, T) + (1, ml_w, 2 * ml_w), side=(w_gate, b_gate))
    oa = _na_attention(P, _na_bias_tables(rpb_a[0], T // GRID_W), q_norm_a[0], k_norm_a[0], n_ctx)
    hf = _mlstm_scan(P, gates, n_ctx, reverse=False)
    hb = _mlstm_scan(P, gates, n_ctx, reverse=True)
    xc = _merge_ab(oa, hf, hb, P, xc, gate, h_norm_b[0], w_out_ab[0].astype(BF16), n_ctx, tm_out)

    mod, gate = modulation(1)
    sm = jax.nn.softmax(lb_c.astype(F32), axis=0)
    lb = (jnp.cumsum(sm, axis=0) - sm[0])[1].reshape(1, -1)
    hw = HG_HEADS * HG_DIM
    P = _inproj(xc, norm_w[1], mod, w_in_c[0].astype(BF16), jnp.zeros((1, 5 * hw), F32), n_ctx, tm_in, hw, BF16)
    of = _gla_scan(P, lb, n_ctx, reverse=False)
    ob = _gla_scan(P, lb, n_ctx, reverse=True)
    return _merge_c(of, ob, P, xc, gate[:, 1:2], h_norm_c[0], w_out_c[0].astype(BF16), n_ctx, n_ctx)
```

```python
import functools

import numpy as np
import jax
import jax.numpy as jnp
from jax import lax
from jax.experimental import pallas as pl
from jax.experimental.pallas import tpu as pltpu

F32 = jnp.float32
BF16 = jnp.bfloat16
HIGHEST = lax.Precision.HIGHEST

GRID_W = 64
NA_HEADS = 16
NA_DIM = 64
NA_WIN_H = 8
NA_WIN_W = 16
ML_HEADS = 4
ML_DIM = 256
HG_HEADS = 16
HG_DIM = 128
ROPE_BASE = 10000.0
EPS = 1e-6
NEG = -1e30
LOG2E = 1.4426950408889634

LANES = 128
V7X_VMEM_LIMIT = 56 * 1024 * 1024

NA_QROWS = 4
NA_KROWS = NA_QROWS + NA_WIN_H
ML_CHUNK = 256
HG_CHUNK = 64
HG_BLOCK = 256
HG_GROUP = 16

NT = (((1,), (1,)), ((), ()))
TN = (((0,), (0,)), ((), ()))


def _params(*sem):
    return pltpu.CompilerParams(dimension_semantics=sem, vmem_limit_bytes=V7X_VMEM_LIMIT)


def _sigmoid(z):
    return 0.5 * jnp.tanh(0.5 * z) + 0.5


def _silu(z):
    h = 0.5 * z
    return h + h * jnp.tanh(h)


def _log_sigmoid(z):
    return jnp.minimum(z, 0.0) - jnp.log1p(jnp.exp(-jnp.abs(z)))


def _split_bf16(x, terms):
    out = []
    for _ in range(terms - 1):
        out.append(x.astype(BF16))
        x = x - out[-1].astype(F32)
    return out + [x.astype(BF16)]


def _cumsum_rows(tri_bf16, x, terms=3):
    return sum(jnp.dot(tri_bf16, t, preferred_element_type=F32) for t in _split_bf16(x, terms))


def _cumsum_cols(x, tri_bf16, terms=3):
    return sum(lax.dot_general(t, tri_bf16, NT, preferred_element_type=F32) for t in _split_bf16(x, terms))


def _ada_kernel(c_ref, w_ref, b_ref, o_ref):
    s = _silu(c_ref[...])
    o_ref[...] = jnp.dot(s.astype(BF16), w_ref[...].astype(BF16), preferred_element_type=F32) + b_ref[...]


def _ada(cvec, w, b):
    R, D = cvec.shape
    N = w.shape[1]
    tn = D
    return pl.pallas_call(
        _ada_kernel, grid=(N // tn,),
        in_specs=[pl.BlockSpec((R, D), lambda n: (0, 0)),
                  pl.BlockSpec((D, tn), lambda n: (0, n)),
                  pl.BlockSpec((1, tn), lambda n: (0, n))],
        out_specs=pl.BlockSpec((R, tn), lambda n: (0, n)),
        out_shape=jax.ShapeDtypeStruct((R, N), F32),
        compiler_params=_params("arbitrary"), name="ada",
    )(cvec, w, b.reshape(1, N))


def _inproj_kernel(x_ref, nw_ref, mod_ref, w_ref, bias_ref, *rest, n_ctx, rope_cols):
    if rope_cols is None:
        o_ref, hn_ref = rest
    else:
        cos_ref, sin_ref, ws_ref, bs_ref, o_ref, side_ref, hn_ref = rest
    i = pl.program_id(1)
    n = pl.program_id(2)
    tm = x_ref.shape[1]

    def normed():
        x = x_ref[0]
        y = x * lax.rsqrt(jnp.mean(x * x, axis=-1, keepdims=True) + EPS) * nw_ref[...]
        row = i * tm + lax.broadcasted_iota(jnp.int32, (tm, 1), 0)
        is_ctx = row < n_ctx
        shift = jnp.where(is_ctx, mod_ref[0, 0:1, :], mod_ref[0, 2:3, :])
        scale = jnp.where(is_ctx, mod_ref[0, 1:2, :], mod_ref[0, 3:4, :])
        return (y * (1.0 + scale) + shift).astype(BF16)

    def project(h, rotate):
        acc = jnp.dot(h, w_ref[...], preferred_element_type=F32) + bias_ref[...]
        if not rotate:
            o_ref[0] = acc.astype(o_ref.dtype)
            return
        _, q_col, k_col = rope_cols
        ml_w = ML_HEADS * ML_DIM
        cos, sin = cos_ref[...], sin_ref[...]
        for c in range(0, acc.shape[1], ML_DIM):
            cs = slice(c, c + ML_DIM)
            if q_col <= c < q_col + ml_w:
                o_ref[0, :, cs] = _rope(acc[:, cs], cos, sin).astype(o_ref.dtype)
            elif k_col <= c < k_col + ml_w:
                o_ref[0, :, cs] = (_rope(acc[:, cs], cos, sin) * (ML_DIM ** -0.5)).astype(o_ref.dtype)
            else:
                o_ref[0, :, cs] = acc[:, cs].astype(o_ref.dtype)

    rope_tile = None if rope_cols is None else rope_cols[0]

    @pl.when(n == 0)
    def _():
        h = normed()
        hn_ref[...] = h
        project(h, rope_tile == 0)
        if rope_cols is not None:
            side_ref[0] = jnp.dot(h, ws_ref[...], preferred_element_type=F32) + bs_ref[...]

    if rope_tile not in (None, 0):
        @pl.when(n == rope_tile)
        def _():
            project(hn_ref[...], True)

    @pl.when((n != 0) if rope_tile in (None, 0) else ((n != 0) & (n != rope_tile)))
    def _():
        project(hn_ref[...], False)


def _inproj(xc, norm_w, mod, w, bias, n_ctx, tm, tn, out_dtype, rope=None, side=None):
    B, S, D = xc.shape
    N = w.shape[1]
    in_specs = [pl.BlockSpec((1, tm, D), lambda b, i, n: (b, i, 0)),
                pl.BlockSpec((1, D), lambda b, i, n: (0, 0)),
                pl.BlockSpec((1, 4, D), lambda b, i, n: (b, 0, 0)),
                pl.BlockSpec((D, tn), lambda b, i, n: (0, n)),
                pl.BlockSpec((1, tn), lambda b, i, n: (0, n))]
    args = [xc, norm_w.reshape(1, D), mod, w, bias]
    out_specs = pl.BlockSpec((1, tm, tn), lambda b, i, n: (b, i, n))
    out_shape = jax.ShapeDtypeStruct((B, S, N), out_dtype)
    if rope is not None:
        in_specs += [pl.BlockSpec((tm, ML_DIM), lambda b, i, n: (i, 0))] * 2
        in_specs += [pl.BlockSpec((D, LANES), lambda b, i, n: (0, 0)), pl.BlockSpec((1, LANES), lambda b, i, n: (0, 0))]
        args += list(rope[:2]) + list(side)
        out_specs = (out_specs, pl.BlockSpec((1, tm, LANES), lambda b, i, n: (b, i, 0)))
        out_shape = (out_shape, jax.ShapeDtypeStruct((B, S, LANES), F32))
    return pl.pallas_call(
        functools.partial(_inproj_kernel, n_ctx=n_ctx, rope_cols=None if rope is None else tuple(rope[2:])),
        grid=(B, S // tm, N // tn),
        in_specs=in_specs,
        out_specs=out_specs,
        out_shape=out_shape,
        scratch_shapes=[pltpu.VMEM((tm, D), BF16)],
        compiler_params=_params("parallel", "parallel", "arbitrary"), name="inproj",
    )(*args)


def _na_bias_tables(rpb, rows):
    H, nri, nci = rpb.shape
    qc, kc = np.arange(GRID_W)[:, None], np.arange(GRID_W)[None, :]
    c0 = np.clip(qc - NA_WIN_W // 2, 0, GRID_W - NA_WIN_W)
    col_ok = (kc >= c0) & (kc < c0 + NA_WIN_W)
    col_sel = ((kc - qc + NA_WIN_W - 1)[..., None] == np.arange(nci)) & col_ok[..., None]
    sel = np.einsum("de,qkj->qdkej", np.eye(2), col_sel).reshape(GRID_W, 2, GRID_W, 2 * nci).astype(np.float32)
    rp = jnp.pad(rpb.astype(F32), ((0, 0), (1, 1), (0, 0)))
    pair = jnp.stack([rp[:, :nri + 1], rp[:, 1:]], axis=2).reshape(H, nri + 1, 2 * nci)
    tiles = jnp.einsum("hic,qdkc->hiqdk", pair, sel, precision=HIGHEST).reshape(H, nri + 1, GRID_W, 2 * GRID_W)

    qr, ka = np.arange(NA_QROWS)[:, None], np.arange(NA_KROWS)[None, :]
    nblk = rows // NA_QROWS
    tables = []
    for blk in (0, 1, nblk - 1):
        kr0 = int(np.clip(NA_QROWS * blk - NA_WIN_H // 2, 0, rows - NA_KROWS))
        r = NA_QROWS * blk + qr
        r0 = np.clip(r - NA_WIN_H // 2, 0, rows - NA_WIN_H)
        krow = kr0 + ka
        row_ok = (krow >= r0) & (krow < r0 + NA_WIN_H)
        idx = np.clip((krow - r + NA_WIN_H - 1)[:, 0::2] + 1, 0, nri)
        valid = (row_ok.reshape(NA_QROWS, NA_KROWS // 2, 1, 2, 1) & col_ok[None, None, :, None, :])
        valid = valid.reshape(NA_QROWS, NA_KROWS // 2, GRID_W, 2 * GRID_W)
        tables.append(jnp.where(valid[None], tiles[:, idx] * LOG2E, NEG))
    return jnp.stack(tables)


def _na_kernel(q_ref, k_ref, v_ref, bias_ref, qn_ref, kn_ref, ones_ref, o_ref, kt_s, q0_s, q1_s, v0_s, v1_s,
               la0_s, la1_s, lb0_s, lb1_s, *, n_ctx, rows):
    la_s, lb_s = (la0_s, la1_s), (lb0_s, lb1_s)
    nq = NA_QROWS * GRID_W
    win = NA_KROWS * GRID_W
    nblk = rows // NA_QROWS
    lane = lax.broadcasted_iota(jnp.int32, (1, LANES), 1)
    in_h0 = lane < NA_DIM
    qh_s, vh_s = (q0_s, q1_s), (v0_s, v1_s)
    den_lane = (NA_DIM, 0)

    def rms(x, w):
        ss = jnp.dot((x * x).astype(BF16), ones_ref[...], preferred_element_type=F32)
        return x * lax.rsqrt(ss * (1.0 / NA_DIM) + EPS) * w

    kt_s[...] = rms(k_ref[0].astype(F32), kn_ref[...]).T.astype(BF16)
    q = rms(q_ref[0].astype(F32), qn_ref[...]) * (NA_DIM ** -0.5 * LOG2E)
    q0_s[...] = jnp.where(in_h0, q, 0.0).astype(BF16)
    q1_s[...] = jnp.where(in_h0, 0.0, q).astype(BF16)
    v = v_ref[0].astype(F32)
    v0_s[...] = jnp.where(lane == den_lane[0], 1.0, v).astype(BF16)
    v1_s[...] = jnp.where(lane == den_lane[1], 1.0, v).astype(BF16)

    def softmax_pv(h, s_parts, bias, v_starts):
        if bias is not None:
            s_parts = [s_parts[0] + bias] + s_parts[1:]
        m = functools.reduce(jnp.maximum, [jnp.max(s, axis=1, keepdims=True) for s in s_parts])
        return sum(jnp.dot(jnp.exp2(s - m).astype(BF16), vh_s[h][pl.ds(st, s.shape[1]), :],
                           preferred_element_type=F32) for s, st in zip(s_parts, v_starts))

    def store(q_start, o):
        den = [o[h][:, den_lane[h]:den_lane[h] + 1] for h in range(2)]
        o_ref[0, pl.ds(q_start, nq), :] = jnp.where(in_h0, o[0] / den[0], o[1] / den[1]).astype(o_ref.dtype)

    kc = kt_s[:, 0:n_ctx]
    s_ctx = [jnp.dot(qh_s[h][0:n_ctx, :], kc, preferred_element_type=F32) for h in range(2)]
    store(0, [softmax_pv(h, [s_ctx[h]], None, [0]) for h in range(2)])

    def block_pos(blk):
        q_start = pl.multiple_of(n_ctx + nq * blk, nq)
        kr0 = jnp.clip(NA_QROWS * blk - NA_WIN_H // 2, 0, rows - NA_KROWS)
        return q_start, pl.multiple_of(n_ctx + GRID_W * kr0, nq)

    def logits_to(bufs, blk):
        q_start, k_start = block_pos(blk)
        for h in range(2):
            qh = qh_s[h][pl.ds(q_start, nq), :]
            bufs[h][:, 0:win] = jnp.dot(qh, kt_s[:, pl.ds(k_start, win)], preferred_element_type=F32)
            bufs[h][:, win:win + n_ctx] = jnp.dot(qh, kc, preferred_element_type=F32)

    def attend_from(bufs, blk):
        q_start, k_start = block_pos(blk)
        case = jnp.where(blk == 0, 0, jnp.where(blk == nblk - 1, 2, 1))
        o = []
        for h in range(2):
            tiles = bias_ref[case, h]
            bias = jnp.concatenate([jnp.concatenate([tiles[r, a] for a in range(NA_KROWS // 2)], axis=1)
                                    for r in range(NA_QROWS)], axis=0)
            o.append(softmax_pv(h, [bufs[h][:, 0:win], bufs[h][:, win:win + n_ctx]], bias, [k_start, 0]))
        store(q_start, o)

    logits_to(la_s, 0)

    def two_blocks(it, carry):
        b0 = 2 * it
        logits_to(lb_s, b0 + 1)
        attend_from(la_s, b0)
        logits_to(la_s, jnp.minimum(b0 + 2, nblk - 1))
        attend_from(lb_s, b0 + 1)
        return carry

    lax.fori_loop(0, nblk // 2, two_blocks, 0)


def _na_attention(P, bias_tab, q_norm, k_norm, n_ctx):
    B, S, _ = P.shape
    width = NA_HEADS * NA_DIM
    npair = width // LANES
    nq = NA_QROWS * GRID_W
    rows = (S - n_ctx) // GRID_W
    assert n_ctx == nq and rows % (2 * NA_QROWS) == 0
    win = NA_KROWS * GRID_W
    hp = LANES // NA_DIM
    ones = jnp.asarray(np.kron(np.eye(hp), np.ones((NA_DIM, NA_DIM))), BF16)
    qn = jnp.tile(q_norm, hp).reshape(1, LANES)
    kn = jnp.tile(k_norm, hp).reshape(1, LANES)
    seq = lambda c: pl.BlockSpec((1, S, LANES), lambda b, p: (b, 0, c * npair + p))
    vec = pl.BlockSpec((1, LANES), lambda b, p: (0, 0))
    return pl.pallas_call(
        functools.partial(_na_kernel, n_ctx=n_ctx, rows=rows),
        grid=(B, npair),
        in_specs=[seq(0), seq(1), seq(2),
                  pl.BlockSpec((bias_tab.shape[0], hp) + bias_tab.shape[2:], lambda b, p: (0, p, 0, 0, 0, 0)),
                  vec, vec, pl.BlockSpec((LANES, LANES), lambda b, p: (0, 0))],
        out_specs=seq(0),
        out_shape=jax.ShapeDtypeStruct((B, S, width), BF16),
        scratch_shapes=([pltpu.VMEM((LANES, S), BF16)] + [pltpu.VMEM((S, LANES), BF16)] * 4
                        + [pltpu.VMEM((nq, win + n_ctx), F32)] * 4),
        compiler_params=_params("parallel", "parallel"), name="na_attention",
    )(P, P, P, bias_tab, qn, kn, ones)


def _rope_tables(n_ctx, T):
    half = ML_DIM // 4
    t = jnp.arange(T)
    freqs = ROPE_BASE ** (-jnp.arange(half, dtype=F32) / half)
    ang_r = (t // GRID_W).astype(F32)[:, None] * freqs[None, :]
    ang_c = (t % GRID_W).astype(F32)[:, None] * freqs[None, :]
    cos = jnp.concatenate([jnp.cos(ang_r)] * 2 + [jnp.cos(ang_c)] * 2, axis=-1)
    sin = jnp.concatenate([-jnp.sin(ang_r), jnp.sin(ang_r), -jnp.sin(ang_c), jnp.sin(ang_c)], axis=-1)
    cos = jnp.concatenate([jnp.ones((n_ctx, ML_DIM), F32), cos], axis=0)
    sin = jnp.concatenate([jnp.zeros((n_ctx, ML_DIM), F32), sin], axis=0)
    return cos, sin


def _rope(x, cos, sin):
    xr = jnp.concatenate([pltpu.roll(x[:, a * LANES:(a + 1) * LANES], LANES // 2, axis=1)
                          for a in range(x.shape[1] // LANES)], axis=1)
    return x * cos + xr * sin


def _mlstm_kernel(q_ref, k_ref, v_ref, g_ref, o_ref, c_ref, n_ref, m_ref, *, reverse):
    L = q_ref.shape[1]

    @pl.when(pl.program_id(1) == 0)
    def _():
        c_ref[...] = jnp.zeros_like(c_ref)
        n_ref[...] = jnp.zeros_like(n_ref)
        m_ref[...] = jnp.zeros_like(m_ref)

    t_idx = lax.broadcasted_iota(jnp.int32, (L, L), 0)
    s_idx = lax.broadcasted_iota(jnp.int32, (L, L), 1)
    mask = (s_idx >= t_idx) if reverse else (s_idx <= t_idx)
    tri = mask.astype(BF16)
    last = 0 if reverse else L - 1

    g = g_ref[0]
    lsg = _log_sigmoid(g)
    g_t = g.T
    b_cols = _cumsum_rows(tri, lsg)
    b_rows = _cumsum_cols(lsg.T, tri)
    ones = jnp.ones((L, LANES), BF16)

    heads = range(ML_HEADS)
    hs = [slice(h * ML_DIM, (h + 1) * ML_DIM) for h in heads]
    qb = [q_ref[0, :, hs[h]] for h in heads]
    kb = [k_ref[0, :, hs[h]] for h in heads]
    vb = [v_ref[0, :, hs[h]] for h in heads]
    qk = [lax.dot_general(qb[h], kb[h], NT, preferred_element_type=F32) for h in heads]
    qc = [jnp.dot(qb[h], c_ref[h].astype(BF16), preferred_element_type=F32) for h in heads]
    qn = [jnp.dot(qb[h], n_ref[h].astype(BF16), preferred_element_type=F32) for h in heads]
    kwb, decay, m_t, carry_w, m_new, dec = ([None] * ML_HEADS for _ in range(6))
    for h in heads:
        ii = (2 if reverse else 0) * ML_HEADS + h
        fi = (3 if reverse else 1) * ML_HEADS + h
        ig_col, ig_row = g[:, ii:ii + 1], g_t[ii:ii + 1, :]
        b_col, b_row = b_cols[:, fi:fi + 1], b_rows[fi:fi + 1, :]
        b_last = b_col[last:last + 1, :]
        m_prev = m_ref[h, 0:1, 0:1]
        dmat = jnp.where(mask, b_col - b_row + ig_row, NEG)
        m_inter = b_col + m_prev
        m_t[h] = jnp.maximum(jnp.max(dmat, axis=1, keepdims=True), m_inter)
        decay[h] = jnp.exp(dmat - m_t[h])
        carry_w[h] = jnp.broadcast_to(jnp.exp(m_inter - m_t[h]), (L, LANES))
        g_col = b_last - b_col + ig_col
        m_new[h] = jnp.maximum(b_last + m_prev, jnp.max(g_col, axis=0, keepdims=True))
        kwb[h] = (kb[h].astype(F32) * jnp.exp(g_col - m_new[h])).astype(BF16)
        dec[h] = jnp.exp(b_last + m_prev - m_new[h])

    upd = [lax.dot_general(kwb[h], vb[h], TN, preferred_element_type=F32) for h in heads]
    kw_sum = [lax.dot_general(kwb[h], ones, TN, preferred_element_type=F32) for h in heads]
    sb = [(qk[h] * decay[h]).astype(BF16) for h in heads]
    sv = [jnp.dot(sb[h], vb[h], preferred_element_type=F32) for h in heads]
    s_sum = [jnp.dot(sb[h], ones, preferred_element_type=F32) for h in heads]
    for h in heads:
        den = s_sum[h] + carry_w[h] * qn[h]
        inv = 1.0 / jnp.maximum(jnp.abs(den), jnp.exp(-m_t[h]))
        num = sv[h] + jnp.concatenate([carry_w[h]] * (ML_DIM // LANES), axis=1) * qc[h]
        o_ref[0, :, hs[h]] = (num * jnp.concatenate([inv] * (ML_DIM // LANES), axis=1)).astype(o_ref.dtype)
        c_ref[h] = dec[h] * c_ref[h] + upd[h]
        n_ref[h] = dec[h] * n_ref[h] + kw_sum[h]
        m_ref[h] = jnp.broadcast_to(m_new[h], m_ref.shape[1:])


def _mlstm_scan(P, gates, n_ctx, reverse):
    B, S, _ = P.shape
    L = ML_CHUNK
    width = ML_HEADS * ML_DIM
    assert n_ctx == L and S % L == 0
    nch = S // L

    def cidx(j):
        return jnp.where(j == 0, 0, nch - j) if reverse else j

    return pl.pallas_call(
        functools.partial(_mlstm_kernel, reverse=reverse),
        grid=(B, nch),
        in_specs=[pl.BlockSpec((1, L, width), lambda b, j: (b, cidx(j), 4)),
                  pl.BlockSpec((1, L, width), lambda b, j: (b, cidx(j), 5)),
                  pl.BlockSpec((1, L, width), lambda b, j: (b, cidx(j), 6)),
                  pl.BlockSpec((1, L, LANES), lambda b, j: (b, cidx(j), 0))],
        out_specs=pl.BlockSpec((1, L, width), lambda b, j: (b, cidx(j), 0)),
        out_shape=jax.ShapeDtypeStruct((B, S, width), BF16),
        scratch_shapes=[pltpu.VMEM((ML_HEADS, ML_DIM, ML_DIM), F32),
                        pltpu.VMEM((ML_HEADS, ML_DIM, LANES), F32),
                        pltpu.VMEM((ML_HEADS, 8, LANES), F32)],
        compiler_params=_params("parallel", "arbitrary"), name="mlstm_bwd" if reverse else "mlstm_fwd",
    )(P, P, P, gates)


def _merge_ab_kernel(oa_ref, za_ref, hf_ref, hb_ref, ob_ref, zb_ref, x_ref, gate_ref, hn_ref, w_ref, o_ref, *, n_ctx):
    tm = x_ref.shape[1]
    width = oa_ref.shape[2]
    ya = oa_ref[0] * _silu(za_ref[0])
    hb = (hf_ref[0].astype(F32) + hb_ref[0].astype(F32)) * _sigmoid(ob_ref[0]).astype(F32)
    parts = []
    for h in range(ML_HEADS):
        seg = hb[:, h * ML_DIM:(h + 1) * ML_DIM]
        parts.append(seg * lax.rsqrt(jnp.mean(seg * seg, axis=1, keepdims=True) + EPS))
    yb = jnp.concatenate(parts, axis=1) * hn_ref[...] * _silu(zb_ref[0]).astype(F32)
    y = (jnp.dot(ya, w_ref[0:width, :], preferred_element_type=F32)
         + jnp.dot(yb.astype(BF16), w_ref[width:2 * width, :], preferred_element_type=F32))
    row = pl.program_id(1) * tm + lax.broadcasted_iota(jnp.int32, (tm, 1), 0)
    gate = jnp.where(row < n_ctx, gate_ref[0, 0:1, :], gate_ref[0, 1:2, :])
    o_ref[0] = x_ref[0] + gate * y


def _merge_ab(oa, hf, hb, P, xc, gate, h_norm, w_out, n_ctx, tm):
    B, S, D = xc.shape
    width = oa.shape[2]
    tok = lambda c: pl.BlockSpec((1, tm, width), lambda b, i: (b, i, c))
    return pl.pallas_call(
        functools.partial(_merge_ab_kernel, n_ctx=n_ctx),
        grid=(B, S // tm),
        in_specs=[tok(0), tok(3), tok(0), tok(0), tok(7), tok(8),
                  pl.BlockSpec((1, tm, D), lambda b, i: (b, i, 0)),
                  pl.BlockSpec((1, 2, D), lambda b, i: (b, 0, 0)),
                  pl.BlockSpec((1, width), lambda b, i: (0, 0)),
                  pl.BlockSpec((2 * width, D), lambda b, i: (0, 0))],
        out_specs=pl.BlockSpec((1, tm, D), lambda b, i: (b, i, 0)),
        out_shape=jax.ShapeDtypeStruct((B, S, D), F32),
        compiler_params=_params("parallel", "parallel"), name="merge_ab",
    )(oa, P, hf, hb, P, P, xc, gate, h_norm.reshape(1, width), w_out)


def _gla_kernel(q_ref, f_ref, v_ref, lb_ref, o_ref, st_ref, qm_s, km_s, ev_s, *, reverse):
    C = HG_CHUNK
    T = q_ref.shape[1]
    nsub = T // C
    gw = q_ref.shape[2]

    @pl.when(pl.program_id(2) == 0)
    def _():
        st_ref[...] = jnp.zeros_like(st_ref)
        qm_s[...] = jnp.zeros_like(qm_s)
        km_s[...] = jnp.zeros_like(km_s)
        ev_s[...] = jnp.zeros_like(ev_s)

    def causal(n, same_chunk):
        t_idx = lax.broadcasted_iota(jnp.int32, (n, n), 0)
        s_idx = lax.broadcasted_iota(jnp.int32, (n, n), 1)
        m = (s_idx >= t_idx) if reverse else (s_idx <= t_idx)
        return m & (t_idx // C == s_idx // C) if same_chunk else m

    mask = causal(C, False)
    tri = causal(T, True).astype(BF16)
    last = 0 if reverse else C - 1
    mid = C // 2 if reverse else C // 2 - 1
    order = list(range(nsub - 1, -1, -1) if reverse else range(nsub))
    rows = [slice(c * C, (c + 1) * C) for c in range(nsub)]
    hw = gw // 2
    nh = hw // HG_DIM

    def decay_stage(half_idx):
        cols = slice(half_idx * hw, (half_idx + 1) * hw)
        lb = lb_ref[:, cols]
        c1 = 0.5 * (1.0 - lb)
        f = (lb + c1) + c1 * jnp.tanh(0.5 * f_ref[0, :, cols].astype(F32))
        b = _cumsum_rows(tri, jnp.log(f), terms=2)
        b_mid = [b[c * C + mid:c * C + mid + 1, :] for c in range(nsub)]
        b_last = [b[c * C + last:c * C + last + 1, :] for c in range(nsub)]
        b_mid_rows = jnp.concatenate([jnp.broadcast_to(r, (C, hw)) for r in b_mid], axis=0)
        qm = _silu(q_ref[0, :, cols].astype(F32)) * jnp.exp(b - b_mid_rows)
        km = (1.0 - f) * jnp.exp(b_mid_rows - b)
        return dict(cols=cols, qm=qm, km=km,
                    e_mid=[jnp.exp(r) for r in b_mid],
                    e_last_mid=[jnp.exp(l - r) for l, r in zip(b_last, b_mid)],
                    dec=[jnp.exp(l) for l in b_last])

    ev_names = ("e_mid", "e_last_mid", "dec")

    def pending(half_idx):
        cols = slice(half_idx * hw, (half_idx + 1) * hw)
        d = dict(cols=cols, qm=qm_s[:, cols], km=km_s[:, cols], vb=v_ref[0, :, cols])
        for n, name in enumerate(ev_names):
            d[name] = [ev_s[n, c:c + 1, cols] for c in range(nsub)]
        return d

    def stash(d):
        qm_s[:, d["cols"]] = d["qm"]
        km_s[:, d["cols"]] = d["km"]
        for n, name in enumerate(ev_names):
            for c in range(nsub):
                ev_s[n, c:c + 1, d["cols"]] = d[name][c]

    def score_stage(d):
        qmb, kmb, vb = d["qm"].astype(BF16), d["km"].astype(BF16), d["vb"]
        a, upd = {}, {}
        for g in range(nh):
            sl = slice(g * HG_DIM, (g + 1) * HG_DIM)
            for c, rs in enumerate(rows):
                s = lax.dot_general(qmb[rs, sl], kmb[rs, sl], NT, preferred_element_type=F32)
                a[g, c] = jnp.where(mask, s, 0.0).astype(BF16)
                kh = (d["km"][rs, sl] * d["e_last_mid"][c][:, sl]).astype(BF16)
                upd[g, c] = lax.dot_general(vb[rs, sl], kh, TN, preferred_element_type=F32)
        d["upd"] = upd
        d["intra"] = {(g, c): jnp.dot(a[g, c], vb[rows[c], g * HG_DIM:(g + 1) * HG_DIM],
                                      preferred_element_type=F32) for (g, c) in a}

    def state_stage(half_idx, d):
        for g in range(nh):
            sl = slice(g * HG_DIM, (g + 1) * HG_DIM)
            out_cols = slice(half_idx * hw + g * HG_DIM, half_idx * hw + (g + 1) * HG_DIM)
            st = st_ref[half_idx * nh + g]
            for c in order:
                rs = rows[c]
                qi = (d["qm"][rs, sl] * d["e_mid"][c][:, sl]).astype(BF16)
                inter_t = lax.dot_general(st.astype(BF16), qi, NT, preferred_element_type=F32)
                o_ref[0, rs, out_cols] = (d["intra"][g, c] + inter_t.T).astype(o_ref.dtype)
                st = st * d["dec"][c][:, sl] + d["upd"][g, c]
            st_ref[half_idx * nh + g] = st

    previous = [pending(0), pending(1)]
    score_stage(previous[0])
    score_stage(previous[1])
    state_stage(0, previous[0])
    state_stage(1, previous[1])
    stash(decay_stage(0))
    stash(decay_stage(1))


def _gla_scan(P, lb, n_ctx, reverse):
    B, S, _ = P.shape
    C = HG_BLOCK
    width = HG_HEADS * HG_DIM
    gw = HG_GROUP * HG_DIM
    ngrp = width // gw
    assert n_ctx % C == 0 and S % C == 0 and C % HG_CHUNK == 0
    nch, nctx_ch = S // C, n_ctx // C
    fcol = (2 if reverse else 1) * ngrp

    def cidx(j):
        if not reverse:
            return j
        return jnp.where(j < nctx_ch, nctx_ch - 1 - j, nch + nctx_ch - 1 - j)

    def oidx(j):
        return jnp.maximum(cidx(j) - nctx_ch, 0) if not reverse else jnp.where(j < nctx_ch, nch - nctx_ch - 1,
                                                                                cidx(j) - nctx_ch)

    head = lambda j: cidx(jnp.minimum(j, nch - 1))
    tail = lambda j: jnp.maximum(j - 1, 0)
    return pl.pallas_call(
        functools.partial(_gla_kernel, reverse=reverse),
        grid=(B, ngrp, nch + 1),
        in_specs=[pl.BlockSpec((1, C, gw), lambda b, g, j: (b, head(j), g)),
                  pl.BlockSpec((1, C, gw), lambda b, g, j: (b, head(j), fcol + g)),
                  pl.BlockSpec((1, C, gw), lambda b, g, j: (b, cidx(tail(j)), 3 * ngrp + g)),
                  pl.BlockSpec((1, gw), lambda b, g, j: (0, g))],
        out_specs=pl.BlockSpec((1, C, gw), lambda b, g, j: (b, oidx(tail(j)), g)),
        out_shape=jax.ShapeDtypeStruct((B, S - n_ctx, width), BF16),
        scratch_shapes=[pltpu.VMEM((HG_GROUP, HG_DIM, HG_DIM), F32),
                        pltpu.VMEM((C, gw), F32), pltpu.VMEM((C, gw), F32),
                        pltpu.VMEM((3, 8, gw), F32)],
        compiler_params=_params("parallel", "parallel", "arbitrary"), name="gla_bwd" if reverse else "gla_fwd",
    )(P, P, P, lb)


def _merge_c_kernel(of_ref, ob_ref, z_ref, x_ref, gate_ref, hn_ref, w_ref, o_ref):
    o = of_ref[0].astype(F32) + ob_ref[0].astype(F32)
    parts = []
    for h in range(HG_HEADS):
        seg = o[:, h * HG_DIM:(h + 1) * HG_DIM]
        parts.append(seg * lax.rsqrt(jnp.mean(seg * seg, axis=1, keepdims=True) + EPS))
    y = jnp.concatenate(parts, axis=1) * hn_ref[...] * _silu(z_ref[0]).astype(F32)
    o_ref[0] = x_ref[0] + gate_ref[0] * jnp.dot(y.astype(BF16), w_ref[...], preferred_element_type=F32)


def _merge_c(of, ob, P, xc, gate, h_norm, w_out, n_ctx, tm):
    B, T, width = of.shape
    D = xc.shape[2]
    off = n_ctx // tm
    return pl.pallas_call(
        _merge_c_kernel,
        grid=(B, T // tm),
        in_specs=[pl.BlockSpec((1, tm, width), lambda b, i: (b, i, 0)),
                  pl.BlockSpec((1, tm, width), lambda b, i: (b, i, 0)),
                  pl.BlockSpec((1, tm, width), lambda b, i: (b, i + off, 4)),
                  pl.BlockSpec((1, tm, D), lambda b, i: (b, i + off, 0)),
                  pl.BlockSpec((1, 1, D), lambda b, i: (b, 0, 0)),
                  pl.BlockSpec((1, width), lambda b, i: (0, 0)),
                  pl.BlockSpec((width, D), lambda b, i: (0, 0))],
        out_specs=pl.BlockSpec((1, tm, D), lambda b, i: (b, i, 0)),
        out_shape=jax.ShapeDtypeStruct((B, T, D), F32),
        compiler_params=_params("parallel", "parallel"), name="merge_c",
    )(of, ob, P, xc, gate, h_norm.reshape(1, width), w_out)


def _pad_cols(w, n):
    return jnp.pad(w, ((0, 0), (0, n - w.shape[1])))


def kernel(x, c, ctx, c_ctx, norm_w, w_ada, b_ada, w_in_ab, b_gate_ab, q_norm_a, k_norm_a, rpb_a, h_norm_b, w_out_ab,
           w_in_c, lb_c, h_norm_c, w_out_c):
    B, T, D = x.shape
    n_ctx = ctx.shape[1]
    S = n_ctx + T
    tm_in = S // 4
    tm_out = S // 16
    xc = jnp.concatenate([ctx, x], axis=1)
    cvec = jnp.concatenate([c, c_ctx[None], jnp.zeros((16 - B - 1, D), F32)], axis=0)

    def modulation(l):
        m = _ada(cvec, w_ada[l], b_ada[l])
        shift, scale, gate = m[:, :D], m[:, D:2 * D], m[:, 2 * D:]
        ctx_row = lambda a: jnp.broadcast_to(a[B][None], (B, D))
        mod = jnp.stack([ctx_row(shift), ctx_row(scale), shift[:B], scale[:B]], axis=1)
        return mod, jnp.stack([ctx_row(gate), gate[:B]], axis=1)

    mod, gate = modulation(0)
    n_main = 9 * NA_HEADS * NA_DIM
    w_main = w_in_ab[0][:, :n_main].astype(BF16)
    w_gate = _pad_cols(w_in_ab[0][:, n_main:], LANES).astype(BF16)
    b_gate = _pad_cols(b_gate_ab[0][None], LANES)
    ml_w = ML_HEADS * ML_DIM
    P, gates = _inproj(xc, norm_w[0], mod, w_main, jnp.zeros((1, n_main), F32), n_ctx, tm_in, n_main // 3, BF16,
                       rope=_rope_tables(n_ctx, T) + (1, ml_w, 2 * ml_w), side=(w_gate, b_gate))
    oa = _na_attention(P, _na_bias_tables(rpb_a[0], T // GRID_W), q_norm_a[0], k_norm_a[0], n_ctx)
    hf = _mlstm_scan(P, gates, n_ctx, reverse=False)
    hb = _mlstm_scan(P, gates, n_ctx, reverse=True)
    xc = _merge_ab(oa, hf, hb, P, xc, gate, h_norm_b[0], w_out_ab[0].astype(BF16), n_ctx, tm_out)

    mod, gate = modulation(1)
    sm = jax.nn.softmax(lb_c.astype(F32), axis=0)
    lb = (jnp.cumsum(sm, axis=0) - sm[0])[1].reshape(1, -1)
    hw = HG_HEADS * HG_DIM
    P = _inproj(xc, norm_w[1], mod, w_in_c[0].astype(BF16), jnp.zeros((1, 5 * hw), F32), n_ctx, tm_in, hw, BF16)
    of = _gla_scan(P, lb, n_ctx, reverse=False)
    ob = _gla_scan(P, lb, n_ctx, reverse=True)
    return _merge_c(of, ob, P, xc, gate[:, 1:2], h_norm_c[0], w_out_c[0].astype(BF16), n_ctx, n_ctx)
```
